```python
import jax, jax.numpy as jnp
from jax import lax
import numpy as np

D_MODEL = 1024
BATCH = 16
SEQ = 4096
DEPTH = 4

CHUNK = 64
Q_BLOCK = 128
FOX_HEADS = D_MODEL // 128
FOX_HD = 64
FOX_W = FOX_HEADS * FOX_HD
ML_HEADS = 4
ML_HD = D_MODEL // 8
ML_W = ML_HEADS * ML_HD
CONV_K = 4
N_MEM = 256
X_HEADS = 4
X_HD = D_MODEL // X_HEADS
D_FF = 4 * D_MODEL
N_BRANCH = 2
EPS = 1e-6
SPLIT_SIZES = (FOX_W, FOX_W, FOX_W, FOX_HEADS,
               2 * ML_W, ML_W, ML_W, ML_HEADS, ML_HEADS,
               N_BRANCH * D_MODEL)
IN_WIDTH = sum(SPLIT_SIZES)
SPLIT_POINTS = tuple(int(s) for s in np.cumsum(SPLIT_SIZES)[:-1])

kernel_name = "hybrid_fox_mlstm_gated_trunk"


def rms_norm(x, g):
    xf = x.astype(jnp.float32)
    y = xf * lax.rsqrt(jnp.mean(xf * xf, axis=-1, keepdims=True) + EPS)
    return (y * g.astype(jnp.float32)).astype(x.dtype)


def causal_conv(u, w, b):
    S = u.shape[1]
    up = jnp.pad(u, ((0, 0), (CONV_K - 1, 0), (0, 0)))
    out = b
    for j in range(CONV_K):
        out = out + w[j] * up[:, j:j + S]
    return out


def fox_attention(q, k, v, fpre):
    B, S, H, d = q.shape
    nq = S // Q_BLOCK
    f32 = jnp.float32
    qf = q.astype(f32).transpose(0, 2, 1, 3) * (d ** -0.5)
    kf = k.astype(f32).transpose(0, 2, 1, 3)
    vf = v.astype(f32).transpose(0, 2, 1, 3)
    c = jnp.cumsum(jax.nn.log_sigmoid(fpre.astype(f32)), axis=1).transpose(0, 2, 1)
    qb = qf.reshape(B, H, nq, Q_BLOCK, d).transpose(2, 0, 1, 3, 4)
    cb = c.reshape(B, H, nq, Q_BLOCK).transpose(2, 0, 1, 3)
    kpos = jnp.arange(S)

    def block(args):
        qi, ci, i = args
        qpos = i * Q_BLOCK + jnp.arange(Q_BLOCK)
        logits = jnp.einsum('bhqd,bhkd->bhqk', qi, kf) + ci[..., :, None] - c[..., None, :]
        logits = jnp.where(kpos[None, :] <= qpos[:, None], logits, -jnp.inf)
        p = jax.nn.softmax(logits, axis=-1)
        return jnp.einsum('bhqk,bhkd->bhqd', p, vf)

    o = lax.map(block, (qb, cb, jnp.arange(nq)))
    return o.transpose(1, 0, 3, 2, 4).reshape(B, S, H * d)


def mlstm_chunkwise(q, k, v, ipre, fpre):
    B, S, H, d = q.shape
    L = CHUNK
    nc = S // L
    f32 = jnp.float32
    def chunked(t):
        return t.astype(f32).transpose(0, 2, 1, 3).reshape(B, H, nc, L, d)
    qc, kc, vc = chunked(q), chunked(k) * (d ** -0.5), chunked(v)
    ig = ipre.astype(f32).transpose(0, 2, 1).reshape(B, H, nc, L)
    logf = jax.nn.log_sigmoid(fpre.astype(f32)).transpose(0, 2, 1).reshape(B, H, nc, L)
    bcum = jnp.cumsum(logf, axis=-1)
    bL = bcum[..., -1]
    a = bL[..., None] - bcum + ig
    m_loc = jnp.max(a, axis=-1)
    wloc = jnp.exp(a - m_loc[..., None])
    C_loc = jnp.einsum('bhcl,bhcld,bhcle->bhcde', wloc, kc, vc)
    n_loc = jnp.einsum('bhcl,bhcld->bhcd', wloc, kc)

    def step(carry, xs):
        C, n, m = carry
        bl, ml, Cl, nl = xs
        m_new = jnp.maximum(bl + m, ml)
        s_old = jnp.exp(bl + m - m_new)
        s_loc = jnp.exp(ml - m_new)
        C_new = s_old[..., None, None] * C + s_loc[..., None, None] * Cl
        n_new = s_old[..., None] * n + s_loc[..., None] * nl
        return (C_new, n_new, m_new), (C, n, m)

    init = (jnp.zeros((B, H, d, d), f32), jnp.zeros((B, H, d), f32), jnp.zeros((B, H), f32))
    xs = (jnp.moveaxis(bL, 2, 0), jnp.moveaxis(m_loc, 2, 0),
          jnp.moveaxis(C_loc, 2, 0), jnp.moveaxis(n_loc, 2, 0))
    _, (C_prev, n_prev, m_prev) = lax.scan(step, init, xs)
    C_prev = jnp.moveaxis(C_prev, 0, 2)
    n_prev = jnp.moveaxis(n_prev, 0, 2)
    m_prev = jnp.moveaxis(m_prev, 0, 2)

    tri = jnp.tril(jnp.ones((L, L), dtype=bool))
    Dlog = bcum[..., :, None] - bcum[..., None, :] + ig[..., None, :]
    Dlog = jnp.where(tri, Dlog, -jnp.inf)
    inter_log = bcum + m_prev[..., None]
    m_t = jnp.maximum(inter_log, jnp.max(Dlog, axis=-1))
    Dw = jnp.exp(Dlog - m_t[..., None])
    inter_w = jnp.exp(inter_log - m_t)
    sqk = jnp.einsum('bhcld,bhcsd->bhcls', qc, kc) * Dw
    num = inter_w[..., None] * jnp.einsum('bhcld,bhcde->bhcle', qc, C_prev) \
        + jnp.einsum('bhcls,bhcse->bhcle', sqk, vc)
    den = inter_w * jnp.einsum('bhcld,bhcd->bhcl', qc, n_prev) + jnp.sum(sqk, axis=-1)
    h = num / jnp.maximum(jnp.abs(den), jnp.exp(-m_t))[..., None]
    return h.reshape(B, H, S, d).transpose(0, 2, 1, 3)


def cross_attention(hn, memn, wq, wkv, wo):
    B, S, _ = hn.shape
    q = (hn @ wq).reshape(B, S, X_HEADS, X_HD)
    kv = memn @ wkv
    k, v = jnp.split(kv, 2, axis=-1)
    k = k.reshape(B, -1, X_HEADS, X_HD)
    v = v.reshape(B, -1, X_HEADS, X_HD)
    logits = jnp.einsum('bqhd,bkhd->bhqk', q, k).astype(jnp.float32) * (X_HD ** -0.5)
    p = jax.nn.softmax(logits, axis=-1).astype(v.dtype)
    o = jnp.einsum('bhqk,bkhd->bqhd', p, v).reshape(B, S, D_MODEL)
    return o @ wo


def setup_inputs(seed: int = 0) -> dict:
    key = jax.random.key(seed)
    ks = jax.random.split(key, 32)
    f32 = jnp.float32
    def nrm(k, shape, fan_in):
        return jax.random.normal(k, shape, f32) * (fan_in ** -0.5)
    def gain(k, shape):
        return 1.0 + 0.02 * jax.random.normal(k, shape, f32)
    return {
        "x": jax.random.normal(ks[0], (BATCH, SEQ, D_MODEL), f32),
        "mem": jax.random.normal(ks[1], (BATCH, N_MEM, D_MODEL), f32),
        "norm_mix": gain(ks[2], (DEPTH, D_MODEL)),
        "norm_xattn": gain(ks[3], (DEPTH, D_MODEL)),
        "norm_mem": gain(ks[4], (DEPTH, D_MODEL)),
        "norm_mlp": gain(ks[5], (DEPTH, D_MODEL)),
        "w_in": nrm(ks[6], (DEPTH, D_MODEL, IN_WIDTH), D_MODEL),
        "b_fox_f": jax.random.uniform(ks[7], (DEPTH, FOX_HEADS), f32, 2.0, 5.0),
        "b_mlstm_i": 0.1 * jax.random.normal(ks[8], (DEPTH, ML_HEADS), f32),
        "b_mlstm_f": jax.random.uniform(ks[9], (DEPTH, ML_HEADS), f32, 3.0, 6.0),
        "b_gate": 0.01 * jax.random.normal(ks[10], (DEPTH, N_BRANCH * D_MODEL), f32),
        "conv_w": nrm(ks[11], (DEPTH, CONV_K, 2 * ML_W), CONV_K),
        "conv_b": 0.01 * jax.random.normal(ks[12], (DEPTH, 2 * ML_W), f32),
        "mlstm_norm": gain(ks[13], (DEPTH, ML_W)),
        "w_up_fox": nrm(ks[14], (DEPTH, FOX_W, D_MODEL), FOX_W),
        "w_up_mlstm": nrm(ks[15], (DEPTH, ML_W, D_MODEL), ML_W),
        "w_mix_out": nrm(ks[16], (DEPTH, D_MODEL, D_MODEL), D_MODEL),
        "w_xq": nrm(ks[17], (DEPTH, D_MODEL, D_MODEL), D_MODEL),
        "w_xkv": nrm(ks[18], (DEPTH, D_MODEL, 2 * D_MODEL), D_MODEL),
        "w_xo": nrm(ks[19], (DEPTH, D_MODEL, D_MODEL), D_MODEL),
        "w_mlp_in": nrm(ks[20], (DEPTH, D_MODEL, D_FF), D_MODEL),
        "w_mlp_out": nrm(ks[21], (DEPTH, D_FF, D_MODEL), D_FF),
        "norm_final": gain(ks[22], (D_MODEL,)),
    }


def reference(x, mem, norm_mix, norm_xattn, norm_mem, norm_mlp, w_in, b_fox_f, b_mlstm_i,
              b_mlstm_f, b_gate, conv_w, conv_b, mlstm_norm, w_up_fox, w_up_mlstm, w_mix_out,
              w_xq, w_xkv, w_xo, w_mlp_in, w_mlp_out, norm_final):
    B, S, _ = x.shape
    h = x
    for l in range(DEPTH):
        xn = rms_norm(h, norm_mix[l])
        proj = xn @ w_in[l]
        fq, fk, fv, ff, mqk, mv, mo, mi, mf, gpre = jnp.split(proj, SPLIT_POINTS, axis=-1)
        y_fox = fox_attention(fq.reshape(B, S, FOX_HEADS, FOX_HD),
                              fk.reshape(B, S, FOX_HEADS, FOX_HD),
                              fv.reshape(B, S, FOX_HEADS, FOX_HD),
                              ff + b_fox_f[l]).astype(h.dtype)
        qk = jax.nn.silu(causal_conv(mqk, conv_w[l], conv_b[l]))
        mq, mk = jnp.split(qk, 2, axis=-1)
        hm = mlstm_chunkwise(mq.reshape(B, S, ML_HEADS, ML_HD),
                             mk.reshape(B, S, ML_HEADS, ML_HD),
                             mv.reshape(B, S, ML_HEADS, ML_HD),
                             mi + b_mlstm_i[l], mf + b_mlstm_f[l])
        hm = hm * lax.rsqrt(jnp.mean(hm * hm, axis=-1, keepdims=True) + EPS)
        hm = hm.reshape(B, S, ML_W) * mlstm_norm[l].astype(jnp.float32)
        y_ml = (hm * jax.nn.sigmoid(mo.astype(jnp.float32))).astype(h.dtype)
        g = jax.nn.sigmoid(gpre + b_gate[l])
        g_fox, g_ml = jnp.split(g, 2, axis=-1)
        merged = g_fox * (y_fox @ w_up_fox[l]) + g_ml * (y_ml @ w_up_mlstm[l])
        h = h + merged @ w_mix_out[l]
        hn = rms_norm(h, norm_xattn[l])
        memn = rms_norm(mem, norm_mem[l])
        h = h + cross_attention(hn, memn, w_xq[l], w_xkv[l], w_xo[l])
        hn = rms_norm(h, norm_mlp[l])
        a = jax.nn.relu(hn @ w_mlp_in[l])
        h = h + (a * a) @ w_mlp_out[l]
    return rms_norm(h, norm_final)
```

```python
import functools

import jax
import jax.numpy as jnp
from jax import lax
from jax.experimental import pallas as pl
from jax.experimental.pallas import tpu as pltpu

F32 = jnp.float32
BF16 = jnp.bfloat16

D_MODEL = 1024
FOX_HEADS = 8
FOX_HD = 64
FOX_W = FOX_HEADS * FOX_HD
ML_HEADS = 4
ML_HD = 128
ML_W = ML_HEADS * ML_HD
CONV_K = 4
X_HEADS = 4
X_HD = D_MODEL // X_HEADS
D_FF = 4 * D_MODEL
EPS = 1e-6

LANES = 128
V7X_VMEM_LIMIT_BYTES = 56 * 1024 * 1024

G_OFF = 0
MQK_OFF = G_OFF + 2 * D_MODEL
FQ_OFF = MQK_OFF + 2 * ML_W
FK_OFF = FQ_OFF + FOX_W
FV_OFF = FK_OFF + FOX_W
MV_OFF = FV_OFF + FOX_W
MO_OFF = MV_OFF + ML_W
PROJ_W = MO_OFF + ML_W
GATE_FF = 0
GATE_MI = GATE_FF + FOX_HEADS
GATE_MF = GATE_MI + ML_HEADS
GATE_USED = GATE_MF + ML_HEADS

ML_CHUNK = 256
NT_DIMS = (((1,), (1,)), ((), ()))
TN_DIMS = (((0,), (0,)), ((), ()))


def _params(*sem):
    return pltpu.CompilerParams(dimension_semantics=sem,
                                vmem_limit_bytes=V7X_VMEM_LIMIT_BYTES)


def _rms(x, g):
    return x * lax.rsqrt(jnp.mean(x * x, axis=-1, keepdims=True) + EPS) * g


def _norm_proj_gates_kernel(x_ref, g_ref, w_ref, wg_ref, o_ref, og_ref, xn_ref):
    @pl.when(pl.program_id(1) == 0)
    def _():
        xn = _rms(x_ref[...], g_ref[...]).astype(BF16)
        xn_ref[...] = xn
        og_ref[...] = jnp.dot(xn, wg_ref[...], preferred_element_type=F32)

    o_ref[...] = jnp.dot(xn_ref[...], w_ref[...],
                         preferred_element_type=F32).astype(BF16)


def _norm_proj_kernel(x_ref, g_ref, w_ref, o_ref, xn_ref):
    @pl.when(pl.program_id(1) == 0)
    def _():
        xn_ref[...] = _rms(x_ref[...], g_ref[...]).astype(BF16)

    o_ref[...] = jnp.dot(xn_ref[...], w_ref[...],
                         preferred_element_type=F32).astype(BF16)


def _norm_proj(x, gain, w, wg=None, *, tm, tn):
    m, d = x.shape
    n = w.shape[1]
    grid = (m // tm, n // tn)
    x_spec = pl.BlockSpec((tm, d), lambda i, j: (i, 0))
    g_spec = pl.BlockSpec((1, d), lambda i, j: (0, 0))
    w_spec = pl.BlockSpec((d, tn), lambda i, j: (0, j))
    o_spec = pl.BlockSpec((tm, tn), lambda i, j: (i, j))
    scratch = [pltpu.VMEM((tm, d), BF16)]
    if wg is None:
        return pl.pallas_call(
            _norm_proj_kernel, grid=grid,
            in_specs=[x_spec, g_spec, w_spec], out_specs=o_spec,
            out_shape=jax.ShapeDtypeStruct((m, n), BF16),
            scratch_shapes=scratch,
            compiler_params=_params("parallel", "arbitrary"),
            name="norm_proj")(x, gain, w)
    wg_spec = pl.BlockSpec((d, LANES), lambda i, j: (0, 0))
    og_spec = pl.BlockSpec((tm, LANES), lambda i, j: (i, 0))
    return pl.pallas_call(
        _norm_proj_gates_kernel, grid=grid,
        in_specs=[x_spec, g_spec, w_spec, wg_spec],
        out_specs=[o_spec, og_spec],
        out_shape=[jax.ShapeDtypeStruct((m, n), BF16),
                   jax.ShapeDtypeStruct((m, LANES), F32)],
        scratch_shapes=scratch,
        compiler_params=_params("parallel", "arbitrary"),
        name="norm_in_proj")(x, gain, w, wg)


def _split3(v):
    hi = v.astype(BF16)
    r1 = v - hi.astype(F32)
    mid = r1.astype(BF16)
    lo = (r1 - mid.astype(F32)).astype(BF16)
    return hi, mid, lo


def _gates_kernel(x_ref, b_ref, r_ref, ct_ref, carry_ref, *, tg):
    @pl.when(pl.program_id(1) == 0)
    def _():
        carry_ref[...] = jnp.zeros_like(carry_ref)

    c = ML_CHUNK
    row = lax.broadcasted_iota(jnp.int32, (c, c), 0)
    col = lax.broadcasted_iota(jnp.int32, (c, c), 1)
    tri = jnp.where(row >= col, 1.0, 0.0).astype(BF16)
    lane = lax.broadcasted_iota(jnp.int32, (c, LANES), 1)
    is_fox = lane < GATE_MI
    is_ig = (lane >= GATE_MI) & (lane < GATE_MF)
    is_used = lane < GATE_USED
    bias = b_ref[...]
    for blk in range(tg // c):
        x = x_ref[blk * c:(blk + 1) * c, :] + bias
        logsig = jnp.minimum(x, 0.0) - jnp.log1p(jnp.exp(-jnp.abs(x)))
        val = jnp.where(is_ig, x, logsig)
        val = jnp.where(is_used, val, 0.0)
        hi, mid, lo = _split3(val)
        cs = (jnp.dot(tri, hi, preferred_element_type=F32)
              + jnp.dot(tri, mid, preferred_element_type=F32)
              + jnp.dot(tri, lo, preferred_element_type=F32))
        glob = cs + carry_ref[0:1, :]
        carry_ref[0:1, :] = glob[c - 1:c, :]
        out = jnp.where(is_fox, glob, jnp.where(is_ig, val, cs))
        r_ref[blk * c:(blk + 1) * c, :] = out
        ct_ref[0, :, blk * c:(blk + 1) * c] = out.T[:GATE_USED, :]


def _gates(gpre, bias, *, batch, seq, tg):
    ns = seq // tg
    return pl.pallas_call(
        functools.partial(_gates_kernel, tg=tg),
        grid=(batch, ns),
        in_specs=[pl.BlockSpec((tg, LANES), lambda b, s: (b * ns + s, 0)),
                  pl.BlockSpec((1, LANES), lambda b, s: (0, 0))],
        out_specs=[pl.BlockSpec((tg, LANES), lambda b, s: (b * ns + s, 0)),
                   pl.BlockSpec((1, GATE_USED, tg), lambda b, s: (b, 0, s))],
        out_shape=[jax.ShapeDtypeStruct((batch * seq, LANES), F32),
                   jax.ShapeDtypeStruct((batch, GATE_USED, seq), F32)],
        scratch_shapes=[pltpu.VMEM((8, LANES), F32)],
        compiler_params=_params("parallel", "arbitrary"),
        name="gates")(gpre, bias)


def _fox_kernel(q_ref, k_ref, v_ref, ct_ref, o_ref, m_ref, l_ref, acc_ref, *, tq):
    i = pl.program_id(2)
    half = FOX_HD
    lane = lax.broadcasted_iota(jnp.int32, (tq, LANES), 1)
    first = lane < half
    q2 = q_ref[...] * (FOX_HD ** -0.5)
    zero = jnp.zeros_like(q2)
    qh = (jnp.where(first, q2, zero), jnp.where(first, zero, q2))

    m_ref[...] = jnp.full_like(m_ref, -jnp.inf)
    l_ref[...] = jnp.zeros_like(l_ref)
    acc_ref[...] = jnp.zeros_like(acc_ref)

    def step(j, masked):
        r0 = pl.multiple_of(j * tq, tq)
        ks = k_ref[pl.ds(r0, tq), :]
        vs = v_ref[pl.ds(r0, tq), :]
        cj = ct_ref[0, 0, :, pl.ds(r0, tq)]
        if masked:
            keep = (lax.broadcasted_iota(jnp.int32, (tq, tq), 1)
                    <= lax.broadcasted_iota(jnp.int32, (tq, tq), 0))
        for h in range(2):
            s = lax.dot_general(qh[h], ks, NT_DIMS, preferred_element_type=F32)
            t = s - cj[h:h + 1, :]
            if masked:
                t = jnp.where(keep, t, -jnp.inf)
            m_old = m_ref[h]
            m_new = jnp.maximum(m_old, jnp.max(t, axis=-1, keepdims=True))
            alpha = jnp.exp(m_old - m_new)
            pe = jnp.exp(t - m_new)
            l_ref[h] = alpha * l_ref[h] + jnp.sum(pe, axis=-1, keepdims=True)
            acc_ref[h] = alpha * acc_ref[h] + jnp.dot(
                pe.astype(BF16), vs, preferred_element_type=F32)
            m_ref[h] = m_new

    def body(j, carry):
        step(j, False)
        return carry

    lax.fori_loop(0, i, body, 0)
    step(i, True)

    out = jnp.where(first, acc_ref[0] / l_ref[0], acc_ref[1] / l_ref[1])
    o_ref[...] = out.astype(BF16)


def _fox(proj, ct, *, batch, seq, tq):
    nq = seq // tq
    npair = FOX_HEADS // 2
    ct_pairs = ct[:, :FOX_HEADS, :].reshape(batch, npair, 2, seq)
    qb, kb, vb = FQ_OFF // LANES, FK_OFF // LANES, FV_OFF // LANES
    return pl.pallas_call(
        functools.partial(_fox_kernel, tq=tq),
        grid=(batch, npair, nq),
        in_specs=[
            pl.BlockSpec((tq, LANES), lambda b, p, i: (b * nq + i, qb + p)),
            pl.BlockSpec((seq, LANES), lambda b, p, i: (b, kb + p)),
            pl.BlockSpec((seq, LANES), lambda b, p, i: (b, vb + p)),
            pl.BlockSpec((1, 1, 2, seq), lambda b, p, i: (b, p, 0, 0)),
        ],
        out_specs=pl.BlockSpec((tq, LANES), lambda b, p, i: (b * nq + i, p)),
        out_shape=jax.ShapeDtypeStruct((batch * seq, FOX_W), BF16),
        scratch_shapes=[pltpu.VMEM((2, tq, 1), F32),
                        pltpu.VMEM((2, tq, 1), F32),
                        pltpu.VMEM((2, tq, LANES), F32)],
        compiler_params=_params("parallel", "parallel", "arbitrary"),
        name="fox_attention")(proj, proj, proj, ct_pairs)


def _mlstm_kernel(u_ref, v_ref, mo_ref, r_ref, ct_ref, cw_ref, cb_ref, nw_ref,
                  o_ref, ubuf, q_s, k_s, st_ref, m_ref, *, tt):
    s_idx = pl.program_id(1)
    halo = 8

    @pl.when(s_idx == 0)
    def _():
        ubuf[0:halo, :] = jnp.zeros((halo, 2 * ML_W), F32)
        st_ref[...] = jnp.zeros_like(st_ref)
        m_ref[...] = jnp.zeros_like(m_ref)

    @pl.when(s_idx > 0)
    def _():
        ubuf[0:halo, :] = ubuf[tt:tt + halo, :]

    c = ML_CHUNK
    for blk in range(tt // c):
        ubuf[halo + blk * c:halo + (blk + 1) * c, :] = (
            u_ref[blk * c:(blk + 1) * c, :].astype(F32))
    for blk in range(tt // c):
        acc = cb_ref[...]
        for j in range(CONV_K):
            off = halo - (CONV_K - 1) + j + blk * c
            acc = acc + cw_ref[j:j + 1, :] * ubuf[off:off + c, :]
        qk = acc * jax.nn.sigmoid(acc)
        q_s[blk * c:(blk + 1) * c, :] = qk[:, :ML_W].astype(BF16)
        k_s[blk * c:(blk + 1) * c, :] = (qk[:, ML_W:] * (ML_HD ** -0.5)).astype(BF16)

    row = lax.broadcasted_iota(jnp.int32, (c, c), 0)
    col = lax.broadcasted_iota(jnp.int32, (c, c), 1)
    causal = col <= row
    one_col = jnp.where(lax.broadcasted_iota(jnp.int32, (c, ML_HD), 1) == 0,
                        1.0, 0.0).astype(BF16)

    def chunk(ci, carry):
        r0 = pl.multiple_of(ci * c, c)
        rb = r_ref[pl.ds(r0, c), :]
        cb = ct_ref[0, :, pl.ds(r0, c)]
        for h in range(ML_HEADS):
            hs = slice(h * ML_HD, (h + 1) * ML_HD)
            qh = q_s[pl.ds(r0, c), hs]
            kh = k_s[pl.ds(r0, c), hs]
            vh = v_ref[pl.ds(r0, c), hs]
            f_col = rb[:, GATE_MF + h:GATE_MF + h + 1]
            i_col = rb[:, GATE_MI + h:GATE_MI + h + 1]
            f_row = cb[GATE_MF + h:GATE_MF + h + 1, :]
            i_row = cb[GATE_MI + h:GATE_MI + h + 1, :]
            b_row = f_row - i_row
            f_last = f_row[:, c - 1:c]
            m_prev = m_ref[h:h + 1, 0:1]

            dlog = jnp.where(causal, f_col - b_row, -jnp.inf)
            inter_log = f_col + m_prev
            m_t = jnp.maximum(inter_log, jnp.max(dlog, axis=-1, keepdims=True))
            dw = jnp.exp(dlog - m_t)
            inter_w = jnp.exp(inter_log - m_t)
            sqk = lax.dot_general(qh, kh, NT_DIMS, preferred_element_type=F32) * dw
            state = st_ref[h]
            qc = jnp.dot(qh, state.astype(BF16), preferred_element_type=F32)
            num = inter_w * qc[:, :ML_HD] + jnp.dot(
                sqk.astype(BF16), vh, preferred_element_type=F32)
            den = inter_w * qc[:, ML_HD:ML_HD + 1] + jnp.sum(sqk, axis=-1, keepdims=True)
            hh = num / jnp.maximum(jnp.abs(den), jnp.exp(-m_t))
            hh = hh * lax.rsqrt(jnp.mean(hh * hh, axis=-1, keepdims=True) + EPS)
            y = hh * nw_ref[:, hs] * jax.nn.sigmoid(mo_ref[pl.ds(r0, c), hs].astype(F32))
            o_ref[pl.ds(r0, c), hs] = y.astype(BF16)

            m_loc = jnp.max(f_last - b_row, axis=-1, keepdims=True)
            m_new = jnp.maximum(f_last + m_prev, m_loc)
            s_old = jnp.exp(f_last + m_prev - m_new)
            w_col = jnp.exp(f_last - (f_col - i_col) - m_new)
            kw = (kh.astype(F32) * w_col).astype(BF16)
            v_aug = jnp.concatenate([vh, one_col], axis=1)
            upd = lax.dot_general(kw, v_aug, TN_DIMS, preferred_element_type=F32)
            st_ref[h] = s_old * state + upd
            m_ref[h:h + 1, 0:1] = m_new
        return carry

    lax.fori_loop(0, tt // c, chunk, 0)


def _mlstm(proj, rows, ct, conv_w, conv_b, norm_w, *, batch, seq, tt):
    ns = seq // tt
    ub = MQK_OFF // (2 * ML_W)
    vb = MV_OFF // ML_W
    ob = MO_OFF // ML_W
    rmap = lambda b, s: (b * ns + s, 0)
    return pl.pallas_call(
        functools.partial(_mlstm_kernel, tt=tt),
        grid=(batch, ns),
        in_specs=[
            pl.BlockSpec((tt, 2 * ML_W), lambda b, s: (b * ns + s, ub)),
            pl.BlockSpec((tt, ML_W), lambda b, s: (b * ns + s, vb)),
            pl.BlockSpec((tt, ML_W), lambda b, s: (b * ns + s, ob)),
            pl.BlockSpec((tt, LANES), rmap),
            pl.BlockSpec((1, GATE_USED, tt), lambda b, s: (b, 0, s)),
            pl.BlockSpec((CONV_K, 2 * ML_W), lambda b, s: (0, 0)),
            pl.BlockSpec((1, 2 * ML_W), lambda b, s: (0, 0)),
            pl.BlockSpec((1, ML_W), lambda b, s: (0, 0)),
        ],
        out_specs=pl.BlockSpec((tt, ML_W), rmap),
        out_shape=jax.ShapeDtypeStruct((batch * seq, ML_W), BF16),
        scratch_shapes=[pltpu.VMEM((tt + 8, 2 * ML_W), F32),
                        pltpu.VMEM((tt, ML_W), BF16),
                        pltpu.VMEM((tt, ML_W), BF16),
                        pltpu.VMEM((ML_HEADS, ML_HD, 2 * ML_HD), F32),
                        pltpu.VMEM((8, LANES), F32)],
        compiler_params=_params("parallel", "arbitrary"),
        name="mlstm")(proj, proj, proj, rows, ct, conv_w, conv_b, norm_w)


def _merge_kernel(yf_ref, ym_ref, g_ref, bg_ref, h_ref, wf_ref, wm_ref, wo_ref, o_ref):
    g = jax.nn.sigmoid(g_ref[...].astype(F32) + bg_ref[...])
    uf = jnp.dot(yf_ref[...], wf_ref[...], preferred_element_type=F32)
    um = jnp.dot(ym_ref[...], wm_ref[...], preferred_element_type=F32)
    merged = g[:, :D_MODEL] * uf + g[:, D_MODEL:] * um
    o_ref[...] = h_ref[...] + jnp.dot(merged.astype(BF16), wo_ref[...],
                                      preferred_element_type=F32)


def _merge(yf, ym, proj, b_gate, h, wf, wm, wo, *, tm):
    m = h.shape[0]
    row = lambda i: (i, 0)
    const = lambda i: (0, 0)
    return pl.pallas_call(
        _merge_kernel, grid=(m // tm,),
        in_specs=[
            pl.BlockSpec((tm, FOX_W), row),
            pl.BlockSpec((tm, ML_W), row),
            pl.BlockSpec((tm, 2 * D_MODEL), lambda i: (i, G_OFF // (2 * D_MODEL))),
            pl.BlockSpec((1, 2 * D_MODEL), const),
            pl.BlockSpec((tm, D_MODEL), row),
            pl.BlockSpec((FOX_W, D_MODEL), const),
            pl.BlockSpec((ML_W, D_MODEL), const),
            pl.BlockSpec((D_MODEL, D_MODEL), const),
        ],
        out_specs=pl.BlockSpec((tm, D_MODEL), row),
        out_shape=jax.ShapeDtypeStruct((m, D_MODEL), F32),
        compiler_params=_params("parallel"),
        name="merge_mix_out")(yf, ym, proj, b_gate, h, wf, wm, wo)


def _xattn_kernel(h_ref, g_ref, wq_ref, k_ref, v_ref, wo_ref, o_ref):
    x = h_ref[...]
    hn = _rms(x, g_ref[...]).astype(BF16)
    q = jnp.dot(hn, wq_ref[...], preferred_element_type=F32)
    q = (q * (X_HD ** -0.5)).astype(BF16)
    outs = []
    for hd in range(X_HEADS):
        hs = slice(hd * X_HD, (hd + 1) * X_HD)
        s = lax.dot_general(q[:, hs], k_ref[:, hs], NT_DIMS, preferred_element_type=F32)
        e = jnp.exp(s - jnp.max(s, axis=-1, keepdims=True))
        pr = e / jnp.sum(e, axis=-1, keepdims=True)
        outs.append(jnp.dot(pr.astype(BF16), v_ref[:, hs], preferred_element_type=F32))
    o = jnp.concatenate(outs, axis=1).astype(BF16)
    o_ref[...] = x + jnp.dot(o, wo_ref[...], preferred_element_type=F32)


def _xattn(h, gain, wq, kv, wo, *, batch, seq, n_mem, tm):
    ns = seq // tm
    const = lambda b, s: (0, 0)
    return pl.pallas_call(
        _xattn_kernel, grid=(batch, ns),
        in_specs=[
            pl.BlockSpec((tm, D_MODEL), lambda b, s: (b * ns + s, 0)),
            pl.BlockSpec((1, D_MODEL), const),
            pl.BlockSpec((D_MODEL, D_MODEL), const),
            pl.BlockSpec((n_mem, D_MODEL), lambda b, s: (b, 0)),
            pl.BlockSpec((n_mem, D_MODEL), lambda b, s: (b, 1)),
            pl.BlockSpec((D_MODEL, D_MODEL), const),
        ],
        out_specs=pl.BlockSpec((tm, D_MODEL), lambda b, s: (b * ns + s, 0)),
        out_shape=jax.ShapeDtypeStruct((batch * seq, D_MODEL), F32),
        compiler_params=_params("parallel", "parallel"),
        name="cross_attention")(h, gain, wq, kv, kv, wo)


def _mlp_kernel(h_ref, g_ref, w1_ref, w2_ref, gf_ref, o_ref, hn_ref, *, final):
    j = pl.program_id(1)

    @pl.when(j == 0)
    def _():
        x = h_ref[...]
        hn_ref[...] = _rms(x, g_ref[...]).astype(BF16)
        o_ref[...] = x

    a = jnp.maximum(jnp.dot(hn_ref[...], w1_ref[...], preferred_element_type=F32), 0.0)
    o_ref[...] += jnp.dot((a * a).astype(BF16), w2_ref[...], preferred_element_type=F32)

    if final:
        @pl.when(j == pl.num_programs(1) - 1)
        def _():
            o_ref[...] = _rms(o_ref[...], gf_ref[...])


def _mlp(h, gain, w1, w2, gain_final, *, tm, tf, final):
    m = h.shape[0]
    const = lambda i, j: (0, 0)
    return pl.pallas_call(
        functools.partial(_mlp_kernel, final=final),
        grid=(m // tm, D_FF // tf),
        in_specs=[
            pl.BlockSpec((tm, D_MODEL), lambda i, j: (i, 0)),
            pl.BlockSpec((1, D_MODEL), const),
            pl.BlockSpec((D_MODEL, tf), lambda i, j: (0, j)),
            pl.BlockSpec((tf, D_MODEL), lambda i, j: (j, 0)),
            pl.BlockSpec((1, D_MODEL), const),
        ],
        out_specs=pl.BlockSpec((tm, D_MODEL), lambda i, j: (i, 0)),
        out_shape=jax.ShapeDtypeStruct((m, D_MODEL), F32),
        scratch_shapes=[pltpu.VMEM((tm, D_MODEL), BF16)],
        compiler_params=_params("parallel", "arbitrary"),
        name="mlp")(h, gain, w1, w2, gain_final)


def _pick(total, prefs):
    for t in prefs:
        if total % t == 0:
            return t
    raise ValueError(f"no tile in {prefs} divides {total}")


def kernel(x, mem, norm_mix, norm_xattn, norm_mem, norm_mlp, w_in, b_fox_f, b_mlstm_i,
           b_mlstm_f, b_gate, conv_w, conv_b, mlstm_norm, w_up_fox, w_up_mlstm, w_mix_out,
           w_xq, w_xkv, w_xo, w_mlp_in, w_mlp_out, norm_final):
    batch, seq, d = x.shape
    n_mem = mem.shape[1]
    depth = w_in.shape[0]
    assert d == D_MODEL and seq % ML_CHUNK == 0
    m = batch * seq

    tm_proj = _pick(m, (1024, 512, 256))
    tn_proj = PROJ_W // 4
    t_seq = _pick(seq, (1024, 512, 256))
    tq = _pick(seq, (512, 256))
    tm_tok = _pick(m, (512, 256))
    tm_x = _pick(seq, (512, 256))
    tm_mlp = _pick(m, (1024, 512, 256))
    tf = 1024

    sp = [0, FOX_W, 2 * FOX_W, 3 * FOX_W, 3 * FOX_W + FOX_HEADS]
    sp += [sp[-1] + 2 * ML_W, sp[-1] + 3 * ML_W, sp[-1] + 4 * ML_W]
    sp += [sp[-1] + ML_HEADS, sp[-1] + 2 * ML_HEADS, sp[-1] + 2 * ML_HEADS + 2 * D_MODEL]
    seg = [w_in[:, :, a:b] for a, b in zip(sp[:-1], sp[1:])]
    fq, fk, fv, ff, mqk, mv, mo, mi, mf, gp = seg
    w_main = jnp.concatenate([gp, mqk, fq, fk, fv, mv, mo], axis=2).astype(BF16)
    w_gate = jnp.concatenate(
        [ff, mi, mf, jnp.zeros((depth, d, LANES - GATE_USED), w_in.dtype)], axis=2).astype(BF16)
    gate_bias = jnp.concatenate(
        [b_fox_f, b_mlstm_i, b_mlstm_f, jnp.zeros((depth, LANES - GATE_USED), F32)], axis=1)
    bf = lambda w: w.astype(BF16)
    w_up_fox, w_up_mlstm, w_mix_out = bf(w_up_fox), bf(w_up_mlstm), bf(w_mix_out)
    w_xq, w_xkv, w_xo = bf(w_xq), bf(w_xkv), bf(w_xo)
    w_mlp_in, w_mlp_out = bf(w_mlp_in), bf(w_mlp_out)

    h = x.reshape(m, d)
    mem2 = mem.reshape(batch * n_mem, d)
    for l in range(depth):
        proj, gpre = _norm_proj(h, norm_mix[l][None], w_main[l], w_gate[l],
                                tm=tm_proj, tn=tn_proj)
        rows, ct = _gates(gpre, gate_bias[l][None], batch=batch, seq=seq, tg=t_seq)
        y_fox = _fox(proj, ct, batch=batch, seq=seq, tq=tq)
        y_ml = _mlstm(proj, rows, ct, conv_w[l], conv_b[l][None], mlstm_norm[l][None],
                      batch=batch, seq=seq, tt=t_seq)
        h = _merge(y_fox, y_ml, proj, b_gate[l][None], h,
                   w_up_fox[l], w_up_mlstm[l], w_mix_out[l], tm=tm_tok)
        kv = _norm_proj(mem2, norm_mem[l][None], w_xkv[l],
                        tm=_pick(batch * n_mem, (1024, 512, 256)), tn=1024)
        h = _xattn(h, norm_xattn[l][None], w_xq[l], kv, w_xo[l],
                   batch=batch, seq=seq, n_mem=n_mem, tm=tm_x)
        h = _mlp(h, norm_mlp[l][None], w_mlp_in[l], w_mlp_out[l], norm_final[None],
                 tm=tm_mlp, tf=tf, final=(l == depth - 1))
    return h.reshape(batch, seq, d)
```

```python
import functools

import jax
import jax.numpy as jnp
import numpy as np
from jax import lax
from jax.experimental import pallas as pl
from jax.experimental.pallas import tpu as pltpu

F32 = jnp.float32
BF16 = jnp.bfloat16

D_MODEL = 1024
FOX_HEADS = 8
FOX_HD = 64
FOX_W = FOX_HEADS * FOX_HD
ML_HEADS = 4
ML_HD = 128
ML_W = ML_HEADS * ML_HD
CONV_K = 4
X_HEADS = 4
X_HD = D_MODEL // X_HEADS
D_FF = 4 * D_MODEL
EPS = 1e-6
LOG2E = 1.4426950408889634

LANES = 128
V7X_VMEM_LIMIT_BYTES = 56 * 1024 * 1024

G_OFF = 0
MQK_OFF = G_OFF + 2 * D_MODEL
FQ_OFF = MQK_OFF + 2 * ML_W
FK_OFF = FQ_OFF + FOX_W
FV_OFF = FK_OFF + FOX_W
MV_OFF = FV_OFF + FOX_W
MO_OFF = MV_OFF + ML_W
PROJ_W = MO_OFF + ML_W
GATE_FF = 0
GATE_MI = GATE_FF + FOX_HEADS
GATE_MF = GATE_MI + ML_HEADS
GATE_USED = GATE_MF + ML_HEADS

ML_CHUNK = 256
NT_DIMS = (((1,), (1,)), ((), ()))
TN_DIMS = (((0,), (0,)), ((), ()))


def _params(*sem):
    return pltpu.CompilerParams(dimension_semantics=sem,
                                vmem_limit_bytes=V7X_VMEM_LIMIT_BYTES)


def _rms(x, g):
    return x * lax.rsqrt(jnp.mean(x * x, axis=-1, keepdims=True) + EPS) * g


def _norm_proj_gates_kernel(x_ref, g_ref, w_ref, wg_ref, o_ref, og_ref, xn_ref):
    @pl.when(pl.program_id(1) == 0)
    def _():
        xn = _rms(x_ref[...], g_ref[...]).astype(BF16)
        xn_ref[...] = xn
        og_ref[...] = jnp.dot(xn, wg_ref[...], preferred_element_type=F32)

    o_ref[...] = jnp.dot(xn_ref[...], w_ref[...],
                         preferred_element_type=F32).astype(BF16)


def _norm_proj_kernel(x_ref, g_ref, w_ref, o_ref, xn_ref):
    @pl.when(pl.program_id(1) == 0)
    def _():
        xn_ref[...] = _rms(x_ref[...], g_ref[...]).astype(BF16)

    o_ref[...] = jnp.dot(xn_ref[...], w_ref[...],
                         preferred_element_type=F32).astype(BF16)


def _norm_proj(x, gain, w, wg=None, *, tm, tn):
    m, d = x.shape
    n = w.shape[1]
    grid = (m // tm, n // tn)
    x_spec = pl.BlockSpec((tm, d), lambda i, j: (i, 0))
    g_spec = pl.BlockSpec((1, d), lambda i, j: (0, 0))
    w_spec = pl.BlockSpec((d, tn), lambda i, j: (0, j))
    o_spec = pl.BlockSpec((tm, tn), lambda i, j: (i, j))
    scratch = [pltpu.VMEM((tm, d), BF16)]
    if wg is None:
        return pl.pallas_call(
            _norm_proj_kernel, grid=grid,
            in_specs=[x_spec, g_spec, w_spec], out_specs=o_spec,
            out_shape=jax.ShapeDtypeStruct((m, n), BF16),
            scratch_shapes=scratch,
            compiler_params=_params("parallel", "arbitrary"),
            name="norm_proj")(x, gain, w)
    wg_spec = pl.BlockSpec((d, LANES), lambda i, j: (0, 0))
    og_spec = pl.BlockSpec((tm, LANES), lambda i, j: (i, 0))
    return pl.pallas_call(
        _norm_proj_gates_kernel, grid=grid,
        in_specs=[x_spec, g_spec, w_spec, wg_spec],
        out_specs=[o_spec, og_spec],
        out_shape=[jax.ShapeDtypeStruct((m, n), BF16),
                   jax.ShapeDtypeStruct((m, LANES), F32)],
        scratch_shapes=scratch,
        compiler_params=_params("parallel", "arbitrary"),
        name="norm_in_proj")(x, gain, w, wg)


def _split3(v):
    hi = v.astype(BF16)
    r1 = v - hi.astype(F32)
    mid = r1.astype(BF16)
    lo = (r1 - mid.astype(F32)).astype(BF16)
    return hi, mid, lo


def _gates_kernel(x_ref, b_ref, pm_ref, r_ref, ct_ref, cb_ref, carry_ref, *, tg):
    @pl.when(pl.program_id(1) == 0)
    def _():
        carry_ref[...] = jnp.zeros_like(carry_ref)

    c = ML_CHUNK
    row = lax.broadcasted_iota(jnp.int32, (c, c), 0)
    col = lax.broadcasted_iota(jnp.int32, (c, c), 1)
    tri = jnp.where(row >= col, 1.0, 0.0).astype(BF16)
    lane = lax.broadcasted_iota(jnp.int32, (c, LANES), 1)
    is_fox = lane < GATE_MI
    is_ig = (lane >= GATE_MI) & (lane < GATE_MF)
    is_used = lane < GATE_USED
    bias = b_ref[...]
    for blk in range(tg // c):
        rs = slice(blk * c, (blk + 1) * c)
        x = x_ref[rs, :] + bias
        logsig = jnp.minimum(x, 0.0) - jnp.log1p(jnp.exp(-jnp.abs(x)))
        val = jnp.where(is_ig, x, logsig)
        val = jnp.where(is_used, val, 0.0)
        hi, mid, lo = _split3(val)
        cs = (jnp.dot(tri, hi, preferred_element_type=F32)
              + jnp.dot(tri, mid, preferred_element_type=F32)
              + jnp.dot(tri, lo, preferred_element_type=F32))
        glob = cs + carry_ref[0:1, :]
        carry_ref[0:1, :] = glob[c - 1:c, :]
        out = jnp.where(is_fox, glob, jnp.where(is_ig, val, cs))
        r_ref[rs, :] = out
        ct_ref[0, :, rs] = out.T[:GATE_USED, :]
        terms = _split3(jnp.where(is_fox, glob * (-LOG2E), 0.0))
        cb = jnp.dot(terms[0], pm_ref[0], preferred_element_type=F32)
        for t in range(1, 3):
            cb = cb + jnp.dot(terms[t], pm_ref[t], preferred_element_type=F32)
        cb_ref[rs, :] = cb.astype(BF16)


def _fox_bias_placement():
    pm = np.zeros((3, LANES, FOX_W), np.float32)
    for h in range(FOX_HEADS):
        base = (h // 2) * LANES + (FOX_HD if h % 2 == 0 else 0)
        for t in range(3):
            pm[t, h, base + t] = 1.0
    return jnp.asarray(pm, BF16)


def _gates(gpre, bias, *, batch, seq, tg):
    ns = seq // tg
    rmap = lambda b, s: (b * ns + s, 0)
    return pl.pallas_call(
        functools.partial(_gates_kernel, tg=tg),
        grid=(batch, ns),
        in_specs=[pl.BlockSpec((tg, LANES), rmap),
                  pl.BlockSpec((1, LANES), lambda b, s: (0, 0)),
                  pl.BlockSpec((3, LANES, FOX_W), lambda b, s: (0, 0, 0))],
        out_specs=[pl.BlockSpec((tg, LANES), rmap),
                   pl.BlockSpec((1, GATE_USED, tg), lambda b, s: (b, 0, s)),
                   pl.BlockSpec((tg, FOX_W), rmap)],
        out_shape=[jax.ShapeDtypeStruct((batch * seq, LANES), F32),
                   jax.ShapeDtypeStruct((batch, GATE_USED, seq), F32),
                   jax.ShapeDtypeStruct((batch * seq, FOX_W), BF16)],
        scratch_shapes=[pltpu.VMEM((8, LANES), F32)],
        compiler_params=_params("parallel", "arbitrary"),
        name="gates")(gpre, bias, _fox_bias_placement())


def _fox_lane_rows():
    r = np.zeros((8, LANES), np.float32)
    r[0, :FOX_HD] = 1.0
    r[1, FOX_HD:] = 1.0
    r[2, FOX_HD:FOX_HD + 3] = 1.0
    r[3, 0:3] = 1.0
    r[4, FOX_HD] = 1.0
    r[5, 0] = 1.0
    return jnp.asarray(r, BF16)


def _fox_kernel(q_ref, k_ref, v_ref, cb_ref, mk_ref, o_ref, m_ref, acc_ref, *, tq, tk):
    i = pl.program_id(2)
    n_sub = tq // tk
    lo, hi = mk_ref[0:1, :], mk_ref[1:2, :]
    q2 = q_ref[...]
    qh = (q2 * lo + mk_ref[2:3, :], q2 * hi + mk_ref[3:4, :])
    one_a, one_b = mk_ref[4:5, :], mk_ref[5:6, :]

    m_ref[...] = jnp.full_like(m_ref, -jnp.inf)
    acc_ref[...] = jnp.zeros_like(acc_ref)

    def step(j, row0, masked):
        rows = tq - row0
        r0 = pl.multiple_of(j * tk, tk)
        ks = k_ref[pl.ds(r0, tk), :]
        vs = v_ref[pl.ds(r0, tk), :]
        cb = cb_ref[pl.ds(r0, tk), :]
        kh = (ks * lo + cb * hi, ks * hi + cb * lo)
        vh = (vs * lo + one_a, vs * hi + one_b)
        if masked:
            keep = (lax.broadcasted_iota(jnp.int32, (rows, tk), 1) + j * tk
                    <= lax.broadcasted_iota(jnp.int32, (rows, tk), 0) + (i * tq + row0))
        for h in range(2):
            s = lax.dot_general(qh[h][row0:, :], kh[h], NT_DIMS, preferred_element_type=F32)
            if masked:
                s = jnp.where(keep, s, -jnp.inf)
            m_old = m_ref[h, row0:, :]
            m_new = jnp.maximum(m_old, jnp.max(s, axis=-1, keepdims=True))
            alpha = jnp.exp2(m_old - m_new)
            pe = jnp.exp2(s - jnp.concatenate([m_new] * (tk // LANES), axis=1))
            acc_ref[h, row0:, :] = alpha * acc_ref[h, row0:, :] + jnp.dot(
                pe.astype(BF16), vh[h], preferred_element_type=F32)
            m_ref[h, row0:, :] = m_new

    def body(j, carry):
        step(j, 0, False)
        return carry

    lax.fori_loop(0, i * n_sub, body, 0)
    for d in range(n_sub):
        step(i * n_sub + d, d * tk, True)

    acc_a, acc_b = acc_ref[0], acc_ref[1]
    l_a = jnp.broadcast_to(acc_a[:, FOX_HD:FOX_HD + 1], (tq, LANES))
    l_b = jnp.broadcast_to(acc_b[:, 0:1], (tq, LANES))
    first = lax.broadcasted_iota(jnp.int32, (tq, LANES), 1) < FOX_HD
    o_ref[...] = jnp.where(first, acc_a / l_a, acc_b / l_b).astype(BF16)


def _fox(proj, cb, *, batch, seq, tq, tk):
    nq = seq // tq
    npair = FOX_HEADS // 2
    qb, kb, vb = FQ_OFF // LANES, FK_OFF // LANES, FV_OFF // LANES
    return pl.pallas_call(
        functools.partial(_fox_kernel, tq=tq, tk=tk),
        grid=(batch, npair, nq),
        in_specs=[
            pl.BlockSpec((tq, LANES), lambda b, p, i: (b * nq + i, qb + p)),
            pl.BlockSpec((seq, LANES), lambda b, p, i: (b, kb + p)),
            pl.BlockSpec((seq, LANES), lambda b, p, i: (b, vb + p)),
            pl.BlockSpec((seq, LANES), lambda b, p, i: (b, p)),
            pl.BlockSpec((8, LANES), lambda b, p, i: (0, 0)),
        ],
        out_specs=pl.BlockSpec((tq, LANES), lambda b, p, i: (b * nq + i, p)),
        out_shape=jax.ShapeDtypeStruct((batch * seq, FOX_W), BF16),
        scratch_shapes=[pltpu.VMEM((2, tq, LANES), F32),
                        pltpu.VMEM((2, tq, LANES), F32)],
        compiler_params=_params("parallel", "parallel", "arbitrary"),
        name="fox_attention")(proj, proj, proj, cb, _fox_lane_rows())


def _mlstm_kernel(u_ref, v_ref, mo_ref, r_ref, ct_ref, cw_ref, cb_ref, nw_ref,
                  o_ref, ubuf, q_s, k_s, st_ref, m_ref, *, tt):
    s_idx = pl.program_id(1)
    halo = 8

    @pl.when(s_idx == 0)
    def _():
        ubuf[0:halo, :] = jnp.zeros((halo, 2 * ML_W), F32)
        st_ref[...] = jnp.zeros_like(st_ref)
        m_ref[...] = jnp.zeros_like(m_ref)

    @pl.when(s_idx > 0)
    def _():
        ubuf[0:halo, :] = ubuf[tt:tt + halo, :]

    c = ML_CHUNK
    for blk in range(tt // c):
        ubuf[halo + blk * c:halo + (blk + 1) * c, :] = (
            u_ref[blk * c:(blk + 1) * c, :].astype(F32))
    for blk in range(tt // c):
        acc = cb_ref[...]
        for j in range(CONV_K):
            off = halo - (CONV_K - 1) + j + blk * c
            acc = acc + cw_ref[j:j + 1, :] * ubuf[off:off + c, :]
        qk = acc * jax.nn.sigmoid(acc)
        q_s[blk * c:(blk + 1) * c, :] = qk[:, :ML_W].astype(BF16)
        k_s[blk * c:(blk + 1) * c, :] = (qk[:, ML_W:] * (ML_HD ** -0.5)).astype(BF16)

    row = lax.broadcasted_iota(jnp.int32, (c, c), 0)
    col = lax.broadcasted_iota(jnp.int32, (c, c), 1)
    causal = col <= row
    one_col = jnp.where(lax.broadcasted_iota(jnp.int32, (c, ML_HD), 1) == 0,
                        1.0, 0.0).astype(BF16)

    def chunk(ci, carry):
        r0 = pl.multiple_of(ci * c, c)
        rb = r_ref[pl.ds(r0, c), :]
        cb = ct_ref[0, :, pl.ds(r0, c)]
        for h in range(ML_HEADS):
            hs = slice(h * ML_HD, (h + 1) * ML_HD)
            qh = q_s[pl.ds(r0, c), hs]
            kh = k_s[pl.ds(r0, c), hs]
            vh = v_ref[pl.ds(r0, c), hs]
            f_col = rb[:, GATE_MF + h:GATE_MF + h + 1]
            i_col = rb[:, GATE_MI + h:GATE_MI + h + 1]
            f_row = cb[GATE_MF + h:GATE_MF + h + 1, :]
            i_row = cb[GATE_MI + h:GATE_MI + h + 1, :]
            b_row = f_row - i_row
            f_last = f_row[:, c - 1:c]
            m_prev = m_ref[h:h + 1, 0:1]

            dlog = jnp.where(causal, f_col - b_row, -jnp.inf)
            inter_log = f_col + m_prev
            m_t = jnp.maximum(inter_log, jnp.max(dlog, axis=-1, keepdims=True))
            dw = jnp.exp(dlog - m_t)
            inter_w = jnp.exp(inter_log - m_t)
            sqk = lax.dot_general(qh, kh, NT_DIMS, preferred_element_type=F32) * dw
            state = st_ref[h]
            qc = jnp.dot(qh, state.astype(BF16), preferred_element_type=F32)
            num = inter_w * qc[:, :ML_HD] + jnp.dot(
                sqk.astype(BF16), vh, preferred_element_type=F32)
            den = inter_w * qc[:, ML_HD:ML_HD + 1] + jnp.sum(sqk, axis=-1, keepdims=True)
            hh = num / jnp.maximum(jnp.abs(den), jnp.exp(-m_t))
            hh = hh * lax.rsqrt(jnp.mean(hh * hh, axis=-1, keepdims=True) + EPS)
            y = hh * nw_ref[:, hs] * jax.nn.sigmoid(mo_ref[pl.ds(r0, c), hs].astype(F32))
            o_ref[pl.ds(r0, c), hs] = y.astype(BF16)

            m_loc = jnp.max(f_last - b_row, axis=-1, keepdims=True)
            m_new = jnp.maximum(f_last + m_prev, m_loc)
            s_old = jnp.exp(f_last + m_prev - m_new)
            w_col = jnp.exp(f_last - (f_col - i_col) - m_new)
            kw = (kh.astype(F32) * w_col).astype(BF16)
            v_aug = jnp.concatenate([vh, one_col], axis=1)
            upd = lax.dot_general(kw, v_aug, TN_DIMS, preferred_element_type=F32)
            st_ref[h] = s_old * state + upd
            m_ref[h:h + 1, 0:1] = m_new
        return carry

    lax.fori_loop(0, tt // c, chunk, 0)


def _mlstm(proj, rows, ct, conv_w, conv_b, norm_w, *, batch, seq, tt):
    ns = seq // tt
    ub = MQK_OFF // (2 * ML_W)
    vb = MV_OFF // ML_W
    ob = MO_OFF // ML_W
    rmap = lambda b, s: (b * ns + s, 0)
    return pl.pallas_call(
        functools.partial(_mlstm_kernel, tt=tt),
        grid=(batch, ns),
        in_specs=[
            pl.BlockSpec((tt, 2 * ML_W), lambda b, s: (b * ns + s, ub)),
            pl.BlockSpec((tt, ML_W), lambda b, s: (b * ns + s, vb)),
            pl.BlockSpec((tt, ML_W), lambda b, s: (b * ns + s, ob)),
            pl.BlockSpec((tt, LANES), rmap),
            pl.BlockSpec((1, GATE_USED, tt), lambda b, s: (b, 0, s)),
            pl.BlockSpec((CONV_K, 2 * ML_W), lambda b, s: (0, 0)),
            pl.BlockSpec((1, 2 * ML_W), lambda b, s: (0, 0)),
            pl.BlockSpec((1, ML_W), lambda b, s: (0, 0)),
        ],
        out_specs=pl.BlockSpec((tt, ML_W), rmap),
        out_shape=jax.ShapeDtypeStruct((batch * seq, ML_W), BF16),
        scratch_shapes=[pltpu.VMEM((tt + 8, 2 * ML_W), F32),
                        pltpu.VMEM((tt, ML_W), BF16),
                        pltpu.VMEM((tt, ML_W), BF16),
                        pltpu.VMEM((ML_HEADS, ML_HD, 2 * ML_HD), F32),
                        pltpu.VMEM((8, LANES), F32)],
        compiler_params=_params("parallel", "arbitrary"),
        name="mlstm")(proj, proj, proj, rows, ct, conv_w, conv_b, norm_w)


def _merge_kernel(yf_ref, ym_ref, g_ref, bg_ref, h_ref, wf_ref, wm_ref, wo_ref, o_ref):
    g = jax.nn.sigmoid(g_ref[...].astype(F32) + bg_ref[...])
    uf = jnp.dot(yf_ref[...], wf_ref[...], preferred_element_type=F32)
    um = jnp.dot(ym_ref[...], wm_ref[...], preferred_element_type=F32)
    merged = g[:, :D_MODEL] * uf + g[:, D_MODEL:] * um
    o_ref[...] = h_ref[...] + jnp.dot(merged.astype(BF16), wo_ref[...],
                                      preferred_element_type=F32)


def _merge(yf, ym, proj, b_gate, h, wf, wm, wo, *, tm):
    m = h.shape[0]
    row = lambda i: (i, 0)
    const = lambda i: (0, 0)
    return pl.pallas_call(
        _merge_kernel, grid=(m // tm,),
        in_specs=[
            pl.BlockSpec((tm, FOX_W), row),
            pl.BlockSpec((tm, ML_W), row),
            pl.BlockSpec((tm, 2 * D_MODEL), lambda i: (i, G_OFF // (2 * D_MODEL))),
            pl.BlockSpec((1, 2 * D_MODEL), const),
            pl.BlockSpec((tm, D_MODEL), row),
            pl.BlockSpec((FOX_W, D_MODEL), const),
            pl.BlockSpec((ML_W, D_MODEL), const),
            pl.BlockSpec((D_MODEL, D_MODEL), const),
        ],
        out_specs=pl.BlockSpec((tm, D_MODEL), row),
        out_shape=jax.ShapeDtypeStruct((m, D_MODEL), F32),
        compiler_params=_params("parallel"),
        name="merge_mix_out")(yf, ym, proj, b_gate, h, wf, wm, wo)


def _xattn_kernel(h_ref, g_ref, wq_ref, k_ref, v_ref, wo_ref, o_ref):
    x = h_ref[...]
    hn = _rms(x, g_ref[...]).astype(BF16)
    q = jnp.dot(hn, wq_ref[...], preferred_element_type=F32)
    q = (q * (X_HD ** -0.5)).astype(BF16)
    outs = []
    for hd in range(X_HEADS):
        hs = slice(hd * X_HD, (hd + 1) * X_HD)
        s = lax.dot_general(q[:, hs], k_ref[:, hs], NT_DIMS, preferred_element_type=F32)
        e = jnp.exp(s - jnp.max(s, axis=-1, keepdims=True))
        pr = e / jnp.sum(e, axis=-1, keepdims=True)
        outs.append(jnp.dot(pr.astype(BF16), v_ref[:, hs], preferred_element_type=F32))
    o = jnp.concatenate(outs, axis=1).astype(BF16)
    o_ref[...] = x + jnp.dot(o, wo_ref[...], preferred_element_type=F32)


def _xattn(h, gain, wq, kv, wo, *, batch, seq, n_mem, tm):
    ns = seq // tm
    const = lambda b, s: (0, 0)
    return pl.pallas_call(
        _xattn_kernel, grid=(batch, ns),
        in_specs=[
            pl.BlockSpec((tm, D_MODEL), lambda b, s: (b * ns + s, 0)),
            pl.BlockSpec((1, D_MODEL), const),
            pl.BlockSpec((D_MODEL, D_MODEL), const),
            pl.BlockSpec((n_mem, D_MODEL), lambda b, s: (b, 0)),
            pl.BlockSpec((n_mem, D_MODEL), lambda b, s: (b, 1)),
            pl.BlockSpec((D_MODEL, D_MODEL), const),
        ],
        out_specs=pl.BlockSpec((tm, D_MODEL), lambda b, s: (b * ns + s, 0)),
        out_shape=jax.ShapeDtypeStruct((batch * seq, D_MODEL), F32),
        compiler_params=_params("parallel", "parallel"),
        name="cross_attention")(h, gain, wq, kv, kv, wo)


def _mlp_kernel(h_ref, g_ref, w1_ref, w2_ref, gf_ref, o_ref, hn_ref, *, final):
    j = pl.program_id(1)

    @pl.when(j == 0)
    def _():
        x = h_ref[...]
        hn_ref[...] = _rms(x, g_ref[...]).astype(BF16)
        o_ref[...] = x

    a = jnp.maximum(jnp.dot(hn_ref[...], w1_ref[...], preferred_element_type=F32), 0.0)
    o_ref[...] += jnp.dot((a * a).astype(BF16), w2_ref[...], preferred_element_type=F32)

    if final:
        @pl.when(j == pl.num_programs(1) - 1)
        def _():
            o_ref[...] = _rms(o_ref[...], gf_ref[...])


def _mlp(h, gain, w1, w2, gain_final, *, tm, tf, final):
    m = h.shape[0]
    const = lambda i, j: (0, 0)
    return pl.pallas_call(
        functools.partial(_mlp_kernel, final=final),
        grid=(m // tm, D_FF // tf),
        in_specs=[
            pl.BlockSpec((tm, D_MODEL), lambda i, j: (i, 0)),
            pl.BlockSpec((1, D_MODEL), const),
            pl.BlockSpec((D_MODEL, tf), lambda i, j: (0, j)),
            pl.BlockSpec((tf, D_MODEL), lambda i, j: (j, 0)),
            pl.BlockSpec((1, D_MODEL), const),
        ],
        out_specs=pl.BlockSpec((tm, D_MODEL), lambda i, j: (i, 0)),
        out_shape=jax.ShapeDtypeStruct((m, D_MODEL), F32),
        scratch_shapes=[pltpu.VMEM((tm, D_MODEL), BF16)],
        compiler_params=_params("parallel", "arbitrary"),
        name="mlp")(h, gain, w1, w2, gain_final)


def _pick(total, prefs):
    for t in prefs:
        if total % t == 0:
            return t
    raise ValueError(f"no tile in {prefs} divides {total}")


def kernel(x, mem, norm_mix, norm_xattn, norm_mem, norm_mlp, w_in, b_fox_f, b_mlstm_i,
           b_mlstm_f, b_gate, conv_w, conv_b, mlstm_norm, w_up_fox, w_up_mlstm, w_mix_out,
           w_xq, w_xkv, w_xo, w_mlp_in, w_mlp_out, norm_final):
    batch, seq, d = x.shape
    n_mem = mem.shape[1]
    depth = w_in.shape[0]
    assert d == D_MODEL and seq % ML_CHUNK == 0
    m = batch * seq

    tm_proj = _pick(m, (1024, 512, 256))
    tn_proj = PROJ_W // 4
    t_seq = _pick(seq, (1024, 512, 256))
    tk = _pick(seq, (512, 256))
    tq = _pick(seq, (2 * tk, tk))
    tm_tok = _pick(m, (512, 256))
    tm_x = _pick(seq, (512, 256))
    tm_mlp = _pick(m, (1024, 512, 256))
    tf = 1024

    sp = [0, FOX_W, 2 * FOX_W, 3 * FOX_W, 3 * FOX_W + FOX_HEADS]
    sp += [sp[-1] + 2 * ML_W, sp[-1] + 3 * ML_W, sp[-1] + 4 * ML_W]
    sp += [sp[-1] + ML_HEADS, sp[-1] + 2 * ML_HEADS, sp[-1] + 2 * ML_HEADS + 2 * D_MODEL]
    seg = [w_in[:, :, a:b] for a, b in zip(sp[:-1], sp[1:])]
    fq, fk, fv, ff, mqk, mv, mo, mi, mf, gp = seg
    fq = fq * (FOX_HD ** -0.5 * LOG2E)
    w_main = jnp.concatenate([gp, mqk, fq, fk, fv, mv, mo], axis=2).astype(BF16)
    w_gate = jnp.concatenate(
        [ff, mi, mf, jnp.zeros((depth, d, LANES - GATE_USED), w_in.dtype)], axis=2).astype(BF16)
    gate_bias = jnp.concatenate(
        [b_fox_f, b_mlstm_i, b_mlstm_f, jnp.zeros((depth, LANES - GATE_USED), F32)], axis=1)
    bf = lambda w: w.astype(BF16)
    w_up_fox, w_up_mlstm, w_mix_out = bf(w_up_fox), bf(w_up_mlstm), bf(w_mix_out)
    w_xq, w_xkv, w_xo = bf(w_xq), bf(w_xkv), bf(w_xo)
    w_mlp_in, w_mlp_out = bf(w_mlp_in), bf(w_mlp_out)

    h = x.reshape(m, d)
    mem2 = mem.reshape(batch * n_mem, d)
    for l in range(depth):
        proj, gpre = _norm_proj(h, norm_mix[l][None], w_main[l], w_gate[l],
                                tm=tm_proj, tn=tn_proj)
        rows, ct, cb = _gates(gpre, gate_bias[l][None], batch=batch, seq=seq, tg=t_seq)
        y_fox = _fox(proj, cb, batch=batch, seq=seq, tq=tq, tk=tk)
        y_ml = _mlstm(proj, rows, ct, conv_w[l], conv_b[l][None], mlstm_norm[l][None],
                      batch=batch, seq=seq, tt=t_seq)
        h = _merge(y_fox, y_ml, proj, b_gate[l][None], h,
                   w_up_fox[l], w_up_mlstm[l], w_mix_out[l], tm=tm_tok)
        kv = _norm_proj(mem2, norm_mem[l][None], w_xkv[l],
                        tm=_pick(batch * n_mem, (1024, 512, 256)), tn=1024)
        h = _xattn(h, norm_xattn[l][None], w_xq[l], kv, w_xo[l],
                   batch=batch, seq=seq, n_mem=n_mem, tm=tm_x)
        h = _mlp(h, norm_mlp[l][None], w_mlp_in[l], w_mlp_out[l], norm_final[None],
                 tm=tm_mlp, tf=tf, final=(l == depth - 1))
    return h.reshape(batch, seq, d)
```

```python
import functools

import jax
import jax.numpy as jnp
import numpy as np
from jax import lax
from jax.experimental import pallas as pl
from jax.experimental.pallas import tpu as pltpu

F32 = jnp.float32
BF16 = jnp.bfloat16

D_MODEL = 1024
FOX_HEADS = 8
FOX_HD = 64
FOX_W = FOX_HEADS * FOX_HD
ML_HEADS = 4
ML_HD = 128
ML_W = ML_HEADS * ML_HD
CONV_K = 4
X_HEADS = 4
X_HD = D_MODEL // X_HEADS
D_FF = 4 * D_MODEL
EPS = 1e-6
LOG2E = 1.4426950408889634

LANES = 128
V7X_VMEM_LIMIT_BYTES = 56 * 1024 * 1024

G_OFF = 0
MQK_OFF = G_OFF + 2 * D_MODEL
FQ_OFF = MQK_OFF + 2 * ML_W
FK_OFF = FQ_OFF + FOX_W
FV_OFF = FK_OFF + FOX_W
MV_OFF = FV_OFF + FOX_W
MO_OFF = MV_OFF + ML_W
PROJ_W = MO_OFF + ML_W
GATE_FF = 0
GATE_MI = GATE_FF + FOX_HEADS
GATE_MF = GATE_MI + ML_HEADS
GATE_USED = GATE_MF + ML_HEADS

ML_CHUNK = 256
NT_DIMS = (((1,), (1,)), ((), ()))
TN_DIMS = (((0,), (0,)), ((), ()))


def _params(*sem):
    return pltpu.CompilerParams(dimension_semantics=sem,
                                vmem_limit_bytes=V7X_VMEM_LIMIT_BYTES)


def _rms(x, g):
    return x * lax.rsqrt(jnp.mean(x * x, axis=-1, keepdims=True) + EPS) * g


def _in_proj_kernel(x_ref, g_ref, w_ref, wg_ref, o_ref, og_ref, *, tn):
    xn = _rms(x_ref[...], g_ref[...]).astype(BF16)
    og_ref[...] = jnp.dot(xn, wg_ref[...], preferred_element_type=F32)
    for j in range(w_ref.shape[1] // tn):
        cs = slice(j * tn, (j + 1) * tn)
        o_ref[:, cs] = jnp.dot(xn, w_ref[:, cs], preferred_element_type=F32).astype(BF16)


def _in_proj(x, gain, w, wg, *, tm, tn):
    m, d = x.shape
    n = w.shape[1]
    const = lambda i: (0, 0)
    return pl.pallas_call(
        functools.partial(_in_proj_kernel, tn=tn), grid=(m // tm,),
        in_specs=[pl.BlockSpec((tm, d), lambda i: (i, 0)),
                  pl.BlockSpec((1, d), const),
                  pl.BlockSpec((d, n), const),
                  pl.BlockSpec((d, LANES), const)],
        out_specs=[pl.BlockSpec((tm, n), lambda i: (i, 0)),
                   pl.BlockSpec((tm, LANES), lambda i: (i, 0))],
        out_shape=[jax.ShapeDtypeStruct((m, n), BF16),
                   jax.ShapeDtypeStruct((m, LANES), F32)],
        compiler_params=_params("parallel"),
        name="norm_in_proj")(x, gain, w, wg)


def _norm_proj_kernel(x_ref, g_ref, w_ref, o_ref, xn_ref):
    @pl.when(pl.program_id(1) == 0)
    def _():
        xn_ref[...] = _rms(x_ref[...], g_ref[...]).astype(BF16)

    o_ref[...] = jnp.dot(xn_ref[...], w_ref[...],
                         preferred_element_type=F32).astype(BF16)


def _norm_proj(x, gain, w, *, tm, tn):
    m, d = x.shape
    n = w.shape[1]
    return pl.pallas_call(
        _norm_proj_kernel, grid=(m // tm, n // tn),
        in_specs=[pl.BlockSpec((tm, d), lambda i, j: (i, 0)),
                  pl.BlockSpec((1, d), lambda i, j: (0, 0)),
                  pl.BlockSpec((d, tn), lambda i, j: (0, j))],
        out_specs=pl.BlockSpec((tm, tn), lambda i, j: (i, j)),
        out_shape=jax.ShapeDtypeStruct((m, n), BF16),
        scratch_shapes=[pltpu.VMEM((tm, d), BF16)],
        compiler_params=_params("parallel", "arbitrary"),
        name="norm_proj")(x, gain, w)


def _split3(v):
    hi = v.astype(BF16)
    r1 = v - hi.astype(F32)
    mid = r1.astype(BF16)
    lo = (r1 - mid.astype(F32)).astype(BF16)
    return hi, mid, lo


def _gates_kernel(x_ref, b_ref, pm_ref, r_ref, ct_ref, cb_ref, carry_ref, *, tg):
    @pl.when(pl.program_id(1) == 0)
    def _():
        carry_ref[...] = jnp.zeros_like(carry_ref)

    c = ML_CHUNK
    row = lax.broadcasted_iota(jnp.int32, (c, c), 0)
    col = lax.broadcasted_iota(jnp.int32, (c, c), 1)
    causal = row >= col
    tri = jnp.where(causal, 1.0, 0.0).astype(BF16)
    lane = lax.broadcasted_iota(jnp.int32, (c, LANES), 1)
    is_fox = lane < GATE_MI
    is_ig = (lane >= GATE_MI) & (lane < GATE_MF)
    is_used = lane < GATE_USED
    bias = b_ref[...]
    for blk in range(tg // c):
        rs = slice(blk * c, (blk + 1) * c)
        x = x_ref[rs, :] + bias
        logsig = jnp.minimum(x, 0.0) - jnp.log1p(jnp.exp(-jnp.abs(x)))
        val = jnp.where(is_ig, x, logsig)
        val = jnp.where(is_used, val, 0.0)
        hi, mid, lo = _split3(val)
        cs = (jnp.dot(tri, hi, preferred_element_type=F32)
              + jnp.dot(tri, mid, preferred_element_type=F32)
              + jnp.dot(tri, lo, preferred_element_type=F32))
        glob = cs + carry_ref[0:1, :]
        carry_ref[0:1, :] = glob[c - 1:c, :]
        b_t = pltpu.roll(val, GATE_MF - GATE_MI, axis=1) - cs
        bt = b_t.T
        ct_ref[0, :, rs] = bt[:GATE_USED, :]
        cm = jnp.zeros((c, LANES), F32)
        for h in range(ML_HEADS):
            k = GATE_MF + h
            run_max = jnp.max(jnp.where(causal, bt[k:k + 1, :], -jnp.inf),
                              axis=-1, keepdims=True)
            cm = jnp.where(lane == k, run_max, cm)
        r_ref[rs, 0:LANES] = cs
        r_ref[rs, LANES:2 * LANES] = b_t
        r_ref[rs, 2 * LANES:3 * LANES] = cm
        terms = _split3(jnp.where(is_fox, glob * (-LOG2E), 0.0))
        cb = jnp.dot(terms[0], pm_ref[0], preferred_element_type=F32)
        for t in range(1, 3):
            cb = cb + jnp.dot(terms[t], pm_ref[t], preferred_element_type=F32)
        cb_ref[rs, :] = cb.astype(BF16)


def _fox_bias_placement():
    pm = np.zeros((3, LANES, FOX_W), np.float32)
    for h in range(FOX_HEADS):
        base = (h // 2) * LANES + (FOX_HD if h % 2 == 0 else 0)
        for t in range(3):
            pm[t, h, base + t] = 1.0
    return jnp.asarray(pm, BF16)


def _gates(gpre, bias, *, batch, seq, tg):
    ns = seq // tg
    rmap = lambda b, s: (b * ns + s, 0)
    return pl.pallas_call(
        functools.partial(_gates_kernel, tg=tg),
        grid=(batch, ns),
        in_specs=[pl.BlockSpec((tg, LANES), rmap),
                  pl.BlockSpec((1, LANES), lambda b, s: (0, 0)),
                  pl.BlockSpec((3, LANES, FOX_W), lambda b, s: (0, 0, 0))],
        out_specs=[pl.BlockSpec((tg, 3 * LANES), rmap),
                   pl.BlockSpec((1, GATE_USED, tg), lambda b, s: (b, 0, s)),
                   pl.BlockSpec((tg, FOX_W), rmap)],
        out_shape=[jax.ShapeDtypeStruct((batch * seq, 3 * LANES), F32),
                   jax.ShapeDtypeStruct((batch, GATE_USED, seq), F32),
                   jax.ShapeDtypeStruct((batch * seq, FOX_W), BF16)],
        scratch_shapes=[pltpu.VMEM((8, LANES), F32)],
        compiler_params=_params("parallel", "arbitrary"),
        name="gates")(gpre, bias, _fox_bias_placement())


def _fox_lane_rows():
    r = np.zeros((8, LANES), np.float32)
    r[0, :FOX_HD] = 1.0
    r[1, FOX_HD:] = 1.0
    r[2, FOX_HD:FOX_HD + 3] = 1.0
    r[3, 0:3] = 1.0
    r[4, FOX_HD] = 1.0
    r[5, 0] = 1.0
    return jnp.asarray(r, BF16)


def _fox_kernel(q_ref, k_ref, v_ref, cb_ref, mk_ref, o_ref, m_ref, acc_ref, *, tq, tk):
    i = pl.program_id(2)
    n_sub = tq // tk
    lo, hi = mk_ref[0:1, :], mk_ref[1:2, :]
    q2 = q_ref[...]
    qh = (q2 * lo + mk_ref[2:3, :], q2 * hi + mk_ref[3:4, :])
    one_a, one_b = mk_ref[4:5, :], mk_ref[5:6, :]

    m_ref[...] = jnp.full_like(m_ref, -jnp.inf)
    acc_ref[...] = jnp.zeros_like(acc_ref)

    def step(j, row0, masked):
        rows = tq - row0
        r0 = pl.multiple_of(j * tk, tk)
        ks = k_ref[pl.ds(r0, tk), :]
        vs = v_ref[pl.ds(r0, tk), :]
        cb = cb_ref[pl.ds(r0, tk), :]
        kh = (ks * lo + cb * hi, ks * hi + cb * lo)
        vh = (vs * lo + one_a, vs * hi + one_b)
        if masked:
            keep = (lax.broadcasted_iota(jnp.int32, (rows, tk), 1) + j * tk
                    <= lax.broadcasted_iota(jnp.int32, (rows, tk), 0) + (i * tq + row0))
        for h in range(2):
            s = lax.dot_general(qh[h][row0:, :], kh[h], NT_DIMS, preferred_element_type=F32)
            if masked:
                s = jnp.where(keep, s, -jnp.inf)
            m_old = m_ref[h, row0:, :]
            m_new = jnp.maximum(m_old, jnp.max(s, axis=-1, keepdims=True))
            alpha = jnp.exp2(m_old - m_new)
            pe = jnp.exp2(s - jnp.concatenate([m_new] * (tk // LANES), axis=1))
            acc_ref[h, row0:, :] = alpha * acc_ref[h, row0:, :] + jnp.dot(
                pe.astype(BF16), vh[h], preferred_element_type=F32)
            m_ref[h, row0:, :] = m_new

    def body(j, carry):
        step(j, 0, False)
        return carry

    lax.fori_loop(0, i * n_sub, body, 0)
    for d in range(n_sub):
        step(i * n_sub + d, d * tk, True)

    acc_a, acc_b = acc_ref[0], acc_ref[1]
    l_a = jnp.broadcast_to(acc_a[:, FOX_HD:FOX_HD + 1], (tq, LANES))
    l_b = jnp.broadcast_to(acc_b[:, 0:1], (tq, LANES))
    first = lax.broadcasted_iota(jnp.int32, (tq, LANES), 1) < FOX_HD
    o_ref[...] = jnp.where(first, acc_a / l_a, acc_b / l_b).astype(BF16)


def _fox(proj, cb, *, batch, seq, tq, tk):
    nq = seq // tq
    npair = FOX_HEADS // 2
    qb, kb, vb = FQ_OFF // LANES, FK_OFF // LANES, FV_OFF // LANES
    return pl.pallas_call(
        functools.partial(_fox_kernel, tq=tq, tk=tk),
        grid=(batch, npair, nq),
        in_specs=[
            pl.BlockSpec((tq, LANES), lambda b, p, i: (b * nq + i, qb + p)),
            pl.BlockSpec((seq, LANES), lambda b, p, i: (b, kb + p)),
            pl.BlockSpec((seq, LANES), lambda b, p, i: (b, vb + p)),
            pl.BlockSpec((seq, LANES), lambda b, p, i: (b, p)),
            pl.BlockSpec((8, LANES), lambda b, p, i: (0, 0)),
        ],
        out_specs=pl.BlockSpec((tq, LANES), lambda b, p, i: (b * nq + i, p)),
        out_shape=jax.ShapeDtypeStruct((batch * seq, FOX_W), BF16),
        scratch_shapes=[pltpu.VMEM((2, tq, LANES), F32),
                        pltpu.VMEM((2, tq, LANES), F32)],
        compiler_params=_params("parallel", "parallel", "arbitrary"),
        name="fox_attention")(proj, proj, proj, cb, _fox_lane_rows())


def _mlstm_kernel(u_ref, v_ref, mo_ref, r_ref, ct_ref, sel_ref, cw_ref, cb_ref,
                  nw_ref, o_ref, ubuf, q_s, k_s, st_ref, m_ref, *, tt):
    s_idx = pl.program_id(1)
    halo = 8

    @pl.when(s_idx == 0)
    def _():
        ubuf[0:halo, :] = jnp.zeros((halo, 2 * ML_W), F32)
        st_ref[...] = jnp.zeros_like(st_ref)
        m_ref[...] = jnp.zeros_like(m_ref)

    @pl.when(s_idx > 0)
    def _():
        ubuf[0:halo, :] = ubuf[tt:tt + halo, :]

    c = ML_CHUNK
    for blk in range(tt // c):
        ubuf[halo + blk * c:halo + (blk + 1) * c, :] = (
            u_ref[blk * c:(blk + 1) * c, :].astype(F32))
    for blk in range(tt // c):
        acc = cb_ref[...]
        for j in (CONV_K - 1,) + tuple(range(CONV_K - 1)):
            off = halo - (CONV_K - 1) + j + blk * c
            acc = acc + cw_ref[j:j + 1, :] * ubuf[off:off + c, :]
        qk = acc * jax.nn.sigmoid(acc)
        q_s[blk * c:(blk + 1) * c, :] = qk[:, :ML_W].astype(BF16)
        k_s[blk * c:(blk + 1) * c, :] = (qk[:, ML_W:] * (ML_HD ** -0.5)).astype(BF16)

    row = lax.broadcasted_iota(jnp.int32, (c, c), 0)
    col = lax.broadcasted_iota(jnp.int32, (c, c), 1)
    causal = col <= row
    ones_blk = jnp.ones((c, ML_HD), BF16)

    def chunk(ci, carry):
        r0 = pl.multiple_of(ci * c, c)
        cb = ct_ref[0, :, pl.ds(r0, c)]
        rep = []
        for qi in range(3):
            hi, mid, lo = _split3(r_ref[pl.ds(r0, c), qi * LANES:(qi + 1) * LANES])
            rep.append(jnp.dot(jnp.concatenate([hi, mid], axis=1), sel_ref[...],
                               preferred_element_type=F32)
                       + jnp.dot(lo, sel_ref[0:LANES, :], preferred_element_type=F32))
        for h in range(ML_HEADS):
            hs = slice(h * ML_HD, (h + 1) * ML_HD)
            qh = q_s[pl.ds(r0, c), hs]
            kh = k_s[pl.ds(r0, c), hs]
            vh = v_ref[pl.ds(r0, c), hs]
            f_t, b_t, cm_t = rep[0][:, hs], rep[1][:, hs], rep[2][:, hs]
            b_row = cb[GATE_MF + h:GATE_MF + h + 1, :]
            m_prev = m_ref[h:h + 1, :]

            g_t = jnp.maximum(m_prev, cm_t)
            dw = jnp.exp(jnp.where(causal, b_row - jnp.concatenate([g_t, g_t], axis=1),
                                   -jnp.inf))
            inter_w = jnp.exp(m_prev - g_t)
            sqk = lax.dot_general(qh, kh, NT_DIMS, preferred_element_type=F32) * dw
            state = st_ref[h]
            qc = jnp.dot(qh, state.astype(BF16), preferred_element_type=F32)
            v_aug = jnp.concatenate([vh, ones_blk], axis=1)
            pv = jnp.dot(sqk.astype(BF16), v_aug, preferred_element_type=F32)
            num = inter_w * qc[:, :ML_HD] + pv[:, :ML_HD]
            den = inter_w * qc[:, ML_HD:] + pv[:, ML_HD:]
            hh = num / jnp.maximum(jnp.abs(den), jnp.exp(-(f_t + g_t)))
            ms = jnp.mean(hh * hh, axis=-1, keepdims=True)
            y = (hh * lax.rsqrt(ms + EPS) * nw_ref[:, hs]
                 * jax.nn.sigmoid(mo_ref[pl.ds(r0, c), hs].astype(F32)))
            o_ref[pl.ds(r0, c), hs] = y.astype(BF16)

            f_last = f_t[c - 1:c, :]
            m_new = jnp.maximum(f_last + m_prev, f_last + cm_t[c - 1:c, :])
            s_old = jnp.exp(f_last + m_prev - m_new)
            kw = (kh.astype(F32) * jnp.exp(f_last + b_t - m_new)).astype(BF16)
            upd = lax.dot_general(kw, v_aug, TN_DIMS, preferred_element_type=F32)
            st_ref[h] = jnp.concatenate([s_old, s_old], axis=1) * state + upd
            m_ref[h:h + 1, :] = m_new
        return carry

    lax.fori_loop(0, tt // c, chunk, 0)


def _mlstm(proj, rows, ct, conv_w, conv_b, norm_w, *, batch, seq, tt):
    ns = seq // tt
    ub = MQK_OFF // (2 * ML_W)
    vb = MV_OFF // ML_W
    ob = MO_OFF // ML_W
    rmap = lambda b, s: (b * ns + s, 0)
    sel = np.zeros((2, LANES, ML_W), np.float32)
    for h in range(ML_HEADS):
        sel[:, GATE_MF + h, h * ML_HD:(h + 1) * ML_HD] = 1.0
    sel = jnp.asarray(sel.reshape(2 * LANES, ML_W), BF16)
    return pl.pallas_call(
        functools.partial(_mlstm_kernel, tt=tt),
        grid=(batch, ns),
        in_specs=[
            pl.BlockSpec((tt, 2 * ML_W), lambda b, s: (b * ns + s, ub)),
            pl.BlockSpec((tt, ML_W), lambda b, s: (b * ns + s, vb)),
            pl.BlockSpec((tt, ML_W), lambda b, s: (b * ns + s, ob)),
            pl.BlockSpec((tt, 3 * LANES), rmap),
            pl.BlockSpec((1, GATE_USED, tt), lambda b, s: (b, 0, s)),
            pl.BlockSpec((2 * LANES, ML_W), lambda b, s: (0, 0)),
            pl.BlockSpec((CONV_K, 2 * ML_W), lambda b, s: (0, 0)),
            pl.BlockSpec((1, 2 * ML_W), lambda b, s: (0, 0)),
            pl.BlockSpec((1, ML_W), lambda b, s: (0, 0)),
        ],
        out_specs=pl.BlockSpec((tt, ML_W), rmap),
        out_shape=jax.ShapeDtypeStruct((batch * seq, ML_W), BF16),
        scratch_shapes=[pltpu.VMEM((tt + 8, 2 * ML_W), F32),
                        pltpu.VMEM((tt, ML_W), BF16),
                        pltpu.VMEM((tt, ML_W), BF16),
                        pltpu.VMEM((ML_HEADS, ML_HD, 2 * ML_HD), F32),
                        pltpu.VMEM((8, LANES), F32)],
        compiler_params=_params("parallel", "arbitrary"),
        name="mlstm")(proj, proj, proj, rows, ct, sel, conv_w, conv_b, norm_w)


def _merge_kernel(yf_ref, ym_ref, g_ref, bg_ref, h_ref, wf_ref, wm_ref, wo_ref, o_ref):
    g = jax.nn.sigmoid(g_ref[...].astype(F32) + bg_ref[...])
    uf = jnp.dot(yf_ref[...], wf_ref[...], preferred_element_type=F32)
    um = jnp.dot(ym_ref[...], wm_ref[...], preferred_element_type=F32)
    merged = g[:, :D_MODEL] * uf + g[:, D_MODEL:] * um
    o_ref[...] = h_ref[...] + jnp.dot(merged.astype(BF16), wo_ref[...],
                                      preferred_element_type=F32)


def _merge(yf, ym, proj, b_gate, h, wf, wm, wo, *, tm):
    m = h.shape[0]
    row = lambda i: (i, 0)
    const = lambda i: (0, 0)
    return pl.pallas_call(
        _merge_kernel, grid=(m // tm,),
        in_specs=[
            pl.BlockSpec((tm, FOX_W), row),
            pl.BlockSpec((tm, ML_W), row),
            pl.BlockSpec((tm, 2 * D_MODEL), lambda i: (i, G_OFF // (2 * D_MODEL))),
            pl.BlockSpec((1, 2 * D_MODEL), const),
            pl.BlockSpec((tm, D_MODEL), row),
            pl.BlockSpec((FOX_W, D_MODEL), const),
            pl.BlockSpec((ML_W, D_MODEL), const),
            pl.BlockSpec((D_MODEL, D_MODEL), const),
        ],
        out_specs=pl.BlockSpec((tm, D_MODEL), row),
        out_shape=jax.ShapeDtypeStruct((m, D_MODEL), F32),
        compiler_params=_params("parallel"),
        name="merge_mix_out")(yf, ym, proj, b_gate, h, wf, wm, wo)


def _xattn_kernel(h_ref, g_ref, wq_ref, k_ref, v_ref, wo_ref, o_ref):
    x = h_ref[...]
    hn = _rms(x, g_ref[...]).astype(BF16)
    q = jnp.dot(hn, wq_ref[...], preferred_element_type=F32)
    q = (q * (X_HD ** -0.5)).astype(BF16)
    outs = []
    for hd in range(X_HEADS):
        hs = slice(hd * X_HD, (hd + 1) * X_HD)
        s = lax.dot_general(q[:, hs], k_ref[:, hs], NT_DIMS, preferred_element_type=F32)
        e = jnp.exp(s - jnp.max(s, axis=-1, keepdims=True))
        pr = e / jnp.sum(e, axis=-1, keepdims=True)
        outs.append(jnp.dot(pr.astype(BF16), v_ref[:, hs], preferred_element_type=F32))
    o = jnp.concatenate(outs, axis=1).astype(BF16)
    o_ref[...] = x + jnp.dot(o, wo_ref[...], preferred_element_type=F32)


def _xattn(h, gain, wq, kv, wo, *, batch, seq, n_mem, tm):
    ns = seq // tm
    const = lambda b, s: (0, 0)
    return pl.pallas_call(
        _xattn_kernel, grid=(batch, ns),
        in_specs=[
            pl.BlockSpec((tm, D_MODEL), lambda b, s: (b * ns + s, 0)),
            pl.BlockSpec((1, D_MODEL), const),
            pl.BlockSpec((D_MODEL, D_MODEL), const),
            pl.BlockSpec((n_mem, D_MODEL), lambda b, s: (b, 0)),
            pl.BlockSpec((n_mem, D_MODEL), lambda b, s: (b, 1)),
            pl.BlockSpec((D_MODEL, D_MODEL), const),
        ],
        out_specs=pl.BlockSpec((tm, D_MODEL), lambda b, s: (b * ns + s, 0)),
        out_shape=jax.ShapeDtypeStruct((batch * seq, D_MODEL), F32),
        compiler_params=_params("parallel", "parallel"),
        name="cross_attention")(h, gain, wq, kv, kv, wo)


def _mlp_kernel(h_ref, g_ref, w1_ref, w2_ref, gf_ref, o_ref, *, tf, final):
    x = h_ref[...]
    hn = _rms(x, g_ref[...]).astype(BF16)
    acc = x
    for j in range(D_FF // tf):
        fs = slice(j * tf, (j + 1) * tf)
        a = jnp.maximum(jnp.dot(hn, w1_ref[:, fs], preferred_element_type=F32), 0.0)
        acc = acc + jnp.dot((a * a).astype(BF16), w2_ref[fs, :], preferred_element_type=F32)
    o_ref[...] = _rms(acc, gf_ref[...]) if final else acc


def _mlp(h, gain, w1, w2, gain_final, *, tm, tf, final):
    m = h.shape[0]
    const = lambda i: (0, 0)
    row = lambda i: (i, 0)
    return pl.pallas_call(
        functools.partial(_mlp_kernel, tf=tf, final=final),
        grid=(m // tm,),
        in_specs=[
            pl.BlockSpec((tm, D_MODEL), row),
            pl.BlockSpec((1, D_MODEL), const),
            pl.BlockSpec((D_MODEL, D_FF), const),
            pl.BlockSpec((D_FF, D_MODEL), const),
            pl.BlockSpec((1, D_MODEL), const),
        ],
        out_specs=pl.BlockSpec((tm, D_MODEL), row),
        out_shape=jax.ShapeDtypeStruct((m, D_MODEL), F32),
        compiler_params=_params("parallel"),
        name="mlp")(h, gain, w1, w2, gain_final)


def _pick(total, prefs):
    for t in prefs:
        if total % t == 0:
            return t
    raise ValueError(f"no tile in {prefs} divides {total}")


def kernel(x, mem, norm_mix, norm_xattn, norm_mem, norm_mlp, w_in, b_fox_f, b_mlstm_i,
           b_mlstm_f, b_gate, conv_w, conv_b, mlstm_norm, w_up_fox, w_up_mlstm, w_mix_out,
           w_xq, w_xkv, w_xo, w_mlp_in, w_mlp_out, norm_final):
    batch, seq, d = x.shape
    n_mem = mem.shape[1]
    depth = w_in.shape[0]
    assert d == D_MODEL and seq % ML_CHUNK == 0
    m = batch * seq

    tm_proj = _pick(m, (512, 256))
    tn_proj = 512
    t_seq = _pick(seq, (1024, 512, 256))
    tk = _pick(seq, (512, 256))
    tq = _pick(seq, (2 * tk, tk))
    tm_tok = _pick(m, (512, 256))
    tm_x = _pick(seq, (512, 256))
    tm_mlp = _pick(m, (512, 256))
    tf = 1024

    sp = [0, FOX_W, 2 * FOX_W, 3 * FOX_W, 3 * FOX_W + FOX_HEADS]
    sp += [sp[-1] + 2 * ML_W, sp[-1] + 3 * ML_W, sp[-1] + 4 * ML_W]
    sp += [sp[-1] + ML_HEADS, sp[-1] + 2 * ML_HEADS, sp[-1] + 2 * ML_HEADS + 2 * D_MODEL]
    seg = [w_in[:, :, a:b] for a, b in zip(sp[:-1], sp[1:])]
    fq, fk, fv, ff, mqk, mv, mo, mi, mf, gp = seg
    fq = fq * (FOX_HD ** -0.5 * LOG2E)
    w_main = jnp.concatenate([gp, mqk, fq, fk, fv, mv, mo], axis=2).astype(BF16)
    w_gate = jnp.concatenate(
        [ff, mi, mf, jnp.zeros((depth, d, LANES - GATE_USED), w_in.dtype)], axis=2).astype(BF16)
    gate_bias = jnp.concatenate(
        [b_fox_f, b_mlstm_i, b_mlstm_f, jnp.zeros((depth, LANES - GATE_USED), F32)], axis=1)
    bf = lambda w: w.astype(BF16)
    w_up_fox, w_up_mlstm, w_mix_out = bf(w_up_fox), bf(w_up_mlstm), bf(w_mix_out)
    w_xq, w_xkv, w_xo = bf(w_xq), bf(w_xkv), bf(w_xo)
    w_mlp_in, w_mlp_out = bf(w_mlp_in), bf(w_mlp_out)

    h = x.reshape(m, d)
    mem2 = mem.reshape(batch * n_mem, d)
    for l in range(depth):
        proj, gpre = _in_proj(h, norm_mix[l][None], w_main[l], w_gate[l],
                              tm=tm_proj, tn=tn_proj)
        rows, ct, cb = _gates(gpre, gate_bias[l][None], batch=batch, seq=seq, tg=t_seq)
        y_fox = _fox(proj, cb, batch=batch, seq=seq, tq=tq, tk=tk)
        y_ml = _mlstm(proj, rows, ct, conv_w[l], conv_b[l][None], mlstm_norm[l][None],
                      batch=batch, seq=seq, tt=t_seq)
        h = _merge(y_fox, y_ml, proj, b_gate[l][None], h,
                   w_up_fox[l], w_up_mlstm[l], w_mix_out[l], tm=tm_tok)
        kv = _norm_proj(mem2, norm_mem[l][None], w_xkv[l],
                        tm=_pick(batch * n_mem, (1024, 512, 256)), tn=1024)
        h = _xattn(h, norm_xattn[l][None], w_xq[l], kv, w_xo[l],
                   batch=batch, seq=seq, n_mem=n_mem, tm=tm_x)
        h = _mlp(h, norm_mlp[l][None], w_mlp_in[l], w_mlp_out[l], norm_final[None],
                 tm=tm_mlp, tf=tf, final=(l == depth - 1))
    return h.reshape(batch, seq, d)
```

```python
import functools

import jax
import jax.numpy as jnp
import numpy as np
from jax import lax
from jax.experimental import pallas as pl
from jax.experimental.pallas import tpu as pltpu

F32 = jnp.float32
BF16 = jnp.bfloat16

D_MODEL = 1024
FOX_HEADS = 8
FOX_HD = 64
FOX_W = FOX_HEADS * FOX_HD
ML_HEADS = 4
ML_HD = 128
ML_W = ML_HEADS * ML_HD
CONV_K = 4
X_HEADS = 4
X_HD = D_MODEL // X_HEADS
D_FF = 4 * D_MODEL
EPS = 1e-6
LOG2E = 1.4426950408889634

LANES = 128
V7X_VMEM_LIMIT_BYTES = 56 * 1024 * 1024

G_OFF = 0
MQK_OFF = G_OFF + 2 * D_MODEL
FQ_OFF = MQK_OFF + 2 * ML_W
FK_OFF = FQ_OFF + FOX_W
FV_OFF = FK_OFF + FOX_W
MV_OFF = FV_OFF + FOX_W
MO_OFF = MV_OFF + ML_W
PROJ_W = MO_OFF + ML_W
GATE_FF = 0
GATE_MI = GATE_FF + FOX_HEADS
GATE_MF = GATE_MI + ML_HEADS
GATE_USED = GATE_MF + ML_HEADS

ML_CHUNK = 256
NT_DIMS = (((1,), (1,)), ((), ()))
TN_DIMS = (((0,), (0,)), ((), ()))


def _params(*sem):
    return pltpu.CompilerParams(dimension_semantics=sem,
                                vmem_limit_bytes=V7X_VMEM_LIMIT_BYTES)


def _rms(x, g):
    return x * lax.rsqrt(jnp.mean(x * x, axis=-1, keepdims=True) + EPS) * g


def _in_proj_kernel(x_ref, g_ref, w_ref, wg_ref, o_ref, og_ref, *, tn):
    xn = _rms(x_ref[...], g_ref[...]).astype(BF16)
    og_ref[...] = jnp.dot(xn, wg_ref[...], preferred_element_type=F32)
    for j in range(w_ref.shape[1] // tn):
        cs = slice(j * tn, (j + 1) * tn)
        o_ref[:, cs] = jnp.dot(xn, w_ref[:, cs], preferred_element_type=F32).astype(BF16)


def _in_proj(x, gain, w, wg, *, tm, tn):
    m, d = x.shape
    n = w.shape[1]
    const = lambda i: (0, 0)
    return pl.pallas_call(
        functools.partial(_in_proj_kernel, tn=tn), grid=(m // tm,),
        in_specs=[pl.BlockSpec((tm, d), lambda i: (i, 0)),
                  pl.BlockSpec((1, d), const),
                  pl.BlockSpec((d, n), const),
                  pl.BlockSpec((d, LANES), const)],
        out_specs=[pl.BlockSpec((tm, n), lambda i: (i, 0)),
                   pl.BlockSpec((tm, LANES), lambda i: (i, 0))],
        out_shape=[jax.ShapeDtypeStruct((m, n), BF16),
                   jax.ShapeDtypeStruct((m, LANES), F32)],
        compiler_params=_params("parallel"),
        name="norm_in_proj")(x, gain, w, wg)


def _norm_proj_kernel(x_ref, g_ref, w_ref, o_ref, xn_ref):
    @pl.when(pl.program_id(1) == 0)
    def _():
        xn_ref[...] = _rms(x_ref[...], g_ref[...]).astype(BF16)

    o_ref[...] = jnp.dot(xn_ref[...], w_ref[...],
                         preferred_element_type=F32).astype(BF16)


def _norm_proj(x, gain, w, *, tm, tn):
    m, d = x.shape
    n = w.shape[1]
    return pl.pallas_call(
        _norm_proj_kernel, grid=(m // tm, n // tn),
        in_specs=[pl.BlockSpec((tm, d), lambda i, j: (i, 0)),
                  pl.BlockSpec((1, d), lambda i, j: (0, 0)),
                  pl.BlockSpec((d, tn), lambda i, j: (0, j))],
        out_specs=pl.BlockSpec((tm, tn), lambda i, j: (i, j)),
        out_shape=jax.ShapeDtypeStruct((m, n), BF16),
        scratch_shapes=[pltpu.VMEM((tm, d), BF16)],
        compiler_params=_params("parallel", "arbitrary"),
        name="norm_proj")(x, gain, w)


def _split3(v):
    hi = v.astype(BF16)
    r1 = v - hi.astype(F32)
    mid = r1.astype(BF16)
    lo = (r1 - mid.astype(F32)).astype(BF16)
    return hi, mid, lo


def _gates_kernel(x_ref, b_ref, pm_ref, r_ref, ct_ref, cb_ref, carry_ref, *, tg):
    @pl.when(pl.program_id(1) == 0)
    def _():
        carry_ref[...] = jnp.zeros_like(carry_ref)

    c = ML_CHUNK
    row = lax.broadcasted_iota(jnp.int32, (c, c), 0)
    col = lax.broadcasted_iota(jnp.int32, (c, c), 1)
    causal = row >= col
    tri = jnp.where(causal, 1.0, 0.0).astype(BF16)
    lane = lax.broadcasted_iota(jnp.int32, (c, LANES), 1)
    is_fox = lane < GATE_MI
    is_ig = (lane >= GATE_MI) & (lane < GATE_MF)
    is_used = lane < GATE_USED
    bias = b_ref[...]
    for blk in range(tg // c):
        rs = slice(blk * c, (blk + 1) * c)
        x = x_ref[rs, :] + bias
        logsig = jnp.minimum(x, 0.0) - jnp.log1p(jnp.exp(-jnp.abs(x)))
        val = jnp.where(is_ig, x, logsig)
        val = jnp.where(is_used, val, 0.0)
        csp = jnp.dot(tri, jnp.concatenate(_split3(val), axis=1), preferred_element_type=F32)
        cs = csp[:, 0:LANES] + csp[:, LANES:2 * LANES] + csp[:, 2 * LANES:3 * LANES]
        glob = cs + carry_ref[0:1, :]
        carry_ref[0:1, :] = glob[c - 1:c, :]
        b_t = pltpu.roll(val, GATE_MF - GATE_MI, axis=1) - cs
        bt = b_t.T
        ct_ref[0, :, rs] = bt[:GATE_USED, :]
        cm = jnp.zeros((c, LANES), F32)
        for h in range(ML_HEADS):
            k = GATE_MF + h
            run_max = jnp.max(jnp.where(causal, bt[k:k + 1, :], -jnp.inf),
                              axis=-1, keepdims=True)
            cm = jnp.where(lane == k, run_max, cm)
        r_ref[rs, 0:LANES] = cs
        r_ref[rs, LANES:2 * LANES] = b_t
        r_ref[rs, 2 * LANES:3 * LANES] = cm
        terms = _split3(jnp.where(is_fox, glob * (-LOG2E), 0.0))
        cb_ref[rs, :] = jnp.dot(jnp.concatenate(terms, axis=1), pm_ref[...],
                                preferred_element_type=F32).astype(BF16)


def _fox_bias_placement():
    pm = np.zeros((3, LANES, FOX_W), np.float32)
    for h in range(FOX_HEADS):
        base = (h // 2) * LANES + (FOX_HD if h % 2 == 0 else 0)
        for t in range(3):
            pm[t, h, base + t] = 1.0
    return jnp.asarray(pm.reshape(3 * LANES, FOX_W), BF16)


def _gates(gpre, bias, *, batch, seq, tg):
    ns = seq // tg
    rmap = lambda b, s: (b * ns + s, 0)
    return pl.pallas_call(
        functools.partial(_gates_kernel, tg=tg),
        grid=(batch, ns),
        in_specs=[pl.BlockSpec((tg, LANES), rmap),
                  pl.BlockSpec((1, LANES), lambda b, s: (0, 0)),
                  pl.BlockSpec((3 * LANES, FOX_W), lambda b, s: (0, 0))],
        out_specs=[pl.BlockSpec((tg, 3 * LANES), rmap),
                   pl.BlockSpec((1, GATE_USED, tg), lambda b, s: (b, 0, s)),
                   pl.BlockSpec((tg, FOX_W), rmap)],
        out_shape=[jax.ShapeDtypeStruct((batch * seq, 3 * LANES), F32),
                   jax.ShapeDtypeStruct((batch, GATE_USED, seq), F32),
                   jax.ShapeDtypeStruct((batch * seq, FOX_W), BF16)],
        scratch_shapes=[pltpu.VMEM((8, LANES), F32)],
        compiler_params=_params("parallel", "arbitrary"),
        name="gates")(gpre, bias, _fox_bias_placement())


def _fox_lane_rows():
    r = np.zeros((8, LANES), np.float32)
    r[0, :FOX_HD] = 1.0
    r[1, FOX_HD:] = 1.0
    r[2, FOX_HD:FOX_HD + 3] = 1.0
    r[3, 0:3] = 1.0
    r[4, FOX_HD] = 1.0
    r[5, 0] = 1.0
    return jnp.asarray(r, BF16)


def _fox_kernel(q_ref, k_ref, v_ref, cb_ref, mk_ref, o_ref, m_ref, acc_ref, *, tq, tk):
    i = pl.program_id(2)
    n_sub = tq // tk
    npp = q_ref.shape[1] // LANES
    lo, hi = mk_ref[0:1, :], mk_ref[1:2, :]
    one_a, one_b = mk_ref[4:5, :], mk_ref[5:6, :]
    qh = []
    for pp in range(npp):
        q2 = q_ref[:, pp * LANES:(pp + 1) * LANES]
        qh += [q2 * lo + mk_ref[2:3, :], q2 * hi + mk_ref[3:4, :]]

    m_ref[...] = jnp.full_like(m_ref, -jnp.inf)
    acc_ref[...] = jnp.zeros_like(acc_ref)

    def step(j, row0, masked):
        rows = tq - row0
        r0 = pl.multiple_of(j * tk, tk)
        if masked:
            keep = (lax.broadcasted_iota(jnp.int32, (rows, tk), 1) + j * tk
                    <= lax.broadcasted_iota(jnp.int32, (rows, tk), 0) + (i * tq + row0))
        for pp in range(npp):
            ps = slice(pp * LANES, (pp + 1) * LANES)
            ks = k_ref[pl.ds(r0, tk), ps]
            vs = v_ref[pl.ds(r0, tk), ps]
            cb = cb_ref[pl.ds(r0, tk), ps]
            kh = (ks * lo + cb * hi, ks * hi + cb * lo)
            vh = (vs * lo + one_a, vs * hi + one_b)
            for h in range(2):
                hh = 2 * pp + h
                s = lax.dot_general(qh[hh][row0:, :], kh[h], NT_DIMS,
                                    preferred_element_type=F32)
                if masked:
                    s = jnp.where(keep, s, -jnp.inf)
                m_old = m_ref[hh, row0:, :]
                m_new = jnp.maximum(m_old, jnp.max(s, axis=-1, keepdims=True))
                alpha = jnp.exp2(m_old - m_new)
                pe = jnp.exp2(s - jnp.concatenate([m_new] * (tk // LANES), axis=1))
                acc_ref[hh, row0:, :] = alpha * acc_ref[hh, row0:, :] + jnp.dot(
                    pe.astype(BF16), vh[h], preferred_element_type=F32)
                m_ref[hh, row0:, :] = m_new

    def body(j, carry):
        step(j, 0, False)
        return carry

    lax.fori_loop(0, i * n_sub, body, 0)
    for d in range(n_sub):
        step(i * n_sub + d, d * tk, True)

    first = lax.broadcasted_iota(jnp.int32, (tq, LANES), 1) < FOX_HD
    for pp in range(npp):
        acc_a, acc_b = acc_ref[2 * pp], acc_ref[2 * pp + 1]
        l_a = jnp.broadcast_to(acc_a[:, FOX_HD:FOX_HD + 1], (tq, LANES))
        l_b = jnp.broadcast_to(acc_b[:, 0:1], (tq, LANES))
        o_ref[:, pp * LANES:(pp + 1) * LANES] = jnp.where(
            first, acc_a / l_a, acc_b / l_b).astype(BF16)


def _fox(proj, cb, *, batch, seq, tq, tk, npp):
    nq = seq // tq
    ngrp = FOX_HEADS // 2 // npp
    w = npp * LANES
    qb, kb, vb = FQ_OFF // w, FK_OFF // w, FV_OFF // w
    return pl.pallas_call(
        functools.partial(_fox_kernel, tq=tq, tk=tk),
        grid=(batch, ngrp, nq),
        in_specs=[
            pl.BlockSpec((tq, w), lambda b, p, i: (b * nq + i, qb + p)),
            pl.BlockSpec((seq, w), lambda b, p, i: (b, kb + p)),
            pl.BlockSpec((seq, w), lambda b, p, i: (b, vb + p)),
            pl.BlockSpec((seq, w), lambda b, p, i: (b, p)),
            pl.BlockSpec((8, LANES), lambda b, p, i: (0, 0)),
        ],
        out_specs=pl.BlockSpec((tq, w), lambda b, p, i: (b * nq + i, p)),
        out_shape=jax.ShapeDtypeStruct((batch * seq, FOX_W), BF16),
        scratch_shapes=[pltpu.VMEM((2 * npp, tq, LANES), F32),
                        pltpu.VMEM((2 * npp, tq, LANES), F32)],
        compiler_params=_params("parallel", "parallel", "arbitrary"),
        name="fox_attention")(proj, proj, proj, cb, _fox_lane_rows())


def _mlstm_kernel(u_ref, v_ref, mo_ref, r_ref, ct_ref, sel_ref, cw_ref, cb_ref,
                  nw_ref, o_ref, ubuf, q_s, k_s, st_ref, m_ref, *, tt):
    s_idx = pl.program_id(1)
    halo = 8

    @pl.when(s_idx == 0)
    def _():
        ubuf[0:halo, :] = jnp.zeros((halo, 2 * ML_W), F32)
        st_ref[...] = jnp.zeros_like(st_ref)
        m_ref[...] = jnp.zeros_like(m_ref)

    @pl.when(s_idx > 0)
    def _():
        ubuf[0:halo, :] = ubuf[tt:tt + halo, :]

    c = ML_CHUNK
    for blk in range(tt // c):
        ubuf[halo + blk * c:halo + (blk + 1) * c, :] = (
            u_ref[blk * c:(blk + 1) * c, :].astype(F32))

    def conv_block(blk):
        acc = cb_ref[...]
        for j in (CONV_K - 1,) + tuple(range(CONV_K - 1)):
            off = halo - (CONV_K - 1) + j + blk * c
            acc = acc + cw_ref[j:j + 1, :] * ubuf[off:off + c, :]
        qk = acc * jax.nn.sigmoid(acc)
        q_s[blk * c:(blk + 1) * c, :] = qk[:, :ML_W].astype(BF16)
        k_s[blk * c:(blk + 1) * c, :] = (qk[:, ML_W:] * (ML_HD ** -0.5)).astype(BF16)

    row = lax.broadcasted_iota(jnp.int32, (c, c), 0)
    col = lax.broadcasted_iota(jnp.int32, (c, c), 1)
    causal = col <= row
    ones_blk = jnp.ones((c, ML_HD), BF16)

    def chunk(ci):
        r0 = ci * c
        cb = ct_ref[0, :, pl.ds(r0, c)]
        rep = []
        for qi in range(3):
            hi, mid, lo = _split3(r_ref[pl.ds(r0, c), qi * LANES:(qi + 1) * LANES])
            rep.append(jnp.dot(jnp.concatenate([hi, mid], axis=1), sel_ref[...],
                               preferred_element_type=F32)
                       + jnp.dot(lo, sel_ref[0:LANES, :], preferred_element_type=F32))
        for h in range(ML_HEADS):
            hs = slice(h * ML_HD, (h + 1) * ML_HD)
            qh = q_s[pl.ds(r0, c), hs]
            kh = k_s[pl.ds(r0, c), hs]
            vh = v_ref[pl.ds(r0, c), hs]
            f_t, b_t, cm_t = rep[0][:, hs], rep[1][:, hs], rep[2][:, hs]
            b_row = cb[GATE_MF + h:GATE_MF + h + 1, :]
            m_prev = m_ref[h:h + 1, :]

            g_t = jnp.maximum(m_prev, cm_t)
            dw = jnp.exp(jnp.where(causal, b_row - jnp.concatenate([g_t, g_t], axis=1),
                                   -jnp.inf))
            inter_w = jnp.exp(m_prev - g_t)
            sqk = lax.dot_general(qh, kh, NT_DIMS, preferred_element_type=F32) * dw
            state = st_ref[h]
            qc = jnp.dot(qh, state.astype(BF16), preferred_element_type=F32)
            v_aug = jnp.concatenate([vh, ones_blk], axis=1)
            pv = jnp.dot(sqk.astype(BF16), v_aug, preferred_element_type=F32)
            num = inter_w * qc[:, :ML_HD] + pv[:, :ML_HD]
            den = inter_w * qc[:, ML_HD:] + pv[:, ML_HD:]
            hh = num / jnp.maximum(jnp.abs(den), jnp.exp(-(f_t + g_t)))
            ms = jnp.mean(hh * hh, axis=-1, keepdims=True)
            y = (hh * lax.rsqrt(ms + EPS) * nw_ref[:, hs]
                 * jax.nn.sigmoid(mo_ref[pl.ds(r0, c), hs].astype(F32)))
            o_ref[pl.ds(r0, c), hs] = y.astype(BF16)

            f_last = f_t[c - 1:c, :]
            m_new = jnp.maximum(f_last + m_prev, f_last + cm_t[c - 1:c, :])
            s_old = jnp.exp(f_last + m_prev - m_new)
            kw = (kh.astype(F32) * jnp.exp(f_last + b_t - m_new)).astype(BF16)
            upd = lax.dot_general(kw, v_aug, TN_DIMS, preferred_element_type=F32)
            st_ref[h] = jnp.concatenate([s_old, s_old], axis=1) * state + upd
            m_ref[h:h + 1, :] = m_new

    for ci in range(tt // c):
        conv_block(ci)
        chunk(ci)


def _mlstm(proj, rows, ct, conv_w, conv_b, norm_w, *, batch, seq, tt):
    ns = seq // tt
    ub = MQK_OFF // (2 * ML_W)
    vb = MV_OFF // ML_W
    ob = MO_OFF // ML_W
    rmap = lambda b, s: (b * ns + s, 0)
    sel = np.zeros((2, LANES, ML_W), np.float32)
    for h in range(ML_HEADS):
        sel[:, GATE_MF + h, h * ML_HD:(h + 1) * ML_HD] = 1.0
    sel = jnp.asarray(sel.reshape(2 * LANES, ML_W), BF16)
    return pl.pallas_call(
        functools.partial(_mlstm_kernel, tt=tt),
        grid=(batch, ns),
        in_specs=[
            pl.BlockSpec((tt, 2 * ML_W), lambda b, s: (b * ns + s, ub)),
            pl.BlockSpec((tt, ML_W), lambda b, s: (b * ns + s, vb)),
            pl.BlockSpec((tt, ML_W), lambda b, s: (b * ns + s, ob)),
            pl.BlockSpec((tt, 3 * LANES), rmap),
            pl.BlockSpec((1, GATE_USED, tt), lambda b, s: (b, 0, s)),
            pl.BlockSpec((2 * LANES, ML_W), lambda b, s: (0, 0)),
            pl.BlockSpec((CONV_K, 2 * ML_W), lambda b, s: (0, 0)),
            pl.BlockSpec((1, 2 * ML_W), lambda b, s: (0, 0)),
            pl.BlockSpec((1, ML_W), lambda b, s: (0, 0)),
        ],
        out_specs=pl.BlockSpec((tt, ML_W), rmap),
        out_shape=jax.ShapeDtypeStruct((batch * seq, ML_W), BF16),
        scratch_shapes=[pltpu.VMEM((tt + 8, 2 * ML_W), F32),
                        pltpu.VMEM((tt, ML_W), BF16),
                        pltpu.VMEM((tt, ML_W), BF16),
                        pltpu.VMEM((ML_HEADS, ML_HD, 2 * ML_HD), F32),
                        pltpu.VMEM((8, LANES), F32)],
        compiler_params=_params("parallel", "arbitrary"),
        name="mlstm")(proj, proj, proj, rows, ct, sel, conv_w, conv_b, norm_w)


def _mix_xattn_kernel(yf_ref, ym_ref, gt_ref, bg_ref, h_ref, wf_ref, wm_ref, wmix_ref,
                      g_ref, wq_ref, k_ref, v_ref, wo_ref, o_ref):
    g = jax.nn.sigmoid(gt_ref[...].astype(F32) + bg_ref[...])
    uf = jnp.dot(yf_ref[...], wf_ref[...], preferred_element_type=F32)
    um = jnp.dot(ym_ref[...], wm_ref[...], preferred_element_type=F32)
    merged = g[:, :D_MODEL] * uf + g[:, D_MODEL:] * um
    x = h_ref[...] + jnp.dot(merged.astype(BF16), wmix_ref[...], preferred_element_type=F32)

    hn = _rms(x, g_ref[...]).astype(BF16)
    q = jnp.dot(hn, wq_ref[...], preferred_element_type=F32)
    q = (q * (X_HD ** -0.5)).astype(BF16)
    outs = []
    for hd in range(X_HEADS):
        hs = slice(hd * X_HD, (hd + 1) * X_HD)
        s = lax.dot_general(q[:, hs], k_ref[:, hs], NT_DIMS, preferred_element_type=F32)
        e = jnp.exp(s - jnp.max(s, axis=-1, keepdims=True))
        pr = e / jnp.sum(e, axis=-1, keepdims=True)
        outs.append(jnp.dot(pr.astype(BF16), v_ref[:, hs], preferred_element_type=F32))
    o = jnp.concatenate(outs, axis=1).astype(BF16)
    o_ref[...] = x + jnp.dot(o, wo_ref[...], preferred_element_type=F32)


def _mix_xattn(yf, ym, proj, b_gate, h, wf, wm, wmix, gain, wq, kv, wo, *,
               batch, seq, n_mem, tm):
    ns = seq // tm
    const = lambda b, s: (0, 0)
    row = lambda b, s: (b * ns + s, 0)
    return pl.pallas_call(
        _mix_xattn_kernel, grid=(batch, ns),
        in_specs=[
            pl.BlockSpec((tm, FOX_W), row),
            pl.BlockSpec((tm, ML_W), row),
            pl.BlockSpec((tm, 2 * D_MODEL), lambda b, s: (b * ns + s, G_OFF // (2 * D_MODEL))),
            pl.BlockSpec((1, 2 * D_MODEL), const),
            pl.BlockSpec((tm, D_MODEL), row),
            pl.BlockSpec((FOX_W, D_MODEL), const),
            pl.BlockSpec((ML_W, D_MODEL), const),
            pl.BlockSpec((D_MODEL, D_MODEL), const),
            pl.BlockSpec((1, D_MODEL), const),
            pl.BlockSpec((D_MODEL, D_MODEL), const),
            pl.BlockSpec((n_mem, D_MODEL), lambda b, s: (b, 0)),
            pl.BlockSpec((n_mem, D_MODEL), lambda b, s: (b, 1)),
            pl.BlockSpec((D_MODEL, D_MODEL), const),
        ],
        out_specs=pl.BlockSpec((tm, D_MODEL), row),
        out_shape=jax.ShapeDtypeStruct((batch * seq, D_MODEL), F32),
        compiler_params=_params("parallel", "parallel"),
        name="mix_cross_attention")(yf, ym, proj, b_gate, h, wf, wm, wmix,
                                    gain, wq, kv, kv, wo)


def _mlp_kernel(h_ref, g_ref, w1_ref, w2_ref, gf_ref, o_ref, *, tf, final):
    x = h_ref[...]
    hn = _rms(x, g_ref[...]).astype(BF16)
    acc = x
    for j in range(D_FF // tf):
        fs = slice(j * tf, (j + 1) * tf)
        a = jnp.maximum(jnp.dot(hn, w1_ref[:, fs], preferred_element_type=F32), 0.0)
        acc = acc + jnp.dot((a * a).astype(BF16), w2_ref[fs, :], preferred_element_type=F32)
    o_ref[...] = _rms(acc, gf_ref[...]) if final else acc


def _mlp(h, gain, w1, w2, gain_final, *, tm, tf, final):
    m = h.shape[0]
    const = lambda i: (0, 0)
    row = lambda i: (i, 0)
    return pl.pallas_call(
        functools.partial(_mlp_kernel, tf=tf, final=final),
        grid=(m // tm,),
        in_specs=[
            pl.BlockSpec((tm, D_MODEL), row),
            pl.BlockSpec((1, D_MODEL), const),
            pl.BlockSpec((D_MODEL, D_FF), const),
            pl.BlockSpec((D_FF, D_MODEL), const),
            pl.BlockSpec((1, D_MODEL), const),
        ],
        out_specs=pl.BlockSpec((tm, D_MODEL), row),
        out_shape=jax.ShapeDtypeStruct((m, D_MODEL), F32),
        compiler_params=_params("parallel"),
        name="mlp")(h, gain, w1, w2, gain_final)


def _pick(total, prefs):
    for t in prefs:
        if total % t == 0:
            return t
    raise ValueError(f"no tile in {prefs} divides {total}")


def kernel(x, mem, norm_mix, norm_xattn, norm_mem, norm_mlp, w_in, b_fox_f, b_mlstm_i,
           b_mlstm_f, b_gate, conv_w, conv_b, mlstm_norm, w_up_fox, w_up_mlstm, w_mix_out,
           w_xq, w_xkv, w_xo, w_mlp_in, w_mlp_out, norm_final):
    batch, seq, d = x.shape
    n_mem = mem.shape[1]
    depth = w_in.shape[0]
    assert d == D_MODEL and seq % ML_CHUNK == 0
    m = batch * seq

    tm_proj = _pick(m, (512, 256))
    tn_proj = 512
    t_seq = _pick(seq, (1024, 512, 256))
    tk = _pick(seq, (512, 256))
    tq = _pick(seq, (2 * tk, tk))
    tm_x = _pick(seq, (512, 256))
    tm_mlp = _pick(m, (512, 256))
    tf = 1024

    sp = [0, FOX_W, 2 * FOX_W, 3 * FOX_W, 3 * FOX_W + FOX_HEADS]
    sp += [sp[-1] + 2 * ML_W, sp[-1] + 3 * ML_W, sp[-1] + 4 * ML_W]
    sp += [sp[-1] + ML_HEADS, sp[-1] + 2 * ML_HEADS, sp[-1] + 2 * ML_HEADS + 2 * D_MODEL]
    seg = [w_in[:, :, a:b] for a, b in zip(sp[:-1], sp[1:])]
    fq, fk, fv, ff, mqk, mv, mo, mi, mf, gp = seg
    fq = fq * (FOX_HD ** -0.5 * LOG2E)
    w_main = jnp.concatenate([gp, mqk, fq, fk, fv, mv, mo], axis=2).astype(BF16)
    w_gate = jnp.concatenate(
        [ff, mi, mf, jnp.zeros((depth, d, LANES - GATE_USED), w_in.dtype)], axis=2).astype(BF16)
    gate_bias = jnp.concatenate(
        [b_fox_f, b_mlstm_i, b_mlstm_f, jnp.zeros((depth, LANES - GATE_USED), F32)], axis=1)
    bf = lambda w: w.astype(BF16)
    w_up_fox, w_up_mlstm, w_mix_out = bf(w_up_fox), bf(w_up_mlstm), bf(w_mix_out)
    w_xq, w_xkv, w_xo = bf(w_xq), bf(w_xkv), bf(w_xo)
    w_mlp_in, w_mlp_out = bf(w_mlp_in), bf(w_mlp_out)

    h = x.reshape(m, d)
    mem2 = mem.reshape(batch * n_mem, d)
    for l in range(depth):
        proj, gpre = _in_proj(h, norm_mix[l][None], w_main[l], w_gate[l],
                              tm=tm_proj, tn=tn_proj)
        rows, ct, cb = _gates(gpre, gate_bias[l][None], batch=batch, seq=seq, tg=t_seq)
        y_fox = _fox(proj, cb, batch=batch, seq=seq, tq=tq, tk=tk, npp=2)
        y_ml = _mlstm(proj, rows, ct, conv_w[l], conv_b[l][None], mlstm_norm[l][None],
                      batch=batch, seq=seq, tt=t_seq)
        kv = _norm_proj(mem2, norm_mem[l][None], w_xkv[l],
                        tm=_pick(batch * n_mem, (1024, 512, 256)), tn=1024)
        h = _mix_xattn(y_fox, y_ml, proj, b_gate[l][None], h,
                       w_up_fox[l], w_up_mlstm[l], w_mix_out[l],
                       norm_xattn[l][None], w_xq[l], kv, w_xo[l],
                       batch=batch, seq=seq, n_mem=n_mem, tm=tm_x)
        h = _mlp(h, norm_mlp[l][None], w_mlp_in[l], w_mlp_out[l], norm_final[None],
                 tm=tm_mlp, tf=tf, final=(l == depth - 1))
    return h.reshape(batch, seq, d)
```

```python
import functools

import jax
import jax.numpy as jnp
import numpy as np
from jax import lax
from jax.experimental import pallas as pl
from jax.experimental.pallas import tpu as pltpu

F32 = jnp.float32
BF16 = jnp.bfloat16

D_MODEL = 1024
FOX_HEADS = 8
FOX_HD = 64
FOX_W = FOX_HEADS * FOX_HD
ML_HEADS = 4
ML_HD = 128
ML_W = ML_HEADS * ML_HD
CONV_K = 4
X_HEADS = 4
X_HD = D_MODEL // X_HEADS
D_FF = 4 * D_MODEL
EPS = 1e-6
LOG2E = 1.4426950408889634

LANES = 128
V7X_VMEM_LIMIT_BYTES = 56 * 1024 * 1024

G_OFF = 0
MQK_OFF = G_OFF + 2 * D_MODEL
FQ_OFF = MQK_OFF + 2 * ML_W
FK_OFF = FQ_OFF + FOX_W
FV_OFF = FK_OFF + FOX_W
MV_OFF = FV_OFF + FOX_W
MO_OFF = MV_OFF + ML_W
PROJ_W = MO_OFF + ML_W
GATE_FF = 0
GATE_MI = GATE_FF + FOX_HEADS
GATE_MF = GATE_MI + ML_HEADS
GATE_USED = GATE_MF + ML_HEADS

ML_CHUNK = 256
NT_DIMS = (((1,), (1,)), ((), ()))
TN_DIMS = (((0,), (0,)), ((), ()))


def _params(*sem):
    return pltpu.CompilerParams(dimension_semantics=sem,
                                vmem_limit_bytes=V7X_VMEM_LIMIT_BYTES)


def _rms(x, g):
    return x * lax.rsqrt(jnp.mean(x * x, axis=-1, keepdims=True) + EPS) * g


def _in_proj_kernel(x_ref, g_ref, w_ref, wg_ref, o_ref, og_ref, *, tn):
    xn = _rms(x_ref[...], g_ref[...]).astype(BF16)
    og_ref[...] = jnp.dot(xn, wg_ref[...], preferred_element_type=F32)
    for j in range(w_ref.shape[1] // tn):
        cs = slice(j * tn, (j + 1) * tn)
        o_ref[:, cs] = jnp.dot(xn, w_ref[:, cs], preferred_element_type=F32).astype(BF16)


def _in_proj(x, gain, w, wg, *, tm, tn):
    m, d = x.shape
    n = w.shape[1]
    const = lambda i: (0, 0)
    return pl.pallas_call(
        functools.partial(_in_proj_kernel, tn=tn), grid=(m // tm,),
        in_specs=[pl.BlockSpec((tm, d), lambda i: (i, 0)),
                  pl.BlockSpec((1, d), const),
                  pl.BlockSpec((d, n), const),
                  pl.BlockSpec((d, LANES), const)],
        out_specs=[pl.BlockSpec((tm, n), lambda i: (i, 0)),
                   pl.BlockSpec((tm, LANES), lambda i: (i, 0))],
        out_shape=[jax.ShapeDtypeStruct((m, n), BF16),
                   jax.ShapeDtypeStruct((m, LANES), F32)],
        compiler_params=_params("parallel"),
        name="norm_in_proj")(x, gain, w, wg)


def _norm_proj_kernel(x_ref, g_ref, w_ref, o_ref, xn_ref):
    @pl.when(pl.program_id(1) == 0)
    def _():
        xn_ref[...] = _rms(x_ref[...], g_ref[...]).astype(BF16)

    o_ref[...] = jnp.dot(xn_ref[...], w_ref[...],
                         preferred_element_type=F32).astype(BF16)


def _norm_proj(x, gain, w, *, tm, tn):
    m, d = x.shape
    n = w.shape[1]
    return pl.pallas_call(
        _norm_proj_kernel, grid=(m // tm, n // tn),
        in_specs=[pl.BlockSpec((tm, d), lambda i, j: (i, 0)),
                  pl.BlockSpec((1, d), lambda i, j: (0, 0)),
                  pl.BlockSpec((d, tn), lambda i, j: (0, j))],
        out_specs=pl.BlockSpec((tm, tn), lambda i, j: (i, j)),
        out_shape=jax.ShapeDtypeStruct((m, n), BF16),
        scratch_shapes=[pltpu.VMEM((tm, d), BF16)],
        compiler_params=_params("parallel", "arbitrary"),
        name="norm_proj")(x, gain, w)


def _split3(v):
    hi = v.astype(BF16)
    r1 = v - hi.astype(F32)
    mid = r1.astype(BF16)
    lo = (r1 - mid.astype(F32)).astype(BF16)
    return hi, mid, lo


def _gates_kernel(x_ref, b_ref, pm_ref, r_ref, ct_ref, cb_ref, carry_ref, *, tg):
    @pl.when(pl.program_id(1) == 0)
    def _():
        carry_ref[...] = jnp.zeros_like(carry_ref)

    c = ML_CHUNK
    row = lax.broadcasted_iota(jnp.int32, (c, c), 0)
    col = lax.broadcasted_iota(jnp.int32, (c, c), 1)
    causal = row >= col
    tri = jnp.where(causal, 1.0, 0.0).astype(BF16)
    lane = lax.broadcasted_iota(jnp.int32, (c, LANES), 1)
    is_fox = lane < GATE_MI
    is_ig = (lane >= GATE_MI) & (lane < GATE_MF)
    is_used = lane < GATE_USED
    bias = b_ref[...]
    for blk in range(tg // c):
        rs = slice(blk * c, (blk + 1) * c)
        x = x_ref[rs, :] + bias
        logsig = jnp.minimum(x, 0.0) - jnp.log1p(jnp.exp(-jnp.abs(x)))
        val = jnp.where(is_ig, x, logsig)
        val = jnp.where(is_used, val, 0.0)
        csp = jnp.dot(tri, jnp.concatenate(_split3(val), axis=1), preferred_element_type=F32)
        cs = csp[:, 0:LANES] + csp[:, LANES:2 * LANES] + csp[:, 2 * LANES:3 * LANES]
        glob = cs + carry_ref[0:1, :]
        carry_ref[0:1, :] = glob[c - 1:c, :]
        b_t = pltpu.roll(val, GATE_MF - GATE_MI, axis=1) - cs
        bt = b_t.T
        ct_ref[0, :, rs] = bt[:GATE_USED, :]
        cm = jnp.zeros((c, LANES), F32)
        for h in range(ML_HEADS):
            k = GATE_MF + h
            run_max = jnp.max(jnp.where(causal, bt[k:k + 1, :], -jnp.inf),
                              axis=-1, keepdims=True)
            cm = jnp.where(lane == k, run_max, cm)
        r_ref[rs, 0:LANES] = cs
        r_ref[rs, LANES:2 * LANES] = b_t
        r_ref[rs, 2 * LANES:3 * LANES] = cm
        terms = _split3(jnp.where(is_fox, glob * (-LOG2E), 0.0))
        cb_ref[rs, :] = jnp.dot(jnp.concatenate(terms, axis=1), pm_ref[...],
                                preferred_element_type=F32).astype(BF16)


def _fox_bias_placement():
    pm = np.zeros((3, LANES, FOX_W), np.float32)
    for h in range(FOX_HEADS):
        base = (h // 2) * LANES + (FOX_HD if h % 2 == 0 else 0)
        for t in range(3):
            pm[t, h, base + t] = 1.0
    return jnp.asarray(pm.reshape(3 * LANES, FOX_W), BF16)


def _gates(gpre, bias, *, batch, seq, tg):
    ns = seq // tg
    rmap = lambda b, s: (b * ns + s, 0)
    return pl.pallas_call(
        functools.partial(_gates_kernel, tg=tg),
        grid=(batch, ns),
        in_specs=[pl.BlockSpec((tg, LANES), rmap),
                  pl.BlockSpec((1, LANES), lambda b, s: (0, 0)),
                  pl.BlockSpec((3 * LANES, FOX_W), lambda b, s: (0, 0))],
        out_specs=[pl.BlockSpec((tg, 3 * LANES), rmap),
                   pl.BlockSpec((1, GATE_USED, tg), lambda b, s: (b, 0, s)),
                   pl.BlockSpec((tg, FOX_W), rmap)],
        out_shape=[jax.ShapeDtypeStruct((batch * seq, 3 * LANES), F32),
                   jax.ShapeDtypeStruct((batch, GATE_USED, seq), F32),
                   jax.ShapeDtypeStruct((batch * seq, FOX_W), BF16)],
        scratch_shapes=[pltpu.VMEM((8, LANES), F32)],
        compiler_params=_params("parallel", "arbitrary"),
        name="gates")(gpre, bias, _fox_bias_placement())


def _fox_lane_rows():
    r = np.zeros((8, LANES), np.float32)
    r[0, :FOX_HD] = 1.0
    r[1, FOX_HD:] = 1.0
    r[2, FOX_HD:FOX_HD + 3] = 1.0
    r[3, 0:3] = 1.0
    r[4, FOX_HD] = 1.0
    r[5, 0] = 1.0
    return jnp.asarray(r, BF16)


def _fox_kernel(q_ref, k_ref, v_ref, cb_ref, mk_ref, o_ref, m_ref, acc_ref, *, tq, tk):
    i = pl.program_id(2)
    n_sub = tq // tk
    npp = q_ref.shape[1] // LANES
    lo, hi = mk_ref[0:1, :], mk_ref[1:2, :]
    one_a, one_b = mk_ref[4:5, :], mk_ref[5:6, :]
    qh = []
    for pp in range(npp):
        q2 = q_ref[:, pp * LANES:(pp + 1) * LANES]
        qh += [q2 * lo + mk_ref[2:3, :], q2 * hi + mk_ref[3:4, :]]

    m_ref[...] = jnp.full_like(m_ref, -jnp.inf)
    acc_ref[...] = jnp.zeros_like(acc_ref)

    def step(j, row0, masked):
        rows = tq - row0
        r0 = pl.multiple_of(j * tk, tk)
        if masked:
            keep = (lax.broadcasted_iota(jnp.int32, (rows, tk), 1) + j * tk
                    <= lax.broadcasted_iota(jnp.int32, (rows, tk), 0) + (i * tq + row0))
        for pp in range(npp):
            ps = slice(pp * LANES, (pp + 1) * LANES)
            ks = k_ref[pl.ds(r0, tk), ps]
            vs = v_ref[pl.ds(r0, tk), ps]
            cb = cb_ref[pl.ds(r0, tk), ps]
            kh = (ks * lo + cb * hi, ks * hi + cb * lo)
            vh = (vs * lo + one_a, vs * hi + one_b)
            for h in range(2):
                hh = 2 * pp + h
                s = lax.dot_general(qh[hh][row0:, :], kh[h], NT_DIMS,
                                    preferred_element_type=F32)
                if masked:
                    s = jnp.where(keep, s, -jnp.inf)
                m_old = m_ref[hh, row0:, :]
                m_new = jnp.maximum(m_old, jnp.max(s, axis=-1, keepdims=True))
                alpha = jnp.exp2(m_old - m_new)
                pe = jnp.exp2(s - jnp.concatenate([m_new] * (tk // LANES), axis=1))
                acc_ref[hh, row0:, :] = alpha * acc_ref[hh, row0:, :] + jnp.dot(
                    pe.astype(BF16), vh[h], preferred_element_type=F32)
                m_ref[hh, row0:, :] = m_new

    def body(jj, carry):
        for u in range(n_sub):
            step(jj * n_sub + u, 0, False)
        return carry

    lax.fori_loop(0, i, body, 0)
    for d in range(n_sub):
        step(i * n_sub + d, d * tk, True)

    first = lax.broadcasted_iota(jnp.int32, (tq, LANES), 1) < FOX_HD
    for pp in range(npp):
        acc_a, acc_b = acc_ref[2 * pp], acc_ref[2 * pp + 1]
        l_a = jnp.broadcast_to(acc_a[:, FOX_HD:FOX_HD + 1], (tq, LANES))
        l_b = jnp.broadcast_to(acc_b[:, 0:1], (tq, LANES))
        o_ref[:, pp * LANES:(pp + 1) * LANES] = jnp.where(
            first, acc_a / l_a, acc_b / l_b).astype(BF16)


def _fox(proj, cb, *, batch, seq, tq, tk, npp):
    nq = seq // tq
    ngrp = FOX_HEADS // 2 // npp
    w = npp * LANES
    qb, kb, vb = FQ_OFF // w, FK_OFF // w, FV_OFF // w
    return pl.pallas_call(
        functools.partial(_fox_kernel, tq=tq, tk=tk),
        grid=(batch, ngrp, nq),
        in_specs=[
            pl.BlockSpec((tq, w), lambda b, p, i: (b * nq + i, qb + p)),
            pl.BlockSpec((seq, w), lambda b, p, i: (b, kb + p)),
            pl.BlockSpec((seq, w), lambda b, p, i: (b, vb + p)),
            pl.BlockSpec((seq, w), lambda b, p, i: (b, p)),
            pl.BlockSpec((8, LANES), lambda b, p, i: (0, 0)),
        ],
        out_specs=pl.BlockSpec((tq, w), lambda b, p, i: (b * nq + i, p)),
        out_shape=jax.ShapeDtypeStruct((batch * seq, FOX_W), BF16),
        scratch_shapes=[pltpu.VMEM((2 * npp, tq, LANES), F32),
                        pltpu.VMEM((2 * npp, tq, LANES), F32)],
        compiler_params=_params("parallel", "parallel", "arbitrary"),
        name="fox_attention")(proj, proj, proj, cb, _fox_lane_rows())


def _mlstm_kernel(u_ref, v_ref, mo_ref, r_ref, ct_ref, sel_ref, cw_ref, cb_ref,
                  nw_ref, o_ref, ubuf, q_s, k_s, st_ref, m_ref, *, tt):
    s_idx = pl.program_id(1)
    halo = 8

    @pl.when(s_idx == 0)
    def _():
        ubuf[0:halo, :] = jnp.zeros((halo, 2 * ML_W), F32)
        st_ref[...] = jnp.zeros_like(st_ref)
        m_ref[...] = jnp.zeros_like(m_ref)

    @pl.when(s_idx > 0)
    def _():
        ubuf[0:halo, :] = ubuf[tt:tt + halo, :]

    c = ML_CHUNK
    for blk in range(tt // c):
        ubuf[halo + blk * c:halo + (blk + 1) * c, :] = (
            u_ref[blk * c:(blk + 1) * c, :].astype(F32))

    def conv_block(blk):
        acc = cb_ref[...]
        for j in (CONV_K - 1,) + tuple(range(CONV_K - 1)):
            off = halo - (CONV_K - 1) + j + blk * c
            acc = acc + cw_ref[j:j + 1, :] * ubuf[off:off + c, :]
        qk = acc * jax.nn.sigmoid(acc)
        q_s[blk * c:(blk + 1) * c, :] = qk[:, :ML_W].astype(BF16)
        k_s[blk * c:(blk + 1) * c, :] = (qk[:, ML_W:] * (ML_HD ** -0.5)).astype(BF16)

    row = lax.broadcasted_iota(jnp.int32, (c, c), 0)
    col = lax.broadcasted_iota(jnp.int32, (c, c), 1)
    causal = col <= row
    ones_blk = jnp.ones((c, ML_HD), BF16)

    def chunk(ci):
        r0 = ci * c
        cb = ct_ref[0, :, pl.ds(r0, c)]
        rep = []
        for qi in range(3):
            hi, mid, lo = _split3(r_ref[pl.ds(r0, c), qi * LANES:(qi + 1) * LANES])
            rep.append(jnp.dot(jnp.concatenate([hi, mid], axis=1), sel_ref[...],
                               preferred_element_type=F32)
                       + jnp.dot(lo, sel_ref[0:LANES, :], preferred_element_type=F32))
        for h in range(ML_HEADS):
            hs = slice(h * ML_HD, (h + 1) * ML_HD)
            qh = q_s[pl.ds(r0, c), hs]
            kh = k_s[pl.ds(r0, c), hs]
            vh = v_ref[pl.ds(r0, c), hs]
            f_t, b_t, cm_t = rep[0][:, hs], rep[1][:, hs], rep[2][:, hs]
            b_row = cb[GATE_MF + h:GATE_MF + h + 1, :]
            m_prev = m_ref[h:h + 1, :]

            g_t = jnp.maximum(m_prev, cm_t)
            dw = jnp.exp(jnp.where(causal, b_row - jnp.concatenate([g_t, g_t], axis=1),
                                   -jnp.inf))
            inter_w = jnp.exp(m_prev - g_t)
            sqk = lax.dot_general(qh, kh, NT_DIMS, preferred_element_type=F32) * dw
            state = st_ref[h]
            qc = jnp.dot(qh, state.astype(BF16), preferred_element_type=F32)
            v_aug = jnp.concatenate([vh, ones_blk], axis=1)
            pv = jnp.dot(sqk.astype(BF16), v_aug, preferred_element_type=F32)
            num = inter_w * qc[:, :ML_HD] + pv[:, :ML_HD]
            den = inter_w * qc[:, ML_HD:] + pv[:, ML_HD:]
            hh = num / jnp.maximum(jnp.abs(den), jnp.exp(-(f_t + g_t)))
            ms = jnp.mean(hh * hh, axis=-1, keepdims=True)
            y = (hh * lax.rsqrt(ms + EPS) * nw_ref[:, hs]
                 * jax.nn.sigmoid(mo_ref[pl.ds(r0, c), hs].astype(F32)))
            o_ref[pl.ds(r0, c), hs] = y.astype(BF16)

            f_last = f_t[c - 1:c, :]
            m_new = jnp.maximum(f_last + m_prev, f_last + cm_t[c - 1:c, :])
            s_old = jnp.exp(f_last + m_prev - m_new)
            kw = (kh.astype(F32) * jnp.exp(f_last + b_t - m_new)).astype(BF16)
            upd = lax.dot_general(kw, v_aug, TN_DIMS, preferred_element_type=F32)
            st_ref[h] = jnp.concatenate([s_old, s_old], axis=1) * state + upd
            m_ref[h:h + 1, :] = m_new

    for ci in range(tt // c):
        conv_block(ci)
        chunk(ci)


def _mlstm(proj, rows, ct, conv_w, conv_b, norm_w, *, batch, seq, tt):
    ns = seq // tt
    ub = MQK_OFF // (2 * ML_W)
    vb = MV_OFF // ML_W
    ob = MO_OFF // ML_W
    rmap = lambda b, s: (b * ns + s, 0)
    sel = np.zeros((2, LANES, ML_W), np.float32)
    for h in range(ML_HEADS):
        sel[:, GATE_MF + h, h * ML_HD:(h + 1) * ML_HD] = 1.0
    sel = jnp.asarray(sel.reshape(2 * LANES, ML_W), BF16)
    return pl.pallas_call(
        functools.partial(_mlstm_kernel, tt=tt),
        grid=(batch, ns),
        in_specs=[
            pl.BlockSpec((tt, 2 * ML_W), lambda b, s: (b * ns + s, ub)),
            pl.BlockSpec((tt, ML_W), lambda b, s: (b * ns + s, vb)),
            pl.BlockSpec((tt, ML_W), lambda b, s: (b * ns + s, ob)),
            pl.BlockSpec((tt, 3 * LANES), rmap),
            pl.BlockSpec((1, GATE_USED, tt), lambda b, s: (b, 0, s)),
            pl.BlockSpec((2 * LANES, ML_W), lambda b, s: (0, 0)),
            pl.BlockSpec((CONV_K, 2 * ML_W), lambda b, s: (0, 0)),
            pl.BlockSpec((1, 2 * ML_W), lambda b, s: (0, 0)),
            pl.BlockSpec((1, ML_W), lambda b, s: (0, 0)),
        ],
        out_specs=pl.BlockSpec((tt, ML_W), rmap),
        out_shape=jax.ShapeDtypeStruct((batch * seq, ML_W), BF16),
        scratch_shapes=[pltpu.VMEM((tt + 8, 2 * ML_W), F32),
                        pltpu.VMEM((tt, ML_W), BF16),
                        pltpu.VMEM((tt, ML_W), BF16),
                        pltpu.VMEM((ML_HEADS, ML_HD, 2 * ML_HD), F32),
                        pltpu.VMEM((8, LANES), F32)],
        compiler_params=_params("parallel", "arbitrary"),
        name="mlstm")(proj, proj, proj, rows, ct, sel, conv_w, conv_b, norm_w)


def _mix_xattn_kernel(yf_ref, ym_ref, gt_ref, bg_ref, h_ref, wf_ref, wm_ref, wmix_ref,
                      g_ref, wq_ref, k_ref, v_ref, wo_ref, o_ref):
    g = jax.nn.sigmoid(gt_ref[...].astype(F32) + bg_ref[...])
    uf = jnp.dot(yf_ref[...], wf_ref[...], preferred_element_type=F32)
    um = jnp.dot(ym_ref[...], wm_ref[...], preferred_element_type=F32)
    merged = g[:, :D_MODEL] * uf + g[:, D_MODEL:] * um
    x = h_ref[...] + jnp.dot(merged.astype(BF16), wmix_ref[...], preferred_element_type=F32)

    hn = _rms(x, g_ref[...]).astype(BF16)
    q = jnp.dot(hn, wq_ref[...], preferred_element_type=F32)
    q = (q * (X_HD ** -0.5)).astype(BF16)
    outs = []
    for hd in range(X_HEADS):
        hs = slice(hd * X_HD, (hd + 1) * X_HD)
        s = lax.dot_general(q[:, hs], k_ref[:, hs], NT_DIMS, preferred_element_type=F32)
        e = jnp.exp(s - jnp.max(s, axis=-1, keepdims=True))
        pr = e / jnp.sum(e, axis=-1, keepdims=True)
        outs.append(jnp.dot(pr.astype(BF16), v_ref[:, hs], preferred_element_type=F32))
    o = jnp.concatenate(outs, axis=1).astype(BF16)
    o_ref[...] = x + jnp.dot(o, wo_ref[...], preferred_element_type=F32)


def _mix_xattn(yf, ym, proj, b_gate, h, wf, wm, wmix, gain, wq, kv, wo, *,
               batch, seq, n_mem, tm):
    ns = seq // tm
    const = lambda b, s: (0, 0)
    row = lambda b, s: (b * ns + s, 0)
    return pl.pallas_call(
        _mix_xattn_kernel, grid=(batch, ns),
        in_specs=[
            pl.BlockSpec((tm, FOX_W), row),
            pl.BlockSpec((tm, ML_W), row),
            pl.BlockSpec((tm, 2 * D_MODEL), lambda b, s: (b * ns + s, G_OFF // (2 * D_MODEL))),
            pl.BlockSpec((1, 2 * D_MODEL), const),
            pl.BlockSpec((tm, D_MODEL), row),
            pl.BlockSpec((FOX_W, D_MODEL), const),
            pl.BlockSpec((ML_W, D_MODEL), const),
            pl.BlockSpec((D_MODEL, D_MODEL), const),
            pl.BlockSpec((1, D_MODEL), const),
            pl.BlockSpec((D_MODEL, D_MODEL), const),
            pl.BlockSpec((n_mem, D_MODEL), lambda b, s: (b, 0)),
            pl.BlockSpec((n_mem, D_MODEL), lambda b, s: (b, 1)),
            pl.BlockSpec((D_MODEL, D_MODEL), const),
        ],
        out_specs=pl.BlockSpec((tm, D_MODEL), row),
        out_shape=jax.ShapeDtypeStruct((batch * seq, D_MODEL), F32),
        compiler_params=_params("parallel", "parallel"),
        name="mix_cross_attention")(yf, ym, proj, b_gate, h, wf, wm, wmix,
                                    gain, wq, kv, kv, wo)


def _mlp_kernel(h_ref, g_ref, w1_ref, w2_ref, gf_ref, o_ref, *, tf, final):
    x = h_ref[...]
    hn = _rms(x, g_ref[...]).astype(BF16)
    acc = x
    for j in range(D_FF // tf):
        fs = slice(j * tf, (j + 1) * tf)
        a = jnp.maximum(jnp.dot(hn, w1_ref[:, fs], preferred_element_type=F32), 0.0)
        acc = acc + jnp.dot((a * a).astype(BF16), w2_ref[fs, :], preferred_element_type=F32)
    o_ref[...] = _rms(acc, gf_ref[...]) if final else acc


def _mlp(h, gain, w1, w2, gain_final, *, tm, tf, final):
    m = h.shape[0]
    const = lambda i: (0, 0)
    row = lambda i: (i, 0)
    return pl.pallas_call(
        functools.partial(_mlp_kernel, tf=tf, final=final),
        grid=(m // tm,),
        in_specs=[
            pl.BlockSpec((tm, D_MODEL), row),
            pl.BlockSpec((1, D_MODEL), const),
            pl.BlockSpec((D_MODEL, D_FF), const),
            pl.BlockSpec((D_FF, D_MODEL), const),
            pl.BlockSpec((1, D_MODEL), const),
        ],
        out_specs=pl.BlockSpec((tm, D_MODEL), row),
        out_shape=jax.ShapeDtypeStruct((m, D_MODEL), F32),
        compiler_params=_params("parallel"),
        name="mlp")(h, gain, w1, w2, gain_final)


def _pick(total, prefs):
    for t in prefs:
        if total % t == 0:
            return t
    raise ValueError(f"no tile in {prefs} divides {total}")


def kernel(x, mem, norm_mix, norm_xattn, norm_mem, norm_mlp, w_in, b_fox_f, b_mlstm_i,
           b_mlstm_f, b_gate, conv_w, conv_b, mlstm_norm, w_up_fox, w_up_mlstm, w_mix_out,
           w_xq, w_xkv, w_xo, w_mlp_in, w_mlp_out, norm_final):
    batch, seq, d = x.shape
    n_mem = mem.shape[1]
    depth = w_in.shape[0]
    assert d == D_MODEL and seq % ML_CHUNK == 0
    m = batch * seq

    tm_proj = _pick(m, (512, 256))
    tn_proj = 512
    t_seq = _pick(seq, (1024, 512, 256))
    tk = _pick(seq, (512, 256))
    tq = _pick(seq, (2 * tk, tk))
    tm_x = _pick(seq, (512, 256))
    tm_mlp = _pick(m, (512, 256))
    tf = 1024

    sp = [0, FOX_W, 2 * FOX_W, 3 * FOX_W, 3 * FOX_W + FOX_HEADS]
    sp += [sp[-1] + 2 * ML_W, sp[-1] + 3 * ML_W, sp[-1] + 4 * ML_W]
    sp += [sp[-1] + ML_HEADS, sp[-1] + 2 * ML_HEADS, sp[-1] + 2 * ML_HEADS + 2 * D_MODEL]
    w_in_b = w_in.astype(BF16)
    seg = [w_in_b[:, :, a:b] for a, b in zip(sp[:-1], sp[1:])]
    fq, fk, fv, ff, mqk, mv, mo, mi, mf, gp = seg
    fq = (w_in[:, :, sp[0]:sp[1]] * (FOX_HD ** -0.5 * LOG2E)).astype(BF16)
    w_main = jnp.concatenate([gp, mqk, fq, fk, fv, mv, mo], axis=2)
    w_gate = jnp.concatenate(
        [ff, mi, mf, jnp.zeros((depth, d, LANES - GATE_USED), BF16)], axis=2)
    gate_bias = jnp.concatenate(
        [b_fox_f, b_mlstm_i, b_mlstm_f, jnp.zeros((depth, LANES - GATE_USED), F32)], axis=1)
    bf = lambda w: w.astype(BF16)
    w_up_fox, w_up_mlstm, w_mix_out = bf(w_up_fox), bf(w_up_mlstm), bf(w_mix_out)
    w_xq, w_xkv, w_xo = bf(w_xq), bf(w_xkv), bf(w_xo)
    w_mlp_in, w_mlp_out = bf(w_mlp_in), bf(w_mlp_out)

    h = x.reshape(m, d)
    mem2 = mem.reshape(batch * n_mem, d)
    for l in range(depth):
        proj, gpre = _in_proj(h, norm_mix[l][None], w_main[l], w_gate[l],
                              tm=tm_proj, tn=tn_proj)
        rows, ct, cb = _gates(gpre, gate_bias[l][None], batch=batch, seq=seq, tg=t_seq)
        y_fox = _fox(proj, cb, batch=batch, seq=seq, tq=tq, tk=tk, npp=2)
        y_ml = _mlstm(proj, rows, ct, conv_w[l], conv_b[l][None], mlstm_norm[l][None],
                      batch=batch, seq=seq, tt=t_seq)
        kv = _norm_proj(mem2, norm_mem[l][None], w_xkv[l],
                        tm=_pick(batch * n_mem, (1024, 512, 256)), tn=1024)
        h = _mix_xattn(y_fox, y_ml, proj, b_gate[l][None], h,
                       w_up_fox[l], w_up_mlstm[l], w_mix_out[l],
                       norm_xattn[l][None], w_xq[l], kv, w_xo[l],
                       batch=batch, seq=seq, n_mem=n_mem, tm=tm_x)
        h = _mlp(h, norm_mlp[l][None], w_mlp_in[l], w_mlp_out[l], norm_final[None],
                 tm=tm_mlp, tf=tf, final=(l == depth - 1))
    return h.reshape(batch, seq, d)
```

```python
import functools

import jax
import jax.numpy as jnp
import numpy as np
from jax import lax
from jax.experimental import pallas as pl
from jax.experimental.pallas import tpu as pltpu

F32 = jnp.float32
BF16 = jnp.bfloat16

D_MODEL = 1024
FOX_HEADS = 8
FOX_HD = 64
FOX_W = FOX_HEADS * FOX_HD
ML_HEADS = 4
ML_HD = 128
ML_W = ML_HEADS * ML_HD
CONV_K = 4
X_HEADS = 4
X_HD = D_MODEL // X_HEADS
D_FF = 4 * D_MODEL
EPS = 1e-6
LOG2E = 1.4426950408889634

LANES = 128
V7X_VMEM_LIMIT_BYTES = 56 * 1024 * 1024

G_OFF = 0
MQK_OFF = G_OFF + 2 * D_MODEL
FQ_OFF = MQK_OFF + 2 * ML_W
FK_OFF = FQ_OFF + FOX_W
FV_OFF = FK_OFF + FOX_W
MV_OFF = FV_OFF + FOX_W
MO_OFF = MV_OFF + ML_W
PROJ_W = MO_OFF + ML_W
GATE_FF = 0
GATE_MI = GATE_FF + FOX_HEADS
GATE_MF = GATE_MI + ML_HEADS
GATE_USED = GATE_MF + ML_HEADS

ML_CHUNK = 256
NT_DIMS = (((1,), (1,)), ((), ()))
TN_DIMS = (((0,), (0,)), ((), ()))


def _params(*sem):
    return pltpu.CompilerParams(dimension_semantics=sem,
                                vmem_limit_bytes=V7X_VMEM_LIMIT_BYTES)


def _rms(x, g):
    return x * lax.rsqrt(jnp.mean(x * x, axis=-1, keepdims=True) + EPS) * g


def _in_proj_kernel(x_ref, g_ref, w_ref, wg_ref, o_ref, og_ref, *, tn):
    xn = _rms(x_ref[...], g_ref[...]).astype(BF16)
    og_ref[...] = jnp.dot(xn, wg_ref[...], preferred_element_type=F32)
    for j in range(w_ref.shape[1] // tn):
        cs = slice(j * tn, (j + 1) * tn)
        o_ref[:, cs] = jnp.dot(xn, w_ref[:, cs], preferred_element_type=F32).astype(BF16)


def _in_proj(x, gain, w, wg, *, tm, tn):
    m, d = x.shape
    n = w.shape[1]
    const = lambda i: (0, 0)
    return pl.pallas_call(
        functools.partial(_in_proj_kernel, tn=tn), grid=(m // tm,),
        in_specs=[pl.BlockSpec((tm, d), lambda i: (i, 0)),
                  pl.BlockSpec((1, d), const),
                  pl.BlockSpec((d, n), const),
                  pl.BlockSpec((d, LANES), const)],
        out_specs=[pl.BlockSpec((tm, n), lambda i: (i, 0)),
                   pl.BlockSpec((tm, LANES), lambda i: (i, 0))],
        out_shape=[jax.ShapeDtypeStruct((m, n), BF16),
                   jax.ShapeDtypeStruct((m, LANES), F32)],
        compiler_params=_params("parallel"),
        name="norm_in_proj")(x, gain, w, wg)


def _norm_proj_kernel(x_ref, g_ref, w_ref, o_ref, xn_ref):
    @pl.when(pl.program_id(1) == 0)
    def _():
        xn_ref[...] = _rms(x_ref[...], g_ref[...]).astype(BF16)

    o_ref[...] = jnp.dot(xn_ref[...], w_ref[...],
                         preferred_element_type=F32).astype(BF16)


def _norm_proj(x, gain, w, *, tm, tn):
    m, d = x.shape
    n = w.shape[1]
    return pl.pallas_call(
        _norm_proj_kernel, grid=(m // tm, n // tn),
        in_specs=[pl.BlockSpec((tm, d), lambda i, j: (i, 0)),
                  pl.BlockSpec((1, d), lambda i, j: (0, 0)),
                  pl.BlockSpec((d, tn), lambda i, j: (0, j))],
        out_specs=pl.BlockSpec((tm, tn), lambda i, j: (i, j)),
        out_shape=jax.ShapeDtypeStruct((m, n), BF16),
        scratch_shapes=[pltpu.VMEM((tm, d), BF16)],
        compiler_params=_params("parallel", "arbitrary"),
        name="norm_proj")(x, gain, w)


def _split3(v):
    hi = v.astype(BF16)
    r1 = v - hi.astype(F32)
    mid = r1.astype(BF16)
    lo = (r1 - mid.astype(F32)).astype(BF16)
    return hi, mid, lo


def _gates_kernel(x_ref, b_ref, pm_ref, r_ref, ct_ref, cb_ref, carry_ref, *, tg):
    @pl.when(pl.program_id(1) == 0)
    def _():
        carry_ref[...] = jnp.zeros_like(carry_ref)

    c = ML_CHUNK
    row = lax.broadcasted_iota(jnp.int32, (c, c), 0)
    col = lax.broadcasted_iota(jnp.int32, (c, c), 1)
    causal = row >= col
    tri = jnp.where(causal, 1.0, 0.0).astype(BF16)
    lane = lax.broadcasted_iota(jnp.int32, (c, LANES), 1)
    is_fox = lane < GATE_MI
    is_ig = (lane >= GATE_MI) & (lane < GATE_MF)
    is_used = lane < GATE_USED
    bias = b_ref[...]
    for blk in range(tg // c):
        rs = slice(blk * c, (blk + 1) * c)
        x = x_ref[rs, :] + bias
        logsig = jnp.minimum(x, 0.0) - jnp.log1p(jnp.exp(-jnp.abs(x)))
        val = jnp.where(is_ig, x, logsig)
        val = jnp.where(is_used, val, 0.0)
        csp = jnp.dot(tri, jnp.concatenate(_split3(val), axis=1), preferred_element_type=F32)
        cs = csp[:, 0:LANES] + csp[:, LANES:2 * LANES] + csp[:, 2 * LANES:3 * LANES]
        glob = cs + carry_ref[0:1, :]
        carry_ref[0:1, :] = glob[c - 1:c, :]
        b_t = pltpu.roll(val, GATE_MF - GATE_MI, axis=1) - cs
        bt = b_t.T
        ct_ref[0, :, rs] = bt[:GATE_USED, :]
        cm = jnp.zeros((c, LANES), F32)
        for h in range(ML_HEADS):
            k = GATE_MF + h
            run_max = jnp.max(jnp.where(causal, bt[k:k + 1, :], -jnp.inf),
                              axis=-1, keepdims=True)
            cm = jnp.where(lane == k, run_max, cm)
        r_ref[rs, 0:LANES] = cs
        r_ref[rs, LANES:2 * LANES] = b_t
        r_ref[rs, 2 * LANES:3 * LANES] = cm
        terms = _split3(jnp.where(is_fox, glob * (-LOG2E), 0.0))
        cb_ref[rs, :] = jnp.dot(jnp.concatenate(terms, axis=1), pm_ref[...],
                                preferred_element_type=F32).astype(BF16)


def _fox_bias_placement():
    pm = np.zeros((3, LANES, FOX_W), np.float32)
    for h in range(FOX_HEADS):
        base = (h // 2) * LANES + (FOX_HD if h % 2 == 0 else 0)
        for t in range(3):
            pm[t, h, base + t] = 1.0
    return jnp.asarray(pm.reshape(3 * LANES, FOX_W), BF16)


def _gates(gpre, bias, *, batch, seq, tg):
    ns = seq // tg
    rmap = lambda b, s: (b * ns + s, 0)
    return pl.pallas_call(
        functools.partial(_gates_kernel, tg=tg),
        grid=(batch, ns),
        in_specs=[pl.BlockSpec((tg, LANES), rmap),
                  pl.BlockSpec((1, LANES), lambda b, s: (0, 0)),
                  pl.BlockSpec((3 * LANES, FOX_W), lambda b, s: (0, 0))],
        out_specs=[pl.BlockSpec((tg, 3 * LANES), rmap),
                   pl.BlockSpec((1, GATE_USED, tg), lambda b, s: (b, 0, s)),
                   pl.BlockSpec((tg, FOX_W), rmap)],
        out_shape=[jax.ShapeDtypeStruct((batch * seq, 3 * LANES), F32),
                   jax.ShapeDtypeStruct((batch, GATE_USED, seq), F32),
                   jax.ShapeDtypeStruct((batch * seq, FOX_W), BF16)],
        scratch_shapes=[pltpu.VMEM((8, LANES), F32)],
        compiler_params=_params("parallel", "arbitrary"),
        name="gates")(gpre, bias, _fox_bias_placement())


def _fox_lane_rows():
    r = np.zeros((8, LANES), np.float32)
    r[0, :FOX_HD] = 1.0
    r[1, FOX_HD:] = 1.0
    r[2, FOX_HD:FOX_HD + 3] = 1.0
    r[3, 0:3] = 1.0
    r[4, FOX_HD] = 1.0
    r[5, 0] = 1.0
    return jnp.asarray(r, BF16)


def _fox_kernel(q_ref, k_ref, v_ref, cb_ref, mk_ref, o_ref, m_ref, acc_ref, *, tq, tk):
    i = pl.program_id(2)
    n_sub = tq // tk
    npp = q_ref.shape[1] // LANES
    lo, hi = mk_ref[0:1, :], mk_ref[1:2, :]
    one_a, one_b = mk_ref[4:5, :], mk_ref[5:6, :]
    qh = []
    for pp in range(npp):
        q2 = q_ref[:, pp * LANES:(pp + 1) * LANES]
        qh += [q2 * lo + mk_ref[2:3, :], q2 * hi + mk_ref[3:4, :]]

    m_ref[...] = jnp.full_like(m_ref, -jnp.inf)
    acc_ref[...] = jnp.zeros_like(acc_ref)

    def step(j, row0, masked):
        rows = tq - row0
        r0 = pl.multiple_of(j * tk, tk)
        if masked:
            keep = (lax.broadcasted_iota(jnp.int32, (rows, tk), 1) + j * tk
                    <= lax.broadcasted_iota(jnp.int32, (rows, tk), 0) + (i * tq + row0))
        for pp in range(npp):
            ps = slice(pp * LANES, (pp + 1) * LANES)
            ks = k_ref[pl.ds(r0, tk), ps]
            vs = v_ref[pl.ds(r0, tk), ps]
            cb = cb_ref[pl.ds(r0, tk), ps]
            kh = (ks * lo + cb * hi, ks * hi + cb * lo)
            vh = (vs * lo + one_a, vs * hi + one_b)
            for h in range(2):
                hh = 2 * pp + h
                s = lax.dot_general(qh[hh][row0:, :], kh[h], NT_DIMS,
                                    preferred_element_type=F32)
                if masked:
                    s = jnp.where(keep, s, -jnp.inf)
                m_old = m_ref[hh, row0:, :]
                m_new = jnp.maximum(m_old, jnp.max(s, axis=-1, keepdims=True))
                alpha = jnp.exp2(m_old - m_new)
                pe = jnp.exp2(s - jnp.concatenate([m_new] * (tk // LANES), axis=1))
                acc_ref[hh, row0:, :] = alpha * acc_ref[hh, row0:, :] + jnp.dot(
                    pe.astype(BF16), vh[h], preferred_element_type=F32)
                m_ref[hh, row0:, :] = m_new

    def body(jj, carry):
        for u in range(n_sub):
            step(jj * n_sub + u, 0, False)
        return carry

    lax.fori_loop(0, i, body, 0)
    for d in range(n_sub):
        step(i * n_sub + d, d * tk, True)

    first = lax.broadcasted_iota(jnp.int32, (tq, LANES), 1) < FOX_HD
    for pp in range(npp):
        acc_a, acc_b = acc_ref[2 * pp], acc_ref[2 * pp + 1]
        l_a = jnp.broadcast_to(acc_a[:, FOX_HD:FOX_HD + 1], (tq, LANES))
        l_b = jnp.broadcast_to(acc_b[:, 0:1], (tq, LANES))
        o_ref[:, pp * LANES:(pp + 1) * LANES] = (
            jnp.where(first, acc_a, acc_b) / jnp.where(first, l_a, l_b)).astype(BF16)


def _fox(proj, cb, *, batch, seq, tq, tk, npp):
    nq = seq // tq
    ngrp = FOX_HEADS // 2 // npp
    w = npp * LANES
    qb, kb, vb = FQ_OFF // w, FK_OFF // w, FV_OFF // w
    return pl.pallas_call(
        functools.partial(_fox_kernel, tq=tq, tk=tk),
        grid=(batch, ngrp, nq),
        in_specs=[
            pl.BlockSpec((tq, w), lambda b, p, i: (b * nq + i, qb + p)),
            pl.BlockSpec((seq, w), lambda b, p, i: (b, kb + p)),
            pl.BlockSpec((seq, w), lambda b, p, i: (b, vb + p)),
            pl.BlockSpec((seq, w), lambda b, p, i: (b, p)),
            pl.BlockSpec((8, LANES), lambda b, p, i: (0, 0)),
        ],
        out_specs=pl.BlockSpec((tq, w), lambda b, p, i: (b * nq + i, p)),
        out_shape=jax.ShapeDtypeStruct((batch * seq, FOX_W), BF16),
        scratch_shapes=[pltpu.VMEM((2 * npp, tq, LANES), F32),
                        pltpu.VMEM((2 * npp, tq, LANES), F32)],
        compiler_params=_params("parallel", "parallel", "arbitrary"),
        name="fox_attention")(proj, proj, proj, cb, _fox_lane_rows())


def _mlstm_kernel(u_ref, v_ref, mo_ref, r_ref, ct_ref, sel_ref, cw_ref, cb_ref,
                  nw_ref, o_ref, ubuf, q_s, k_s, st_ref, m_ref, *, tt):
    s_idx = pl.program_id(1)
    halo = 8

    @pl.when(s_idx == 0)
    def _():
        ubuf[0:halo, :] = jnp.zeros((halo, 2 * ML_W), F32)
        st_ref[...] = jnp.zeros_like(st_ref)
        m_ref[...] = jnp.zeros_like(m_ref)

    @pl.when(s_idx > 0)
    def _():
        ubuf[0:halo, :] = ubuf[tt:tt + halo, :]

    c = ML_CHUNK
    for blk in range(tt // c):
        ubuf[halo + blk * c:halo + (blk + 1) * c, :] = (
            u_ref[blk * c:(blk + 1) * c, :].astype(F32))

    def conv_block(blk):
        acc = cb_ref[...]
        for j in (CONV_K - 1,) + tuple(range(CONV_K - 1)):
            off = halo - (CONV_K - 1) + j + blk * c
            acc = acc + cw_ref[j:j + 1, :] * ubuf[off:off + c, :]
        qk = acc * jax.nn.sigmoid(acc)
        q_s[blk * c:(blk + 1) * c, :] = qk[:, :ML_W].astype(BF16)
        k_s[blk * c:(blk + 1) * c, :] = (qk[:, ML_W:] * (ML_HD ** -0.5)).astype(BF16)

    row = lax.broadcasted_iota(jnp.int32, (c, c), 0)
    col = lax.broadcasted_iota(jnp.int32, (c, c), 1)
    causal = col <= row
    ones_blk = jnp.ones((c, ML_HD), BF16)

    def chunk(ci):
        r0 = ci * c
        cb = ct_ref[0, :, pl.ds(r0, c)]
        rep = []
        for qi in range(3):
            hi, mid, lo = _split3(r_ref[pl.ds(r0, c), qi * LANES:(qi + 1) * LANES])
            rep.append(jnp.dot(jnp.concatenate([hi, mid], axis=1), sel_ref[...],
                               preferred_element_type=F32)
                       + jnp.dot(lo, sel_ref[0:LANES, :], preferred_element_type=F32))
        for h in range(ML_HEADS):
            hs = slice(h * ML_HD, (h + 1) * ML_HD)
            qh = q_s[pl.ds(r0, c), hs]
            kh = k_s[pl.ds(r0, c), hs]
            vh = v_ref[pl.ds(r0, c), hs]
            f_t, b_t, cm_t = rep[0][:, hs], rep[1][:, hs], rep[2][:, hs]
            b_row = cb[GATE_MF + h:GATE_MF + h + 1, :]
            m_prev = m_ref[h:h + 1, :]

            g_t = jnp.maximum(m_prev, cm_t)
            dw = jnp.exp(jnp.where(causal, b_row - jnp.concatenate([g_t, g_t], axis=1),
                                   -jnp.inf))
            inter_w = jnp.exp(m_prev - g_t)
            sqk = lax.dot_general(qh, kh, NT_DIMS, preferred_element_type=F32) * dw
            state = st_ref[h]
            qc = jnp.dot(qh, state.astype(BF16), preferred_element_type=F32)
            v_aug = jnp.concatenate([vh, ones_blk], axis=1)
            pv = jnp.dot(sqk.astype(BF16), v_aug, preferred_element_type=F32)
            num = inter_w * qc[:, :ML_HD] + pv[:, :ML_HD]
            den = inter_w * qc[:, ML_HD:] + pv[:, ML_HD:]
            hh = num / jnp.maximum(jnp.abs(den), jnp.exp(-(f_t + g_t)))
            ms = jnp.mean(hh * hh, axis=-1, keepdims=True)
            y = (hh * lax.rsqrt(ms + EPS) * nw_ref[:, hs]
                 * jax.nn.sigmoid(mo_ref[pl.ds(r0, c), hs].astype(F32)))
            o_ref[pl.ds(r0, c), hs] = y.astype(BF16)

            f_last = f_t[c - 1:c, :]
            m_new = jnp.maximum(f_last + m_prev, f_last + cm_t[c - 1:c, :])
            s_old = jnp.exp(f_last + m_prev - m_new)
            kw = (kh.astype(F32) * jnp.exp(f_last + b_t - m_new)).astype(BF16)
            upd = lax.dot_general(kw, v_aug, TN_DIMS, preferred_element_type=F32)
            st_ref[h] = jnp.concatenate([s_old, s_old], axis=1) * state + upd
            m_ref[h:h + 1, :] = m_new

    for ci in range(tt // c):
        conv_block(ci)
        chunk(ci)


def _mlstm(proj, rows, ct, conv_w, conv_b, norm_w, *, batch, seq, tt):
    ns = seq // tt
    ub = MQK_OFF // (2 * ML_W)
    vb = MV_OFF // ML_W
    ob = MO_OFF // ML_W
    rmap = lambda b, s: (b * ns + s, 0)
    sel = np.zeros((2, LANES, ML_W), np.float32)
    for h in range(ML_HEADS):
        sel[:, GATE_MF + h, h * ML_HD:(h + 1) * ML_HD] = 1.0
    sel = jnp.asarray(sel.reshape(2 * LANES, ML_W), BF16)
    return pl.pallas_call(
        functools.partial(_mlstm_kernel, tt=tt),
        grid=(batch, ns),
        in_specs=[
            pl.BlockSpec((tt, 2 * ML_W), lambda b, s: (b * ns + s, ub)),
            pl.BlockSpec((tt, ML_W), lambda b, s: (b * ns + s, vb)),
            pl.BlockSpec((tt, ML_W), lambda b, s: (b * ns + s, ob)),
            pl.BlockSpec((tt, 3 * LANES), rmap),
            pl.BlockSpec((1, GATE_USED, tt), lambda b, s: (b, 0, s)),
            pl.BlockSpec((2 * LANES, ML_W), lambda b, s: (0, 0)),
            pl.BlockSpec((CONV_K, 2 * ML_W), lambda b, s: (0, 0)),
            pl.BlockSpec((1, 2 * ML_W), lambda b, s: (0, 0)),
            pl.BlockSpec((1, ML_W), lambda b, s: (0, 0)),
        ],
        out_specs=pl.BlockSpec((tt, ML_W), rmap),
        out_shape=jax.ShapeDtypeStruct((batch * seq, ML_W), BF16),
        scratch_shapes=[pltpu.VMEM((tt + 8, 2 * ML_W), F32),
                        pltpu.VMEM((tt, ML_W), BF16),
                        pltpu.VMEM((tt, ML_W), BF16),
                        pltpu.VMEM((ML_HEADS, ML_HD, 2 * ML_HD), F32),
                        pltpu.VMEM((8, LANES), F32)],
        compiler_params=_params("parallel", "arbitrary"),
        name="mlstm")(proj, proj, proj, rows, ct, sel, conv_w, conv_b, norm_w)


def _mix_xattn_kernel(yf_ref, ym_ref, gt_ref, bg_ref, h_ref, wf_ref, wm_ref, wmix_ref,
                      g_ref, wq_ref, k_ref, v_ref, wo_ref, o_ref):
    g = jax.nn.sigmoid(gt_ref[...].astype(F32) + bg_ref[...])
    uf = jnp.dot(yf_ref[...], wf_ref[...], preferred_element_type=F32)
    um = jnp.dot(ym_ref[...], wm_ref[...], preferred_element_type=F32)
    merged = g[:, :D_MODEL] * uf + g[:, D_MODEL:] * um
    x = h_ref[...] + jnp.dot(merged.astype(BF16), wmix_ref[...], preferred_element_type=F32)

    hn = _rms(x, g_ref[...]).astype(BF16)
    q = jnp.dot(hn, wq_ref[...], preferred_element_type=F32)
    q = (q * (X_HD ** -0.5)).astype(BF16)
    outs = []
    for hd in range(X_HEADS):
        hs = slice(hd * X_HD, (hd + 1) * X_HD)
        s = lax.dot_general(q[:, hs], k_ref[:, hs], NT_DIMS, preferred_element_type=F32)
        e = jnp.exp(s - jnp.max(s, axis=-1, keepdims=True))
        pr = e / jnp.sum(e, axis=-1, keepdims=True)
        outs.append(jnp.dot(pr.astype(BF16), v_ref[:, hs], preferred_element_type=F32))
    o = jnp.concatenate(outs, axis=1).astype(BF16)
    o_ref[...] = x + jnp.dot(o, wo_ref[...], preferred_element_type=F32)


def _mix_xattn(yf, ym, proj, b_gate, h, wf, wm, wmix, gain, wq, kv, wo, *,
               batch, seq, n_mem, tm):
    ns = seq // tm
    const = lambda b, s: (0, 0)
    row = lambda b, s: (b * ns + s, 0)
    return pl.pallas_call(
        _mix_xattn_kernel, grid=(batch, ns),
        in_specs=[
            pl.BlockSpec((tm, FOX_W), row),
            pl.BlockSpec((tm, ML_W), row),
            pl.BlockSpec((tm, 2 * D_MODEL), lambda b, s: (b * ns + s, G_OFF // (2 * D_MODEL))),
            pl.BlockSpec((1, 2 * D_MODEL), const),
            pl.BlockSpec((tm, D_MODEL), row),
            pl.BlockSpec((FOX_W, D_MODEL), const),
            pl.BlockSpec((ML_W, D_MODEL), const),
            pl.BlockSpec((D_MODEL, D_MODEL), const),
            pl.BlockSpec((1, D_MODEL), const),
            pl.BlockSpec((D_MODEL, D_MODEL), const),
            pl.BlockSpec((n_mem, D_MODEL), lambda b, s: (b, 0)),
            pl.BlockSpec((n_mem, D_MODEL), lambda b, s: (b, 1)),
            pl.BlockSpec((D_MODEL, D_MODEL), const),
        ],
        out_specs=pl.BlockSpec((tm, D_MODEL), row),
        out_shape=jax.ShapeDtypeStruct((batch * seq, D_MODEL), F32),
        compiler_params=_params("parallel", "parallel"),
        name="mix_cross_attention")(yf, ym, proj, b_gate, h, wf, wm, wmix,
                                    gain, wq, kv, kv, wo)


def _mlp_kernel(h_ref, g_ref, w1_ref, w2_ref, gf_ref, o_ref, *, tf, final):
    x = h_ref[...]
    hn = _rms(x, g_ref[...]).astype(BF16)
    acc = x
    for j in range(D_FF // tf):
        fs = slice(j * tf, (j + 1) * tf)
        a = jnp.maximum(jnp.dot(hn, w1_ref[:, fs], preferred_element_type=F32), 0.0)
        acc = acc + jnp.dot((a * a).astype(BF16), w2_ref[fs, :], preferred_element_type=F32)
    o_ref[...] = _rms(acc, gf_ref[...]) if final else acc


def _mlp(h, gain, w1, w2, gain_final, *, tm, tf, final):
    m = h.shape[0]
    const = lambda i: (0, 0)
    row = lambda i: (i, 0)
    return pl.pallas_call(
        functools.partial(_mlp_kernel, tf=tf, final=final),
        grid=(m // tm,),
        in_specs=[
            pl.BlockSpec((tm, D_MODEL), row),
            pl.BlockSpec((1, D_MODEL), const),
            pl.BlockSpec((D_MODEL, D_FF), const),
            pl.BlockSpec((D_FF, D_MODEL), const),
            pl.BlockSpec((1, D_MODEL), const),
        ],
        out_specs=pl.BlockSpec((tm, D_MODEL), row),
        out_shape=jax.ShapeDtypeStruct((m, D_MODEL), F32),
        compiler_params=_params("parallel"),
        name="mlp")(h, gain, w1, w2, gain_final)


def _pick(total, prefs):
    for t in prefs:
        if total % t == 0:
            return t
    raise ValueError(f"no tile in {prefs} divides {total}")


def kernel(x, mem, norm_mix, norm_xattn, norm_mem, norm_mlp, w_in, b_fox_f, b_mlstm_i,
           b_mlstm_f, b_gate, conv_w, conv_b, mlstm_norm, w_up_fox, w_up_mlstm, w_mix_out,
           w_xq, w_xkv, w_xo, w_mlp_in, w_mlp_out, norm_final):
    batch, seq, d = x.shape
    n_mem = mem.shape[1]
    depth = w_in.shape[0]
    assert d == D_MODEL and seq % ML_CHUNK == 0
    m = batch * seq

    tm_proj = _pick(m, (512, 256))
    tn_proj = 512
    t_seq = _pick(seq, (1024, 512, 256))
    tk = _pick(seq, (512, 256))
    tq = _pick(seq, (4 * tk, 2 * tk, tk))
    tm_x = _pick(seq, (512, 256))
    tm_mlp = _pick(m, (512, 256))
    tf = 1024

    sp = [0, FOX_W, 2 * FOX_W, 3 * FOX_W, 3 * FOX_W + FOX_HEADS]
    sp += [sp[-1] + 2 * ML_W, sp[-1] + 3 * ML_W, sp[-1] + 4 * ML_W]
    sp += [sp[-1] + ML_HEADS, sp[-1] + 2 * ML_HEADS, sp[-1] + 2 * ML_HEADS + 2 * D_MODEL]
    w_in_b = w_in.astype(BF16)
    seg = [w_in_b[:, :, a:b] for a, b in zip(sp[:-1], sp[1:])]
    fq, fk, fv, ff, mqk, mv, mo, mi, mf, gp = seg
    fq = (w_in[:, :, sp[0]:sp[1]] * (FOX_HD ** -0.5 * LOG2E)).astype(BF16)
    w_main = jnp.concatenate([gp, mqk, fq, fk, fv, mv, mo], axis=2)
    w_gate = jnp.concatenate(
        [ff, mi, mf, jnp.zeros((depth, d, LANES - GATE_USED), BF16)], axis=2)
    gate_bias = jnp.concatenate(
        [b_fox_f, b_mlstm_i, b_mlstm_f, jnp.zeros((depth, LANES - GATE_USED), F32)], axis=1)
    bf = lambda w: w.astype(BF16)
    w_up_fox, w_up_mlstm, w_mix_out = bf(w_up_fox), bf(w_up_mlstm), bf(w_mix_out)
    w_xq, w_xkv, w_xo = bf(w_xq), bf(w_xkv), bf(w_xo)
    w_mlp_in, w_mlp_out = bf(w_mlp_in), bf(w_mlp_out)

    h = x.reshape(m, d)
    mem2 = mem.reshape(batch * n_mem, d)
    for l in range(depth):
        proj, gpre = _in_proj(h, norm_mix[l][None], w_main[l], w_gate[l],
                              tm=tm_proj, tn=tn_proj)
        rows, ct, cb = _gates(gpre, gate_bias[l][None], batch=batch, seq=seq, tg=t_seq)
        y_fox = _fox(proj, cb, batch=batch, seq=seq, tq=tq, tk=tk, npp=1)
        y_ml = _mlstm(proj, rows, ct, conv_w[l], conv_b[l][None], mlstm_norm[l][None],
                      batch=batch, seq=seq, tt=t_seq)
        kv = _norm_proj(mem2, norm_mem[l][None], w_xkv[l],
                        tm=_pick(batch * n_mem, (1024, 512, 256)), tn=1024)
        h = _mix_xattn(y_fox, y_ml, proj, b_gate[l][None], h,
                       w_up_fox[l], w_up_mlstm[l], w_mix_out[l],
                       norm_xattn[l][None], w_xq[l], kv, w_xo[l],
                       batch=batch, seq=seq, n_mem=n_mem, tm=tm_x)
        h = _mlp(h, norm_mlp[l][None], w_mlp_in[l], w_mlp_out[l], norm_final[None],
                 tm=tm_mlp, tf=tf, final=(l == depth - 1))
    return h.reshape(batch, seq, d)
```

```python
import functools

import jax
import jax.numpy as jnp
import numpy as np
from jax import lax
from jax.experimental import pallas as pl
from jax.experimental.pallas import tpu as pltpu

F32 = jnp.float32
BF16 = jnp.bfloat16

D_MODEL = 1024
FOX_HEADS = 8
FOX_HD = 64
FOX_W = FOX_HEADS * FOX_HD
ML_HEADS = 4
ML_HD = 128
ML_W = ML_HEADS * ML_HD
CONV_K = 4
X_HEADS = 4
X_HD = D_MODEL // X_HEADS
D_FF = 4 * D_MODEL
EPS = 1e-6
LOG2E = 1.4426950408889634

LANES = 128
V7X_VMEM_LIMIT_BYTES = 56 * 1024 * 1024

G_OFF = 0
MQK_OFF = G_OFF + 2 * D_MODEL
FQ_OFF = MQK_OFF + 2 * ML_W
FK_OFF = FQ_OFF + FOX_W
FV_OFF = FK_OFF + FOX_W
MV_OFF = FV_OFF + FOX_W
MO_OFF = MV_OFF + ML_W
PROJ_W = MO_OFF + ML_W
GATE_FF = 0
GATE_MI = GATE_FF + FOX_HEADS
GATE_MF = GATE_MI + ML_HEADS
GATE_USED = GATE_MF + ML_HEADS

ML_CHUNK = 256
NT_DIMS = (((1,), (1,)), ((), ()))
TN_DIMS = (((0,), (0,)), ((), ()))


def _params(*sem):
    return pltpu.CompilerParams(dimension_semantics=sem,
                                vmem_limit_bytes=V7X_VMEM_LIMIT_BYTES)


def _rms(x, g):
    return x * lax.rsqrt(jnp.mean(x * x, axis=-1, keepdims=True) + EPS) * g


def _layer_block(shape, layer):
    zeros = (0,) * len(shape)
    return pl.BlockSpec((None,) + tuple(shape), lambda *_: (layer,) + zeros)


def _in_proj_kernel(x_ref, g_ref, w_ref, wg_ref, o_ref, og_ref, *, tn):
    xn = _rms(x_ref[...], g_ref[...]).astype(BF16)
    og_ref[...] = jnp.dot(xn, wg_ref[...], preferred_element_type=F32)
    for j in range(w_ref.shape[1] // tn):
        cs = slice(j * tn, (j + 1) * tn)
        o_ref[:, cs] = jnp.dot(xn, w_ref[:, cs], preferred_element_type=F32).astype(BF16)


def _in_proj(x, gain, w, wg, *, layer, tm, tn):
    m, d = x.shape
    n = w.shape[-1]
    const = lambda i: (0, 0)
    return pl.pallas_call(
        functools.partial(_in_proj_kernel, tn=tn), grid=(m // tm,),
        in_specs=[pl.BlockSpec((tm, d), lambda i: (i, 0)),
                  pl.BlockSpec((1, d), const),
                  _layer_block((d, n), layer),
                  _layer_block((d, LANES), layer)],
        out_specs=[pl.BlockSpec((tm, n), lambda i: (i, 0)),
                   pl.BlockSpec((tm, LANES), lambda i: (i, 0))],
        out_shape=[jax.ShapeDtypeStruct((m, n), BF16),
                   jax.ShapeDtypeStruct((m, LANES), F32)],
        compiler_params=_params("parallel"),
        name="norm_in_proj")(x, gain, w, wg)


def _norm_proj_kernel(x_ref, g_ref, w_ref, o_ref, xn_ref):
    @pl.when(pl.program_id(1) == 0)
    def _():
        xn_ref[...] = _rms(x_ref[...], g_ref[...]).astype(BF16)

    o_ref[...] = jnp.dot(xn_ref[...], w_ref[...],
                         preferred_element_type=F32).astype(BF16)


def _norm_proj(x, gain, w, *, layer, tm, tn):
    m, d = x.shape
    n = w.shape[-1]
    return pl.pallas_call(
        _norm_proj_kernel, grid=(m // tm, n // tn),
        in_specs=[pl.BlockSpec((tm, d), lambda i, j: (i, 0)),
                  pl.BlockSpec((1, d), lambda i, j: (0, 0)),
                  pl.BlockSpec((None, d, tn), lambda i, j: (layer, 0, j))],
        out_specs=pl.BlockSpec((tm, tn), lambda i, j: (i, j)),
        out_shape=jax.ShapeDtypeStruct((m, n), BF16),
        scratch_shapes=[pltpu.VMEM((tm, d), BF16)],
        compiler_params=_params("parallel", "arbitrary"),
        name="norm_proj")(x, gain, w)


def _split3(v):
    hi = v.astype(BF16)
    r1 = v - hi.astype(F32)
    mid = r1.astype(BF16)
    lo = (r1 - mid.astype(F32)).astype(BF16)
    return hi, mid, lo


def _gates_kernel(x_ref, b_ref, pm_ref, r_ref, ct_ref, cb_ref, carry_ref, *, tg):
    @pl.when(pl.program_id(1) == 0)
    def _():
        carry_ref[...] = jnp.zeros_like(carry_ref)

    c = ML_CHUNK
    row = lax.broadcasted_iota(jnp.int32, (c, c), 0)
    col = lax.broadcasted_iota(jnp.int32, (c, c), 1)
    causal = row >= col
    tri = jnp.where(causal, 1.0, 0.0).astype(BF16)
    lane = lax.broadcasted_iota(jnp.int32, (c, LANES), 1)
    is_fox = lane < GATE_MI
    is_ig = (lane >= GATE_MI) & (lane < GATE_MF)
    is_used = lane < GATE_USED
    bias = b_ref[...]
    for blk in range(tg // c):
        rs = slice(blk * c, (blk + 1) * c)
        x = x_ref[rs, :] + bias
        logsig = jnp.minimum(x, 0.0) - jnp.log1p(jnp.exp(-jnp.abs(x)))
        val = jnp.where(is_ig, x, logsig)
        val = jnp.where(is_used, val, 0.0)
        csp = jnp.dot(tri, jnp.concatenate(_split3(val), axis=1), preferred_element_type=F32)
        cs = csp[:, 0:LANES] + csp[:, LANES:2 * LANES] + csp[:, 2 * LANES:3 * LANES]
        glob = cs + carry_ref[0:1, :]
        carry_ref[0:1, :] = glob[c - 1:c, :]
        b_t = pltpu.roll(val, GATE_MF - GATE_MI, axis=1) - cs
        bt = b_t.T
        ct_ref[0, :, rs] = bt[:GATE_USED, :]
        cm = jnp.zeros((c, LANES), F32)
        for h in range(ML_HEADS):
            k = GATE_MF + h
            run_max = jnp.max(jnp.where(causal, bt[k:k + 1, :], -jnp.inf),
                              axis=-1, keepdims=True)
            cm = jnp.where(lane == k, run_max, cm)
        r_ref[rs, 0:LANES] = cs
        r_ref[rs, LANES:2 * LANES] = b_t
        r_ref[rs, 2 * LANES:3 * LANES] = cm
        terms = _split3(jnp.where(is_fox, glob * (-LOG2E), 0.0))
        cb_ref[rs, :] = jnp.dot(jnp.concatenate(terms, axis=1), pm_ref[...],
                                preferred_element_type=F32).astype(BF16)


def _fox_bias_placement():
    pm = np.zeros((3, LANES, FOX_W), np.float32)
    for h in range(FOX_HEADS):
        base = (h // 2) * LANES + (FOX_HD if h % 2 == 0 else 0)
        for t in range(3):
            pm[t, h, base + t] = 1.0
    return jnp.asarray(pm.reshape(3 * LANES, FOX_W), BF16)


def _gates(gpre, bias, *, batch, seq, tg):
    ns = seq // tg
    rmap = lambda b, s: (b * ns + s, 0)
    return pl.pallas_call(
        functools.partial(_gates_kernel, tg=tg),
        grid=(batch, ns),
        in_specs=[pl.BlockSpec((tg, LANES), rmap),
                  pl.BlockSpec((1, LANES), lambda b, s: (0, 0)),
                  pl.BlockSpec((3 * LANES, FOX_W), lambda b, s: (0, 0))],
        out_specs=[pl.BlockSpec((tg, 3 * LANES), rmap),
                   pl.BlockSpec((1, GATE_USED, tg), lambda b, s: (b, 0, s)),
                   pl.BlockSpec((tg, FOX_W), rmap)],
        out_shape=[jax.ShapeDtypeStruct((batch * seq, 3 * LANES), F32),
                   jax.ShapeDtypeStruct((batch, GATE_USED, seq), F32),
                   jax.ShapeDtypeStruct((batch * seq, FOX_W), BF16)],
        scratch_shapes=[pltpu.VMEM((8, LANES), F32)],
        compiler_params=_params("parallel", "arbitrary"),
        name="gates")(gpre, bias, _fox_bias_placement())


def _fox_lane_rows():
    r = np.zeros((8, LANES), np.float32)
    r[0, :FOX_HD] = 1.0
    r[1, FOX_HD:] = 1.0
    r[2, FOX_HD:FOX_HD + 3] = 1.0
    r[3, 0:3] = 1.0
    r[4, FOX_HD] = 1.0
    r[5, 0] = 1.0
    return jnp.asarray(r, BF16)


def _fox_kernel(q_ref, k_ref, v_ref, cb_ref, mk_ref, o_ref, m_ref, acc_ref, *, tq, tk):
    i = pl.program_id(2)
    n_sub = tq // tk
    npp = q_ref.shape[1] // LANES
    lo, hi = mk_ref[0:1, :], mk_ref[1:2, :]
    one_a, one_b = mk_ref[4:5, :], mk_ref[5:6, :]
    qh = []
    for pp in range(npp):
        q2 = q_ref[:, pp * LANES:(pp + 1) * LANES]
        qh += [q2 * lo + mk_ref[2:3, :], q2 * hi + mk_ref[3:4, :]]

    m_ref[...] = jnp.full_like(m_ref, -jnp.inf)
    acc_ref[...] = jnp.zeros_like(acc_ref)

    def step(j, row0, masked):
        rows = tq - row0
        r0 = pl.multiple_of(j * tk, tk)
        if masked:
            keep = (lax.broadcasted_iota(jnp.int32, (rows, tk), 1) + j * tk
                    <= lax.broadcasted_iota(jnp.int32, (rows, tk), 0) + (i * tq + row0))
        for pp in range(npp):
            ps = slice(pp * LANES, (pp + 1) * LANES)
            ks = k_ref[pl.ds(r0, tk), ps]
            vs = v_ref[pl.ds(r0, tk), ps]
            cb = cb_ref[pl.ds(r0, tk), ps]
            kh = (ks * lo + cb * hi, ks * hi + cb * lo)
            vh = (vs * lo + one_a, vs * hi + one_b)
            for h in range(2):
                hh = 2 * pp + h
                s = lax.dot_general(qh[hh][row0:, :], kh[h], NT_DIMS,
                                    preferred_element_type=F32)
                if masked:
                    s = jnp.where(keep, s, -jnp.inf)
                m_old = m_ref[hh, row0:, :]
                m_new = jnp.maximum(m_old, jnp.max(s, axis=-1, keepdims=True))
                alpha = jnp.exp2(m_old - m_new)
                pe = jnp.exp2(s - jnp.concatenate([m_new] * (tk // LANES), axis=1))
                acc_ref[hh, row0:, :] = alpha * acc_ref[hh, row0:, :] + jnp.dot(
                    pe.astype(BF16), vh[h], preferred_element_type=F32)
                m_ref[hh, row0:, :] = m_new

    def body(jj, carry):
        for u in range(n_sub):
            step(jj * n_sub + u, 0, False)
        return carry

    lax.fori_loop(0, i, body, 0)
    for d in range(n_sub):
        step(i * n_sub + d, d * tk, True)

    first = lax.broadcasted_iota(jnp.int32, (tq, LANES), 1) < FOX_HD
    for pp in range(npp):
        acc_a, acc_b = acc_ref[2 * pp], acc_ref[2 * pp + 1]
        l_a = jnp.broadcast_to(acc_a[:, FOX_HD:FOX_HD + 1], (tq, LANES))
        l_b = jnp.broadcast_to(acc_b[:, 0:1], (tq, LANES))
        o_ref[:, pp * LANES:(pp + 1) * LANES] = (
            jnp.where(first, acc_a, acc_b) / jnp.where(first, l_a, l_b)).astype(BF16)


def _fox(proj, cb, *, batch, seq, tq, tk, npp):
    nq = seq // tq
    ngrp = FOX_HEADS // 2 // npp
    w = npp * LANES
    qb, kb, vb = FQ_OFF // w, FK_OFF // w, FV_OFF // w
    return pl.pallas_call(
        functools.partial(_fox_kernel, tq=tq, tk=tk),
        grid=(batch, ngrp, nq),
        in_specs=[
            pl.BlockSpec((tq, w), lambda b, p, i: (b * nq + i, qb + p)),
            pl.BlockSpec((seq, w), lambda b, p, i: (b, kb + p)),
            pl.BlockSpec((seq, w), lambda b, p, i: (b, vb + p)),
            pl.BlockSpec((seq, w), lambda b, p, i: (b, p)),
            pl.BlockSpec((8, LANES), lambda b, p, i: (0, 0)),
        ],
        out_specs=pl.BlockSpec((tq, w), lambda b, p, i: (b * nq + i, p)),
        out_shape=jax.ShapeDtypeStruct((batch * seq, FOX_W), BF16),
        scratch_shapes=[pltpu.VMEM((2 * npp, tq, LANES), F32),
                        pltpu.VMEM((2 * npp, tq, LANES), F32)],
        compiler_params=_params("parallel", "parallel", "arbitrary"),
        name="fox_attention")(proj, proj, proj, cb, _fox_lane_rows())


def _mlstm_kernel(u_ref, v_ref, mo_ref, r_ref, ct_ref, sel_ref, cw_ref, cb_ref,
                  nw_ref, o_ref, ubuf, q_s, k_s, st_ref, m_ref, *, tt):
    s_idx = pl.program_id(1)
    halo = 8

    @pl.when(s_idx == 0)
    def _():
        ubuf[0:halo, :] = jnp.zeros((halo, 2 * ML_W), F32)
        st_ref[...] = jnp.zeros_like(st_ref)
        m_ref[...] = jnp.zeros_like(m_ref)

    @pl.when(s_idx > 0)
    def _():
        ubuf[0:halo, :] = ubuf[tt:tt + halo, :]

    c = ML_CHUNK
    for blk in range(tt // c):
        ubuf[halo + blk * c:halo + (blk + 1) * c, :] = (
            u_ref[blk * c:(blk + 1) * c, :].astype(F32))

    def conv_block(blk):
        acc = cb_ref[...]
        for j in (CONV_K - 1,) + tuple(range(CONV_K - 1)):
            off = halo - (CONV_K - 1) + j + blk * c
            acc = acc + cw_ref[j:j + 1, :] * ubuf[off:off + c, :]
        qk = acc * jax.nn.sigmoid(acc)
        q_s[blk * c:(blk + 1) * c, :] = qk[:, :ML_W].astype(BF16)
        k_s[blk * c:(blk + 1) * c, :] = (qk[:, ML_W:] * (ML_HD ** -0.5)).astype(BF16)

    row = lax.broadcasted_iota(jnp.int32, (c, c), 0)
    col = lax.broadcasted_iota(jnp.int32, (c, c), 1)
    causal = col <= row
    ones_blk = jnp.ones((c, ML_HD), BF16)

    def chunk(ci):
        r0 = ci * c
        cb = ct_ref[0, :, pl.ds(r0, c)]
        rep = []
        for qi in range(3):
            hi, mid, lo = _split3(r_ref[pl.ds(r0, c), qi * LANES:(qi + 1) * LANES])
            rep.append(jnp.dot(jnp.concatenate([hi, mid], axis=1), sel_ref[...],
                               preferred_element_type=F32)
                       + jnp.dot(lo, sel_ref[0:LANES, :], preferred_element_type=F32))
        for h in range(ML_HEADS):
            hs = slice(h * ML_HD, (h + 1) * ML_HD)
            qh = q_s[pl.ds(r0, c), hs]
            kh = k_s[pl.ds(r0, c), hs]
            vh = v_ref[pl.ds(r0, c), hs]
            f_t, b_t, cm_t = rep[0][:, hs], rep[1][:, hs], rep[2][:, hs]
            b_row = cb[GATE_MF + h:GATE_MF + h + 1, :]
            m_prev = m_ref[h:h + 1, :]

            g_t = jnp.maximum(m_prev, cm_t)
            dw = jnp.exp(jnp.where(causal, b_row - jnp.concatenate([g_t, g_t], axis=1),
                                   -jnp.inf))
            inter_w = jnp.exp(m_prev - g_t)
            sqk = lax.dot_general(qh, kh, NT_DIMS, preferred_element_type=F32) * dw
            state = st_ref[h]
            qc = jnp.dot(qh, state.astype(BF16), preferred_element_type=F32)
            v_aug = jnp.concatenate([vh, ones_blk], axis=1)
            pv = jnp.dot(sqk.astype(BF16), v_aug, preferred_element_type=F32)
            num = inter_w * qc[:, :ML_HD] + pv[:, :ML_HD]
            den = inter_w * qc[:, ML_HD:] + pv[:, ML_HD:]
            hh = num / jnp.maximum(jnp.abs(den), jnp.exp(-(f_t + g_t)))
            ms = jnp.mean(hh * hh, axis=-1, keepdims=True)
            y = (hh * lax.rsqrt(ms + EPS) * nw_ref[:, hs]
                 * jax.nn.sigmoid(mo_ref[pl.ds(r0, c), hs].astype(F32)))
            o_ref[pl.ds(r0, c), hs] = y.astype(BF16)

            f_last = f_t[c - 1:c, :]
            m_new = jnp.maximum(f_last + m_prev, f_last + cm_t[c - 1:c, :])
            s_old = jnp.exp(f_last + m_prev - m_new)
            kw = (kh.astype(F32) * jnp.exp(f_last + b_t - m_new)).astype(BF16)
            upd = lax.dot_general(kw, v_aug, TN_DIMS, preferred_element_type=F32)
            st_ref[h] = jnp.concatenate([s_old, s_old], axis=1) * state + upd
            m_ref[h:h + 1, :] = m_new

    for ci in range(tt // c):
        conv_block(ci)
        chunk(ci)


def _mlstm(proj, rows, ct, conv_w, conv_b, norm_w, *, batch, seq, tt):
    ns = seq // tt
    ub = MQK_OFF // (2 * ML_W)
    vb = MV_OFF // ML_W
    ob = MO_OFF // ML_W
    rmap = lambda b, s: (b * ns + s, 0)
    sel = np.zeros((2, LANES, ML_W), np.float32)
    for h in range(ML_HEADS):
        sel[:, GATE_MF + h, h * ML_HD:(h + 1) * ML_HD] = 1.0
    sel = jnp.asarray(sel.reshape(2 * LANES, ML_W), BF16)
    return pl.pallas_call(
        functools.partial(_mlstm_kernel, tt=tt),
        grid=(batch, ns),
        in_specs=[
            pl.BlockSpec((tt, 2 * ML_W), lambda b, s: (b * ns + s, ub)),
            pl.BlockSpec((tt, ML_W), lambda b, s: (b * ns + s, vb)),
            pl.BlockSpec((tt, ML_W), lambda b, s: (b * ns + s, ob)),
            pl.BlockSpec((tt, 3 * LANES), rmap),
            pl.BlockSpec((1, GATE_USED, tt), lambda b, s: (b, 0, s)),
            pl.BlockSpec((2 * LANES, ML_W), lambda b, s: (0, 0)),
            pl.BlockSpec((CONV_K, 2 * ML_W), lambda b, s: (0, 0)),
            pl.BlockSpec((1, 2 * ML_W), lambda b, s: (0, 0)),
            pl.BlockSpec((1, ML_W), lambda b, s: (0, 0)),
        ],
        out_specs=pl.BlockSpec((tt, ML_W), rmap),
        out_shape=jax.ShapeDtypeStruct((batch * seq, ML_W), BF16),
        scratch_shapes=[pltpu.VMEM((tt + 8, 2 * ML_W), F32),
                        pltpu.VMEM((tt, ML_W), BF16),
                        pltpu.VMEM((tt, ML_W), BF16),
                        pltpu.VMEM((ML_HEADS, ML_HD, 2 * ML_HD), F32),
                        pltpu.VMEM((8, LANES), F32)],
        compiler_params=_params("parallel", "arbitrary"),
        name="mlstm")(proj, proj, proj, rows, ct, sel, conv_w, conv_b, norm_w)


def _mix_xattn_kernel(yf_ref, ym_ref, gt_ref, bg_ref, h_ref, wf_ref, wm_ref, wmix_ref,
                      g_ref, wq_ref, k_ref, v_ref, wo_ref, o_ref):
    g = jax.nn.sigmoid(gt_ref[...].astype(F32) + bg_ref[...])
    uf = jnp.dot(yf_ref[...], wf_ref[...], preferred_element_type=F32)
    um = jnp.dot(ym_ref[...], wm_ref[...], preferred_element_type=F32)
    merged = g[:, :D_MODEL] * uf + g[:, D_MODEL:] * um
    x = h_ref[...] + jnp.dot(merged.astype(BF16), wmix_ref[...], preferred_element_type=F32)

    hn = _rms(x, g_ref[...]).astype(BF16)
    q = jnp.dot(hn, wq_ref[...], preferred_element_type=F32)
    q = (q * (X_HD ** -0.5)).astype(BF16)
    outs = []
    for hd in range(X_HEADS):
        hs = slice(hd * X_HD, (hd + 1) * X_HD)
        s = lax.dot_general(q[:, hs], k_ref[:, hs], NT_DIMS, preferred_element_type=F32)
        e = jnp.exp(s - jnp.max(s, axis=-1, keepdims=True))
        pr = e / jnp.sum(e, axis=-1, keepdims=True)
        outs.append(jnp.dot(pr.astype(BF16), v_ref[:, hs], preferred_element_type=F32))
    o = jnp.concatenate(outs, axis=1).astype(BF16)
    o_ref[...] = x + jnp.dot(o, wo_ref[...], preferred_element_type=F32)


def _mix_xattn(yf, ym, proj, b_gate, h, wf, wm, wmix, gain, wq, kv, wo, *,
               layer, batch, seq, n_mem, tm):
    ns = seq // tm
    const = lambda b, s: (0, 0)
    row = lambda b, s: (b * ns + s, 0)
    return pl.pallas_call(
        _mix_xattn_kernel, grid=(batch, ns),
        in_specs=[
            pl.BlockSpec((tm, FOX_W), row),
            pl.BlockSpec((tm, ML_W), row),
            pl.BlockSpec((tm, 2 * D_MODEL), lambda b, s: (b * ns + s, G_OFF // (2 * D_MODEL))),
            pl.BlockSpec((1, 2 * D_MODEL), const),
            pl.BlockSpec((tm, D_MODEL), row),
            _layer_block((FOX_W, D_MODEL), layer),
            _layer_block((ML_W, D_MODEL), layer),
            _layer_block((D_MODEL, D_MODEL), layer),
            pl.BlockSpec((1, D_MODEL), const),
            _layer_block((D_MODEL, D_MODEL), layer),
            pl.BlockSpec((n_mem, D_MODEL), lambda b, s: (b, 0)),
            pl.BlockSpec((n_mem, D_MODEL), lambda b, s: (b, 1)),
            _layer_block((D_MODEL, D_MODEL), layer),
        ],
        out_specs=pl.BlockSpec((tm, D_MODEL), row),
        out_shape=jax.ShapeDtypeStruct((batch * seq, D_MODEL), F32),
        compiler_params=_params("parallel", "parallel"),
        name="mix_cross_attention")(yf, ym, proj, b_gate, h, wf, wm, wmix,
                                    gain, wq, kv, kv, wo)


def _mlp_kernel(h_ref, g_ref, w1_ref, w2_ref, gf_ref, o_ref, *, tf, final):
    x = h_ref[...]
    hn = _rms(x, g_ref[...]).astype(BF16)
    acc = x
    for j in range(D_FF // tf):
        fs = slice(j * tf, (j + 1) * tf)
        a = jnp.maximum(jnp.dot(hn, w1_ref[:, fs], preferred_element_type=F32), 0.0)
        acc = acc + jnp.dot((a * a).astype(BF16), w2_ref[fs, :], preferred_element_type=F32)
    o_ref[...] = _rms(acc, gf_ref[...]) if final else acc


def _mlp(h, gain, w1, w2, gain_final, *, layer, tm, tf, final):
    m = h.shape[0]
    const = lambda i: (0, 0)
    row = lambda i: (i, 0)
    return pl.pallas_call(
        functools.partial(_mlp_kernel, tf=tf, final=final),
        grid=(m // tm,),
        in_specs=[
            pl.BlockSpec((tm, D_MODEL), row),
            pl.BlockSpec((1, D_MODEL), const),
            _layer_block((D_MODEL, D_FF), layer),
            _layer_block((D_FF, D_MODEL), layer),
            pl.BlockSpec((1, D_MODEL), const),
        ],
        out_specs=pl.BlockSpec((tm, D_MODEL), row),
        out_shape=jax.ShapeDtypeStruct((m, D_MODEL), F32),
        compiler_params=_params("parallel"),
        name="mlp")(h, gain, w1, w2, gain_final)


def _pick(total, prefs):
    for t in prefs:
        if total % t == 0:
            return t
    raise ValueError(f"no tile in {prefs} divides {total}")


def kernel(x, mem, norm_mix, norm_xattn, norm_mem, norm_mlp, w_in, b_fox_f, b_mlstm_i,
           b_mlstm_f, b_gate, conv_w, conv_b, mlstm_norm, w_up_fox, w_up_mlstm, w_mix_out,
           w_xq, w_xkv, w_xo, w_mlp_in, w_mlp_out, norm_final):
    batch, seq, d = x.shape
    n_mem = mem.shape[1]
    depth = w_in.shape[0]
    assert d == D_MODEL and seq % ML_CHUNK == 0
    m = batch * seq

    tm_proj = _pick(m, (512, 256))
    tn_proj = 512
    t_seq = _pick(seq, (1024, 512, 256))
    tk = _pick(seq, (512, 256))
    tq = _pick(seq, (4 * tk, 2 * tk, tk))
    tm_x = _pick(seq, (512, 256))
    tm_mlp = _pick(m, (512, 256))
    tf = 1024

    sp = [0, FOX_W, 2 * FOX_W, 3 * FOX_W, 3 * FOX_W + FOX_HEADS]
    sp += [sp[-1] + 2 * ML_W, sp[-1] + 3 * ML_W, sp[-1] + 4 * ML_W]
    sp += [sp[-1] + ML_HEADS, sp[-1] + 2 * ML_HEADS, sp[-1] + 2 * ML_HEADS + 2 * D_MODEL]
    w_in_b = w_in.astype(BF16)
    seg = [w_in_b[:, :, a:b] for a, b in zip(sp[:-1], sp[1:])]
    fq, fk, fv, ff, mqk, mv, mo, mi, mf, gp = seg
    fq = (w_in[:, :, sp[0]:sp[1]] * (FOX_HD ** -0.5 * LOG2E)).astype(BF16)
    w_main = jnp.concatenate([gp, mqk, fq, fk, fv, mv, mo], axis=2)
    w_gate = jnp.concatenate(
        [ff, mi, mf, jnp.zeros((depth, d, LANES - GATE_USED), BF16)], axis=2)
    gate_bias = jnp.concatenate(
        [b_fox_f, b_mlstm_i, b_mlstm_f, jnp.zeros((depth, LANES - GATE_USED), F32)], axis=1)
    bf = lambda w: w.astype(BF16)
    w_up_fox, w_up_mlstm, w_mix_out = bf(w_up_fox), bf(w_up_mlstm), bf(w_mix_out)
    w_xq, w_xkv, w_xo = bf(w_xq), bf(w_xkv), bf(w_xo)
    w_mlp_in, w_mlp_out = bf(w_mlp_in), bf(w_mlp_out)

    h = x.reshape(m, d)
    mem2 = mem.reshape(batch * n_mem, d)
    for l in range(depth):
        proj, gpre = _in_proj(h, norm_mix[l][None], w_main, w_gate,
                              layer=l, tm=tm_proj, tn=tn_proj)
        rows, ct, cb = _gates(gpre, gate_bias[l][None], batch=batch, seq=seq, tg=t_seq)
        y_fox = _fox(proj, cb, batch=batch, seq=seq, tq=tq, tk=tk, npp=1)
        y_ml = _mlstm(proj, rows, ct, conv_w[l], conv_b[l][None], mlstm_norm[l][None],
                      batch=batch, seq=seq, tt=t_seq)
        kv = _norm_proj(mem2, norm_mem[l][None], w_xkv,
                        layer=l, tm=_pick(batch * n_mem, (1024, 512, 256)), tn=1024)
        h = _mix_xattn(y_fox, y_ml, proj, b_gate[l][None], h,
                       w_up_fox, w_up_mlstm, w_mix_out,
                       norm_xattn[l][None], w_xq, kv, w_xo,
                       layer=l, batch=batch, seq=seq, n_mem=n_mem, tm=tm_x)
        h = _mlp(h, norm_mlp[l][None], w_mlp_in, w_mlp_out, norm_final[None],
                 layer=l, tm=tm_mlp, tf=tf, final=(l == depth - 1))
    return h.reshape(batch, seq, d)
```

```python
import functools

import jax
import jax.numpy as jnp
import numpy as np
from jax import lax
from jax.experimental import pallas as pl
from jax.experimental.pallas import tpu as pltpu

F32 = jnp.float32
BF16 = jnp.bfloat16

D_MODEL = 1024
FOX_HEADS = 8
FOX_HD = 64
FOX_W = FOX_HEADS * FOX_HD
ML_HEADS = 4
ML_HD = 128
ML_W = ML_HEADS * ML_HD
CONV_K = 4
X_HEADS = 4
X_HD = D_MODEL // X_HEADS
D_FF = 4 * D_MODEL
EPS = 1e-6
LOG2E = 1.4426950408889634

LANES = 128
V7X_VMEM_LIMIT_BYTES = 56 * 1024 * 1024

G_OFF = 0
MQK_OFF = G_OFF + 2 * D_MODEL
FQ_OFF = MQK_OFF + 2 * ML_W
FK_OFF = FQ_OFF + FOX_W
FV_OFF = FK_OFF + FOX_W
MV_OFF = FV_OFF + FOX_W
MO_OFF = MV_OFF + ML_W
PROJ_W = MO_OFF + ML_W
GATE_FF = 0
GATE_MI = GATE_FF + FOX_HEADS
GATE_MF = GATE_MI + ML_HEADS
GATE_USED = GATE_MF + ML_HEADS

ML_CHUNK = 256
NT_DIMS = (((1,), (1,)), ((), ()))
TN_DIMS = (((0,), (0,)), ((), ()))


def _params(*sem):
    return pltpu.CompilerParams(dimension_semantics=sem,
                                vmem_limit_bytes=V7X_VMEM_LIMIT_BYTES)


def _rms(x, g):
    return x * lax.rsqrt(jnp.mean(x * x, axis=-1, keepdims=True) + EPS) * g


def _layer_block(shape, layer):
    zeros = (0,) * len(shape)
    return pl.BlockSpec((None,) + tuple(shape), lambda *_: (layer,) + zeros)


def _in_proj_kernel(x_ref, g_ref, w_ref, wg_ref, o_ref, og_ref, *, tn):
    xn = _rms(x_ref[...], g_ref[...]).astype(BF16)
    og_ref[...] = jnp.dot(xn, wg_ref[...], preferred_element_type=F32)
    for j in range(w_ref.shape[1] // tn):
        cs = slice(j * tn, (j + 1) * tn)
        o_ref[:, cs] = jnp.dot(xn, w_ref[:, cs], preferred_element_type=F32).astype(BF16)


def _in_proj(x, gain, w, wg, *, layer, tm, tn):
    m, d = x.shape
    n = w.shape[-1]
    const = lambda i: (0, 0)
    return pl.pallas_call(
        functools.partial(_in_proj_kernel, tn=tn), grid=(m // tm,),
        in_specs=[pl.BlockSpec((tm, d), lambda i: (i, 0)),
                  pl.BlockSpec((1, d), const),
                  _layer_block((d, n), layer),
                  _layer_block((d, LANES), layer)],
        out_specs=[pl.BlockSpec((tm, n), lambda i: (i, 0)),
                   pl.BlockSpec((tm, LANES), lambda i: (i, 0))],
        out_shape=[jax.ShapeDtypeStruct((m, n), BF16),
                   jax.ShapeDtypeStruct((m, LANES), F32)],
        compiler_params=_params("parallel"),
        name="norm_in_proj")(x, gain, w, wg)


def _norm_proj_kernel(x_ref, g_ref, w_ref, o_ref, xn_ref):
    @pl.when(pl.program_id(1) == 0)
    def _():
        xn_ref[...] = _rms(x_ref[...], g_ref[...]).astype(BF16)

    o_ref[...] = jnp.dot(xn_ref[...], w_ref[...],
                         preferred_element_type=F32).astype(BF16)


def _norm_proj(x, gain, w, *, layer, tm, tn):
    m, d = x.shape
    n = w.shape[-1]
    return pl.pallas_call(
        _norm_proj_kernel, grid=(m // tm, n // tn),
        in_specs=[pl.BlockSpec((tm, d), lambda i, j: (i, 0)),
                  pl.BlockSpec((1, d), lambda i, j: (0, 0)),
                  pl.BlockSpec((None, d, tn), lambda i, j: (layer, 0, j))],
        out_specs=pl.BlockSpec((tm, tn), lambda i, j: (i, j)),
        out_shape=jax.ShapeDtypeStruct((m, n), BF16),
        scratch_shapes=[pltpu.VMEM((tm, d), BF16)],
        compiler_params=_params("parallel", "arbitrary"),
        name="norm_proj")(x, gain, w)


def _split3(v):
    hi = v.astype(BF16)
    r1 = v - hi.astype(F32)
    mid = r1.astype(BF16)
    lo = (r1 - mid.astype(F32)).astype(BF16)
    return hi, mid, lo


def _gates_kernel(x_ref, b_ref, pm_ref, r_ref, ct_ref, cb_ref, carry_ref, *, tg):
    @pl.when(pl.program_id(1) == 0)
    def _():
        carry_ref[...] = jnp.zeros_like(carry_ref)

    c = ML_CHUNK
    row = lax.broadcasted_iota(jnp.int32, (c, c), 0)
    col = lax.broadcasted_iota(jnp.int32, (c, c), 1)
    causal = row >= col
    tri = jnp.where(causal, 1.0, 0.0).astype(BF16)
    lane = lax.broadcasted_iota(jnp.int32, (c, LANES), 1)
    is_fox = lane < GATE_MI
    is_ig = (lane >= GATE_MI) & (lane < GATE_MF)
    is_used = lane < GATE_USED
    bias = b_ref[...]
    for blk in range(tg // c):
        rs = slice(blk * c, (blk + 1) * c)
        x = x_ref[rs, :] + bias
        logsig = jnp.minimum(x, 0.0) - jnp.log1p(jnp.exp(-jnp.abs(x)))
        val = jnp.where(is_ig, x, logsig)
        val = jnp.where(is_used, val, 0.0)
        csp = jnp.dot(tri, jnp.concatenate(_split3(val), axis=1), preferred_element_type=F32)
        cs = csp[:, 0:LANES] + csp[:, LANES:2 * LANES] + csp[:, 2 * LANES:3 * LANES]
        glob = cs + carry_ref[0:1, :]
        carry_ref[0:1, :] = glob[c - 1:c, :]
        b_t = pltpu.roll(val, GATE_MF - GATE_MI, axis=1) - cs
        bt = b_t.T
        ct_ref[0, :, rs] = bt[:GATE_USED, :]
        cm = jnp.zeros((c, LANES), F32)
        for h in range(ML_HEADS):
            k = GATE_MF + h
            run_max = jnp.max(jnp.where(causal, bt[k:k + 1, :], -jnp.inf),
                              axis=-1, keepdims=True)
            cm = jnp.where(lane == k, run_max, cm)
        r_ref[rs, 0:LANES] = cs
        r_ref[rs, LANES:2 * LANES] = b_t
        r_ref[rs, 2 * LANES:3 * LANES] = cm
        terms = _split3(jnp.where(is_fox, glob * (-LOG2E), 0.0))
        cb_ref[rs, :] = jnp.dot(jnp.concatenate(terms, axis=1), pm_ref[...],
                                preferred_element_type=F32).astype(BF16)


def _fox_bias_placement():
    pm = np.zeros((3, LANES, FOX_W), np.float32)
    for h in range(FOX_HEADS):
        base = (h // 2) * LANES + (FOX_HD if h % 2 == 0 else 0)
        for t in range(3):
            pm[t, h, base + t] = 1.0
    return jnp.asarray(pm.reshape(3 * LANES, FOX_W), BF16)


def _gates(gpre, bias, *, batch, seq, tg):
    ns = seq // tg
    rmap = lambda b, s: (b * ns + s, 0)
    return pl.pallas_call(
        functools.partial(_gates_kernel, tg=tg),
        grid=(batch, ns),
        in_specs=[pl.BlockSpec((tg, LANES), rmap),
                  pl.BlockSpec((1, LANES), lambda b, s: (0, 0)),
                  pl.BlockSpec((3 * LANES, FOX_W), lambda b, s: (0, 0))],
        out_specs=[pl.BlockSpec((tg, 3 * LANES), rmap),
                   pl.BlockSpec((1, GATE_USED, tg), lambda b, s: (b, 0, s)),
                   pl.BlockSpec((tg, FOX_W), rmap)],
        out_shape=[jax.ShapeDtypeStruct((batch * seq, 3 * LANES), F32),
                   jax.ShapeDtypeStruct((batch, GATE_USED, seq), F32),
                   jax.ShapeDtypeStruct((batch * seq, FOX_W), BF16)],
        scratch_shapes=[pltpu.VMEM((8, LANES), F32)],
        compiler_params=_params("parallel", "arbitrary"),
        name="gates")(gpre, bias, _fox_bias_placement())


def _fox_lane_rows():
    r = np.zeros((8, LANES), np.float32)
    r[0, :FOX_HD] = 1.0
    r[1, FOX_HD:] = 1.0
    r[2, FOX_HD:FOX_HD + 3] = 1.0
    r[3, 0:3] = 1.0
    r[4, FOX_HD] = 1.0
    r[5, 0] = 1.0
    return jnp.asarray(r, BF16)


def _fox_kernel(q_ref, k_ref, v_ref, cb_ref, mk_ref, o_ref, m_ref, acc_ref, *, tq, tk):
    i = pl.program_id(2)
    n_sub = tq // tk
    npp = q_ref.shape[1] // LANES
    lo, hi = mk_ref[0:1, :], mk_ref[1:2, :]
    one_a, one_b = mk_ref[4:5, :], mk_ref[5:6, :]
    qh = []
    for pp in range(npp):
        q2 = q_ref[:, pp * LANES:(pp + 1) * LANES]
        qh += [q2 * lo + mk_ref[2:3, :], q2 * hi + mk_ref[3:4, :]]

    m_ref[...] = jnp.full_like(m_ref, -jnp.inf)
    acc_ref[...] = jnp.zeros_like(acc_ref)
    tri = (lax.broadcasted_iota(jnp.int32, (tk, tk), 1)
           <= lax.broadcasted_iota(jnp.int32, (tk, tk), 0))

    def step(j, row0, masked):
        rows = tq - row0
        r0 = pl.multiple_of(j * tk, tk)
        for pp in range(npp):
            ps = slice(pp * LANES, (pp + 1) * LANES)
            ks = k_ref[pl.ds(r0, tk), ps]
            vs = v_ref[pl.ds(r0, tk), ps]
            cb = cb_ref[pl.ds(r0, tk), ps]
            kh = (ks * lo + cb * hi, ks * hi + cb * lo)
            vh = (vs * lo + one_a, vs * hi + one_b)
            for h in range(2):
                hh = 2 * pp + h
                s = lax.dot_general(qh[hh][row0:, :], kh[h], NT_DIMS,
                                    preferred_element_type=F32)
                if masked:
                    top = jnp.where(tri, s[:tk], -jnp.inf)
                    s = top if rows == tk else jnp.concatenate([top, s[tk:]], axis=0)
                m_old = m_ref[hh, row0:, :]
                m_new = jnp.maximum(m_old, jnp.max(s, axis=-1, keepdims=True))
                alpha = jnp.exp2(m_old - m_new)
                pe = jnp.exp2(s - jnp.concatenate([m_new] * (tk // LANES), axis=1))
                acc_ref[hh, row0:, :] = alpha * acc_ref[hh, row0:, :] + jnp.dot(
                    pe.astype(BF16), vh[h], preferred_element_type=F32)
                m_ref[hh, row0:, :] = m_new

    def body(jj, carry):
        for u in range(n_sub):
            step(jj * n_sub + u, 0, False)
        return carry

    lax.fori_loop(0, i, body, 0)
    for d in range(n_sub):
        step(i * n_sub + d, d * tk, True)

    first = lax.broadcasted_iota(jnp.int32, (tq, LANES), 1) < FOX_HD
    for pp in range(npp):
        acc_a, acc_b = acc_ref[2 * pp], acc_ref[2 * pp + 1]
        l_a = jnp.broadcast_to(acc_a[:, FOX_HD:FOX_HD + 1], (tq, LANES))
        l_b = jnp.broadcast_to(acc_b[:, 0:1], (tq, LANES))
        o_ref[:, pp * LANES:(pp + 1) * LANES] = (
            jnp.where(first, acc_a, acc_b) / jnp.where(first, l_a, l_b)).astype(BF16)


def _fox(proj, cb, *, batch, seq, tq, tk, npp):
    nq = seq // tq
    ngrp = FOX_HEADS // 2 // npp
    w = npp * LANES
    qb, kb, vb = FQ_OFF // w, FK_OFF // w, FV_OFF // w
    return pl.pallas_call(
        functools.partial(_fox_kernel, tq=tq, tk=tk),
        grid=(batch, ngrp, nq),
        in_specs=[
            pl.BlockSpec((tq, w), lambda b, p, i: (b * nq + i, qb + p)),
            pl.BlockSpec((seq, w), lambda b, p, i: (b, kb + p)),
            pl.BlockSpec((seq, w), lambda b, p, i: (b, vb + p)),
            pl.BlockSpec((seq, w), lambda b, p, i: (b, p)),
            pl.BlockSpec((8, LANES), lambda b, p, i: (0, 0)),
        ],
        out_specs=pl.BlockSpec((tq, w), lambda b, p, i: (b * nq + i, p)),
        out_shape=jax.ShapeDtypeStruct((batch * seq, FOX_W), BF16),
        scratch_shapes=[pltpu.VMEM((2 * npp, tq, LANES), F32),
                        pltpu.VMEM((2 * npp, tq, LANES), F32)],
        compiler_params=_params("parallel", "parallel", "arbitrary"),
        name="fox_attention")(proj, proj, proj, cb, _fox_lane_rows())


def _mlstm_kernel(u_ref, v_ref, mo_ref, r_ref, ct_ref, sel_ref, cw_ref, cb_ref,
                  nw_ref, o_ref, ubuf, q_s, k_s, st_ref, m_ref, *, tt):
    s_idx = pl.program_id(1)
    halo = 8

    @pl.when(s_idx == 0)
    def _():
        ubuf[0:halo, :] = jnp.zeros((halo, 2 * ML_W), F32)
        st_ref[...] = jnp.zeros_like(st_ref)
        m_ref[...] = jnp.zeros_like(m_ref)

    @pl.when(s_idx > 0)
    def _():
        ubuf[0:halo, :] = ubuf[tt:tt + halo, :]

    c = ML_CHUNK
    for blk in range(tt // c):
        ubuf[halo + blk * c:halo + (blk + 1) * c, :] = (
            u_ref[blk * c:(blk + 1) * c, :].astype(F32))

    def conv_block(blk):
        acc = cb_ref[...]
        for j in (CONV_K - 1,) + tuple(range(CONV_K - 1)):
            off = halo - (CONV_K - 1) + j + blk * c
            acc = acc + cw_ref[j:j + 1, :] * ubuf[off:off + c, :]
        qk = acc * jax.nn.sigmoid(acc)
        q_s[blk * c:(blk + 1) * c, :] = qk[:, :ML_W].astype(BF16)
        k_s[blk * c:(blk + 1) * c, :] = (qk[:, ML_W:] * (ML_HD ** -0.5)).astype(BF16)

    row = lax.broadcasted_iota(jnp.int32, (c, c), 0)
    col = lax.broadcasted_iota(jnp.int32, (c, c), 1)
    causal = col <= row
    ones_blk = jnp.ones((c, ML_HD), BF16)

    def chunk(ci):
        r0 = ci * c
        cb = ct_ref[0, :, pl.ds(r0, c)]
        rep = []
        for qi in range(3):
            hi, mid, lo = _split3(r_ref[pl.ds(r0, c), qi * LANES:(qi + 1) * LANES])
            rep.append(jnp.dot(jnp.concatenate([hi, mid], axis=1), sel_ref[...],
                               preferred_element_type=F32)
                       + jnp.dot(lo, sel_ref[0:LANES, :], preferred_element_type=F32))
        for h in range(ML_HEADS):
            hs = slice(h * ML_HD, (h + 1) * ML_HD)
            qh = q_s[pl.ds(r0, c), hs]
            kh = k_s[pl.ds(r0, c), hs]
            vh = v_ref[pl.ds(r0, c), hs]
            f_t, b_t, cm_t = rep[0][:, hs], rep[1][:, hs], rep[2][:, hs]
            b_row = cb[GATE_MF + h:GATE_MF + h + 1, :]
            m_prev = m_ref[h:h + 1, :]

            g_t = jnp.maximum(m_prev, cm_t)
            dw = jnp.exp(jnp.where(causal, b_row - jnp.concatenate([g_t, g_t], axis=1),
                                   -jnp.inf))
            inter_w = jnp.exp(m_prev - g_t)
            sqk = lax.dot_general(qh, kh, NT_DIMS, preferred_element_type=F32) * dw
            state = st_ref[h]
            qc = jnp.dot(qh, state.astype(BF16), preferred_element_type=F32)
            v_aug = jnp.concatenate([vh, ones_blk], axis=1)
            pv = jnp.dot(sqk.astype(BF16), v_aug, preferred_element_type=F32)
            num = inter_w * qc[:, :ML_HD] + pv[:, :ML_HD]
            den = inter_w * qc[:, ML_HD:] + pv[:, ML_HD:]
            hh = num / jnp.maximum(jnp.abs(den), jnp.exp(-(f_t + g_t)))
            ms = jnp.mean(hh * hh, axis=-1, keepdims=True)
            y = (hh * lax.rsqrt(ms + EPS) * nw_ref[:, hs]
                 * jax.nn.sigmoid(mo_ref[pl.ds(r0, c), hs].astype(F32)))
            o_ref[pl.ds(r0, c), hs] = y.astype(BF16)

            f_last = f_t[c - 1:c, :]
            m_new = jnp.maximum(f_last + m_prev, f_last + cm_t[c - 1:c, :])
            s_old = jnp.exp(f_last + m_prev - m_new)
            kw = (kh.astype(F32) * jnp.exp(f_last + b_t - m_new)).astype(BF16)
            upd = lax.dot_general(kw, v_aug, TN_DIMS, preferred_element_type=F32)
            st_ref[h] = jnp.concatenate([s_old, s_old], axis=1) * state + upd
            m_ref[h:h + 1, :] = m_new

    for ci in range(tt // c):
        conv_block(ci)
        chunk(ci)


def _mlstm(proj, rows, ct, conv_w, conv_b, norm_w, *, batch, seq, tt):
    ns = seq // tt
    ub = MQK_OFF // (2 * ML_W)
    vb = MV_OFF // ML_W
    ob = MO_OFF // ML_W
    rmap = lambda b, s: (b * ns + s, 0)
    sel = np.zeros((2, LANES, ML_W), np.float32)
    for h in range(ML_HEADS):
        sel[:, GATE_MF + h, h * ML_HD:(h + 1) * ML_HD] = 1.0
    sel = jnp.asarray(sel.reshape(2 * LANES, ML_W), BF16)
    return pl.pallas_call(
        functools.partial(_mlstm_kernel, tt=tt),
        grid=(batch, ns),
        in_specs=[
            pl.BlockSpec((tt, 2 * ML_W), lambda b, s: (b * ns + s, ub)),
            pl.BlockSpec((tt, ML_W), lambda b, s: (b * ns + s, vb)),
            pl.BlockSpec((tt, ML_W), lambda b, s: (b * ns + s, ob)),
            pl.BlockSpec((tt, 3 * LANES), rmap),
            pl.BlockSpec((1, GATE_USED, tt), lambda b, s: (b, 0, s)),
            pl.BlockSpec((2 * LANES, ML_W), lambda b, s: (0, 0)),
            pl.BlockSpec((CONV_K, 2 * ML_W), lambda b, s: (0, 0)),
            pl.BlockSpec((1, 2 * ML_W), lambda b, s: (0, 0)),
            pl.BlockSpec((1, ML_W), lambda b, s: (0, 0)),
        ],
        out_specs=pl.BlockSpec((tt, ML_W), rmap),
        out_shape=jax.ShapeDtypeStruct((batch * seq, ML_W), BF16),
        scratch_shapes=[pltpu.VMEM((tt + 8, 2 * ML_W), F32),
                        pltpu.VMEM((tt, ML_W), BF16),
                        pltpu.VMEM((tt, ML_W), BF16),
                        pltpu.VMEM((ML_HEADS, ML_HD, 2 * ML_HD), F32),
                        pltpu.VMEM((8, LANES), F32)],
        compiler_params=_params("parallel", "arbitrary"),
        name="mlstm")(proj, proj, proj, rows, ct, sel, conv_w, conv_b, norm_w)


def _mix_xattn_kernel(yf_ref, ym_ref, gt_ref, bg_ref, h_ref, wf_ref, wm_ref, wmix_ref,
                      g_ref, wq_ref, k_ref, v_ref, wo_ref, o_ref):
    g = jax.nn.sigmoid(gt_ref[...].astype(F32) + bg_ref[...])
    uf = jnp.dot(yf_ref[...], wf_ref[...], preferred_element_type=F32)
    um = jnp.dot(ym_ref[...], wm_ref[...], preferred_element_type=F32)
    merged = g[:, :D_MODEL] * uf + g[:, D_MODEL:] * um
    x = h_ref[...] + jnp.dot(merged.astype(BF16), wmix_ref[...], preferred_element_type=F32)

    hn = _rms(x, g_ref[...]).astype(BF16)
    q = jnp.dot(hn, wq_ref[...], preferred_element_type=F32)
    q = (q * (X_HD ** -0.5)).astype(BF16)
    outs = []
    for hd in range(X_HEADS):
        hs = slice(hd * X_HD, (hd + 1) * X_HD)
        s = lax.dot_general(q[:, hs], k_ref[:, hs], NT_DIMS, preferred_element_type=F32)
        e = jnp.exp(s - jnp.max(s, axis=-1, keepdims=True))
        pr = e / jnp.sum(e, axis=-1, keepdims=True)
        outs.append(jnp.dot(pr.astype(BF16), v_ref[:, hs], preferred_element_type=F32))
    o = jnp.concatenate(outs, axis=1).astype(BF16)
    o_ref[...] = x + jnp.dot(o, wo_ref[...], preferred_element_type=F32)


def _mix_xattn(yf, ym, proj, b_gate, h, wf, wm, wmix, gain, wq, kv, wo, *,
               layer, batch, seq, n_mem, tm):
    ns = seq // tm
    const = lambda b, s: (0, 0)
    row = lambda b, s: (b * ns + s, 0)
    return pl.pallas_call(
        _mix_xattn_kernel, grid=(batch, ns),
        in_specs=[
            pl.BlockSpec((tm, FOX_W), row),
            pl.BlockSpec((tm, ML_W), row),
            pl.BlockSpec((tm, 2 * D_MODEL), lambda b, s: (b * ns + s, G_OFF // (2 * D_MODEL))),
            pl.BlockSpec((1, 2 * D_MODEL), const),
            pl.BlockSpec((tm, D_MODEL), row),
            _layer_block((FOX_W, D_MODEL), layer),
            _layer_block((ML_W, D_MODEL), layer),
            _layer_block((D_MODEL, D_MODEL), layer),
            pl.BlockSpec((1, D_MODEL), const),
            _layer_block((D_MODEL, D_MODEL), layer),
            pl.BlockSpec((n_mem, D_MODEL), lambda b, s: (b, 0)),
            pl.BlockSpec((n_mem, D_MODEL), lambda b, s: (b, 1)),
            _layer_block((D_MODEL, D_MODEL), layer),
        ],
        out_specs=pl.BlockSpec((tm, D_MODEL), row),
        out_shape=jax.ShapeDtypeStruct((batch * seq, D_MODEL), F32),
        compiler_params=_params("parallel", "parallel"),
        name="mix_cross_attention")(yf, ym, proj, b_gate, h, wf, wm, wmix,
                                    gain, wq, kv, kv, wo)


def _mlp_kernel(h_ref, g_ref, w1_ref, w2_ref, gf_ref, o_ref, *, tf, final):
    x = h_ref[...]
    hn = _rms(x, g_ref[...]).astype(BF16)
    acc = x
    for j in range(D_FF // tf):
        fs = slice(j * tf, (j + 1) * tf)
        a = jnp.maximum(jnp.dot(hn, w1_ref[:, fs], preferred_element_type=F32), 0.0)
        acc = acc + jnp.dot((a * a).astype(BF16), w2_ref[fs, :], preferred_element_type=F32)
    o_ref[...] = _rms(acc, gf_ref[...]) if final else acc


def _mlp(h, gain, w1, w2, gain_final, *, layer, tm, tf, final):
    m = h.shape[0]
    const = lambda i: (0, 0)
    row = lambda i: (i, 0)
    return pl.pallas_call(
        functools.partial(_mlp_kernel, tf=tf, final=final),
        grid=(m // tm,),
        in_specs=[
            pl.BlockSpec((tm, D_MODEL), row),
            pl.BlockSpec((1, D_MODEL), const),
            _layer_block((D_MODEL, D_FF), layer),
            _layer_block((D_FF, D_MODEL), layer),
            pl.BlockSpec((1, D_MODEL), const),
        ],
        out_specs=pl.BlockSpec((tm, D_MODEL), row),
        out_shape=jax.ShapeDtypeStruct((m, D_MODEL), F32),
        compiler_params=_params("parallel"),
        name="mlp")(h, gain, w1, w2, gain_final)


def _pick(total, prefs):
    for t in prefs:
        if total % t == 0:
            return t
    raise ValueError(f"no tile in {prefs} divides {total}")


def kernel(x, mem, norm_mix, norm_xattn, norm_mem, norm_mlp, w_in, b_fox_f, b_mlstm_i,
           b_mlstm_f, b_gate, conv_w, conv_b, mlstm_norm, w_up_fox, w_up_mlstm, w_mix_out,
           w_xq, w_xkv, w_xo, w_mlp_in, w_mlp_out, norm_final):
    batch, seq, d = x.shape
    n_mem = mem.shape[1]
    depth = w_in.shape[0]
    assert d == D_MODEL and seq % ML_CHUNK == 0
    m = batch * seq

    tm_proj = _pick(m, (512, 256))
    tn_proj = 512
    t_seq = _pick(seq, (1024, 512, 256))
    tk = _pick(seq, (512, 256))
    tq = _pick(seq, (4 * tk, 2 * tk, tk))
    tm_x = _pick(seq, (512, 256))
    tm_mlp = _pick(m, (512, 256))
    tf = 1024

    sp = [0, FOX_W, 2 * FOX_W, 3 * FOX_W, 3 * FOX_W + FOX_HEADS]
    sp += [sp[-1] + 2 * ML_W, sp[-1] + 3 * ML_W, sp[-1] + 4 * ML_W]
    sp += [sp[-1] + ML_HEADS, sp[-1] + 2 * ML_HEADS, sp[-1] + 2 * ML_HEADS + 2 * D_MODEL]
    w_in_b = w_in.astype(BF16)
    seg = [w_in_b[:, :, a:b] for a, b in zip(sp[:-1], sp[1:])]
    fq, fk, fv, ff, mqk, mv, mo, mi, mf, gp = seg
    fq = (w_in[:, :, sp[0]:sp[1]] * (FOX_HD ** -0.5 * LOG2E)).astype(BF16)
    w_main = jnp.concatenate([gp, mqk, fq, fk, fv, mv, mo], axis=2)
    w_gate = jnp.concatenate(
        [ff, mi, mf, jnp.zeros((depth, d, LANES - GATE_USED), BF16)], axis=2)
    gate_bias = jnp.concatenate(
        [b_fox_f, b_mlstm_i, b_mlstm_f, jnp.zeros((depth, LANES - GATE_USED), F32)], axis=1)
    bf = lambda w: w.astype(BF16)
    w_up_fox, w_up_mlstm, w_mix_out = bf(w_up_fox), bf(w_up_mlstm), bf(w_mix_out)
    w_xq, w_xkv, w_xo = bf(w_xq), bf(w_xkv), bf(w_xo)
    w_mlp_in, w_mlp_out = bf(w_mlp_in), bf(w_mlp_out)

    h = x.reshape(m, d)
    mem2 = mem.reshape(batch * n_mem, d)
    for l in range(depth):
        proj, gpre = _in_proj(h, norm_mix[l][None], w_main, w_gate,
                              layer=l, tm=tm_proj, tn=tn_proj)
        rows, ct, cb = _gates(gpre, gate_bias[l][None], batch=batch, seq=seq, tg=t_seq)
        y_fox = _fox(proj, cb, batch=batch, seq=seq, tq=tq, tk=tk, npp=1)
        y_ml = _mlstm(proj, rows, ct, conv_w[l], conv_b[l][None], mlstm_norm[l][None],
                      batch=batch, seq=seq, tt=t_seq)
        kv = _norm_proj(mem2, norm_mem[l][None], w_xkv,
                        layer=l, tm=_pick(batch * n_mem, (1024, 512, 256)), tn=1024)
        h = _mix_xattn(y_fox, y_ml, proj, b_gate[l][None], h,
                       w_up_fox, w_up_mlstm, w_mix_out,
                       norm_xattn[l][None], w_xq, kv, w_xo,
                       layer=l, batch=batch, seq=seq, n_mem=n_mem, tm=tm_x)
        h = _mlp(h, norm_mlp[l][None], w_mlp_in, w_mlp_out, norm_final[None],
                 layer=l, tm=tm_mlp, tf=tf, final=(l == depth - 1))
    return h.reshape(batch, seq, d)
```

```python
import functools

import jax
import jax.numpy as jnp
import numpy as np
from jax import lax
from jax.experimental import pallas as pl
from jax.experimental.pallas import tpu as pltpu

F32 = jnp.float32
BF16 = jnp.bfloat16

D_MODEL = 1024
FOX_HEADS = 8
FOX_HD = 64
FOX_W = FOX_HEADS * FOX_HD
ML_HEADS = 4
ML_HD = 128
ML_W = ML_HEADS * ML_HD
CONV_K = 4
X_HEADS = 4
X_HD = D_MODEL // X_HEADS
D_FF = 4 * D_MODEL
EPS = 1e-6
LOG2E = 1.4426950408889634

LANES = 128
SUBLANES = 8
V7X_VMEM_LIMIT_BYTES = 56 * 1024 * 1024

G_OFF = 0
MQK_OFF = G_OFF + 2 * D_MODEL
FQ_OFF = MQK_OFF + 2 * ML_W
FK_OFF = FQ_OFF + FOX_W
FV_OFF = FK_OFF + FOX_W
MV_OFF = FV_OFF + FOX_W
MO_OFF = MV_OFF + ML_W
PROJ_W = MO_OFF + ML_W
GATE_FF = 0
GATE_MI = GATE_FF + FOX_HEADS
GATE_MF = GATE_MI + ML_HEADS
GATE_USED = GATE_MF + ML_HEADS

ML_CHUNK = 256
NT_DIMS = (((1,), (1,)), ((), ()))
TN_DIMS = (((0,), (0,)), ((), ()))


def _params(*sem):
    return pltpu.CompilerParams(dimension_semantics=sem,
                                vmem_limit_bytes=V7X_VMEM_LIMIT_BYTES)


def _rms(x, g):
    return x * lax.rsqrt(jnp.mean(x * x, axis=-1, keepdims=True) + EPS) * g


def _layer_block(shape, layer):
    zeros = (0,) * len(shape)
    return pl.BlockSpec((None,) + tuple(shape), lambda *_: (layer,) + zeros)


def _in_proj_kernel(x_ref, g_ref, w_ref, wg_ref, o_ref, og_ref, *, tn):
    xn = _rms(x_ref[...], g_ref[...]).astype(BF16)
    og_ref[...] = jnp.dot(xn, wg_ref[...], preferred_element_type=F32)
    for j in range(w_ref.shape[1] // tn):
        cs = slice(j * tn, (j + 1) * tn)
        o_ref[:, cs] = jnp.dot(xn, w_ref[:, cs], preferred_element_type=F32).astype(BF16)


def _in_proj(x, gain, w, wg, *, layer, tm, tn):
    m, d = x.shape
    n = w.shape[-1]
    const = lambda i: (0, 0)
    return pl.pallas_call(
        functools.partial(_in_proj_kernel, tn=tn), grid=(m // tm,),
        in_specs=[pl.BlockSpec((tm, d), lambda i: (i, 0)),
                  pl.BlockSpec((1, d), const),
                  _layer_block((d, n), layer),
                  _layer_block((d, LANES), layer)],
        out_specs=[pl.BlockSpec((tm, n), lambda i: (i, 0)),
                   pl.BlockSpec((tm, LANES), lambda i: (i, 0))],
        out_shape=[jax.ShapeDtypeStruct((m, n), BF16),
                   jax.ShapeDtypeStruct((m, LANES), F32)],
        compiler_params=_params("parallel"),
        name="norm_in_proj")(x, gain, w, wg)


def _norm_proj_kernel(x_ref, g_ref, w_ref, o_ref, xn_ref):
    @pl.when(pl.program_id(1) == 0)
    def _():
        xn_ref[...] = _rms(x_ref[...], g_ref[...]).astype(BF16)

    o_ref[...] = jnp.dot(xn_ref[...], w_ref[...],
                         preferred_element_type=F32).astype(BF16)


def _norm_proj(x, gain, w, *, layer, tm, tn):
    m, d = x.shape
    n = w.shape[-1]
    return pl.pallas_call(
        _norm_proj_kernel, grid=(m // tm, n // tn),
        in_specs=[pl.BlockSpec((tm, d), lambda i, j: (i, 0)),
                  pl.BlockSpec((1, d), lambda i, j: (0, 0)),
                  pl.BlockSpec((None, d, tn), lambda i, j: (layer, 0, j))],
        out_specs=pl.BlockSpec((tm, tn), lambda i, j: (i, j)),
        out_shape=jax.ShapeDtypeStruct((m, n), BF16),
        scratch_shapes=[pltpu.VMEM((tm, d), BF16)],
        compiler_params=_params("parallel", "arbitrary"),
        name="norm_proj")(x, gain, w)


def _split3(v):
    hi = v.astype(BF16)
    r1 = v - hi.astype(F32)
    mid = r1.astype(BF16)
    lo = (r1 - mid.astype(F32)).astype(BF16)
    return hi, mid, lo


def _gates_kernel(x_ref, b_ref, pm_ref, r_ref, ct_ref, cb_ref, carry_ref, *, tg):
    @pl.when(pl.program_id(1) == 0)
    def _():
        carry_ref[...] = jnp.zeros_like(carry_ref)

    c = ML_CHUNK
    row = lax.broadcasted_iota(jnp.int32, (c, c), 0)
    col = lax.broadcasted_iota(jnp.int32, (c, c), 1)
    causal = row >= col
    tri = jnp.where(causal, 1.0, 0.0).astype(BF16)
    lane = lax.broadcasted_iota(jnp.int32, (c, LANES), 1)
    is_fox = lane < GATE_MI
    is_ig = (lane >= GATE_MI) & (lane < GATE_MF)
    is_used = lane < GATE_USED
    bias = b_ref[...]
    for blk in range(tg // c):
        rs = slice(blk * c, (blk + 1) * c)
        x = x_ref[rs, :] + bias
        logsig = jnp.minimum(x, 0.0) - jnp.log1p(jnp.exp(-jnp.abs(x)))
        val = jnp.where(is_ig, x, logsig)
        val = jnp.where(is_used, val, 0.0)
        csp = jnp.dot(tri, jnp.concatenate(_split3(val), axis=1), preferred_element_type=F32)
        cs = csp[:, 0:LANES] + csp[:, LANES:2 * LANES] + csp[:, 2 * LANES:3 * LANES]
        glob = cs + carry_ref[0:1, :]
        carry_ref[0:1, :] = glob[c - 1:c, :]
        f_t = cs * LOG2E
        b_t = (pltpu.roll(val, GATE_MF - GATE_MI, axis=1) - cs) * LOG2E
        bt = b_t.T
        ct_ref[0, :, rs] = bt[:GATE_USED, :]
        cm = jnp.zeros((c, LANES), F32)
        for h in range(ML_HEADS):
            k = GATE_MF + h
            run_max = jnp.max(jnp.where(causal, bt[k:k + 1, :], -jnp.inf),
                              axis=-1, keepdims=True)
            cm = jnp.where(lane == k, run_max, cm)
        r_ref[rs, 0:LANES] = f_t
        r_ref[rs, LANES:2 * LANES] = b_t
        r_ref[rs, 2 * LANES:3 * LANES] = cm
        terms = _split3(jnp.where(is_fox, glob * (-LOG2E), 0.0))
        cb_ref[rs, :] = jnp.dot(jnp.concatenate(terms, axis=1), pm_ref[...],
                                preferred_element_type=F32).astype(BF16)


def _fox_bias_placement():
    pm = np.zeros((3, LANES, FOX_W), np.float32)
    for h in range(FOX_HEADS):
        base = (h // 2) * LANES + (FOX_HD if h % 2 == 0 else 0)
        for t in range(3):
            pm[t, h, base + t] = 1.0
    return jnp.asarray(pm.reshape(3 * LANES, FOX_W), BF16)


def _gates(gpre, bias, *, batch, seq, tg):
    ns = seq // tg
    rmap = lambda b, s: (b * ns + s, 0)
    return pl.pallas_call(
        functools.partial(_gates_kernel, tg=tg),
        grid=(batch, ns),
        in_specs=[pl.BlockSpec((tg, LANES), rmap),
                  pl.BlockSpec((1, LANES), lambda b, s: (0, 0)),
                  pl.BlockSpec((3 * LANES, FOX_W), lambda b, s: (0, 0))],
        out_specs=[pl.BlockSpec((tg, 3 * LANES), rmap),
                   pl.BlockSpec((1, GATE_USED, tg), lambda b, s: (b, 0, s)),
                   pl.BlockSpec((tg, FOX_W), rmap)],
        out_shape=[jax.ShapeDtypeStruct((batch * seq, 3 * LANES), F32),
                   jax.ShapeDtypeStruct((batch, GATE_USED, seq), F32),
                   jax.ShapeDtypeStruct((batch * seq, FOX_W), BF16)],
        scratch_shapes=[pltpu.VMEM((SUBLANES, LANES), F32)],
        compiler_params=_params("parallel", "arbitrary"),
        name="gates")(gpre, bias, _fox_bias_placement())


def _fox_lane_rows():
    r = np.zeros((SUBLANES, LANES), np.float32)
    r[0, :FOX_HD] = 1.0
    r[1, FOX_HD:] = 1.0
    r[2, FOX_HD:FOX_HD + 3] = 1.0
    r[3, 0:3] = 1.0
    r[4, FOX_HD] = 1.0
    r[5, 0] = 1.0
    return jnp.asarray(r, BF16)


def _fox_kernel(q_ref, k_ref, v_ref, cb_ref, mk_ref, o_ref, m_ref, acc_ref, *, tq, tk):
    i = pl.program_id(2)
    n_sub = tq // tk
    npp = q_ref.shape[1] // LANES
    lo, hi = mk_ref[0:1, :], mk_ref[1:2, :]
    one_a, one_b = mk_ref[4:5, :], mk_ref[5:6, :]
    qh = []
    for pp in range(npp):
        q2 = q_ref[:, pp * LANES:(pp + 1) * LANES]
        qh += [q2 * lo + mk_ref[2:3, :], q2 * hi + mk_ref[3:4, :]]

    m_ref[...] = jnp.full_like(m_ref, -jnp.inf)
    acc_ref[...] = jnp.zeros_like(acc_ref)
    tri = (lax.broadcasted_iota(jnp.int32, (tk, tk), 1)
           <= lax.broadcasted_iota(jnp.int32, (tk, tk), 0))

    def step(j, row0, masked, tk=tk):
        rows = tq - row0
        r0 = pl.multiple_of(j * tk, tk)
        for pp in range(npp):
            ps = slice(pp * LANES, (pp + 1) * LANES)
            ks = k_ref[pl.ds(r0, tk), ps]
            vs = v_ref[pl.ds(r0, tk), ps]
            cb = cb_ref[pl.ds(r0, tk), ps]
            kh = (ks * lo + cb * hi, ks * hi + cb * lo)
            vh = (vs * lo + one_a, vs * hi + one_b)
            for h in range(2):
                hh = 2 * pp + h
                s = lax.dot_general(qh[hh][row0:, :], kh[h], NT_DIMS,
                                    preferred_element_type=F32)
                if masked:
                    top = jnp.where(tri, s[:tk], -jnp.inf)
                    s = top if rows == tk else jnp.concatenate([top, s[tk:]], axis=0)
                m_old = m_ref[hh, row0:, :]
                m_new = jnp.maximum(m_old, jnp.max(s, axis=-1, keepdims=True))
                alpha = jnp.exp2(m_old - m_new)
                pe = jnp.exp2(s - jnp.concatenate([m_new] * (tk // LANES), axis=1))
                acc_ref[hh, row0:, :] = alpha * acc_ref[hh, row0:, :] + jnp.dot(
                    pe.astype(BF16), vh[h], preferred_element_type=F32)
                m_ref[hh, row0:, :] = m_new

    tkf = 2 * tk

    def body(jj, carry):
        for u in range(tq // tkf):
            step(jj * (tq // tkf) + u, 0, False, tkf)
        return carry

    lax.fori_loop(0, i, body, 0)
    for d in range(n_sub):
        step(i * n_sub + d, d * tk, True)

    first = lax.broadcasted_iota(jnp.int32, (tq, LANES), 1) < FOX_HD
    for pp in range(npp):
        acc_a, acc_b = acc_ref[2 * pp], acc_ref[2 * pp + 1]
        l_a = jnp.broadcast_to(acc_a[:, FOX_HD:FOX_HD + 1], (tq, LANES))
        l_b = jnp.broadcast_to(acc_b[:, 0:1], (tq, LANES))
        o_ref[:, pp * LANES:(pp + 1) * LANES] = (
            jnp.where(first, acc_a, acc_b) / jnp.where(first, l_a, l_b)).astype(BF16)


def _fox(proj, cb, *, batch, seq, tq, tk, npp):
    nq = seq // tq
    ngrp = FOX_HEADS // 2 // npp
    w = npp * LANES
    qb, kb, vb = FQ_OFF // w, FK_OFF // w, FV_OFF // w
    return pl.pallas_call(
        functools.partial(_fox_kernel, tq=tq, tk=tk),
        grid=(batch, ngrp, nq),
        in_specs=[
            pl.BlockSpec((tq, w), lambda b, p, i: (b * nq + i, qb + p)),
            pl.BlockSpec((seq, w), lambda b, p, i: (b, kb + p)),
            pl.BlockSpec((seq, w), lambda b, p, i: (b, vb + p)),
            pl.BlockSpec((seq, w), lambda b, p, i: (b, p)),
            pl.BlockSpec((SUBLANES, LANES), lambda b, p, i: (0, 0)),
        ],
        out_specs=pl.BlockSpec((tq, w), lambda b, p, i: (b * nq + i, p)),
        out_shape=jax.ShapeDtypeStruct((batch * seq, FOX_W), BF16),
        scratch_shapes=[pltpu.VMEM((2 * npp, tq, LANES), F32),
                        pltpu.VMEM((2 * npp, tq, LANES), F32)],
        compiler_params=_params("parallel", "parallel", "arbitrary"),
        name="fox_attention")(proj, proj, proj, cb, _fox_lane_rows())


def _mlstm_kernel(u_ref, v_ref, mo_ref, r_ref, ct_ref, sel_ref, cw_ref, cb_ref,
                  nw_ref, o_ref, ubuf, q_s, k_s, st_ref, m_ref, *, tt):
    s_idx = pl.program_id(1)
    halo = SUBLANES

    @pl.when(s_idx == 0)
    def _():
        ubuf[0:halo, :] = jnp.zeros((halo, 2 * ML_W), F32)
        st_ref[...] = jnp.zeros_like(st_ref)
        m_ref[...] = jnp.zeros_like(m_ref)

    @pl.when(s_idx > 0)
    def _():
        ubuf[0:halo, :] = ubuf[tt:tt + halo, :]

    c = ML_CHUNK
    for blk in range(tt // c):
        ubuf[halo + blk * c:halo + (blk + 1) * c, :] = (
            u_ref[blk * c:(blk + 1) * c, :].astype(F32))

    def conv_block(blk):
        acc = cb_ref[...]
        for j in (CONV_K - 1,) + tuple(range(CONV_K - 1)):
            off = halo - (CONV_K - 1) + j + blk * c
            acc = acc + cw_ref[j:j + 1, :] * ubuf[off:off + c, :]
        qk = acc * jax.nn.sigmoid(acc)
        q_s[blk * c:(blk + 1) * c, :] = qk[:, :ML_W].astype(BF16)
        k_s[blk * c:(blk + 1) * c, :] = (qk[:, ML_W:] * (ML_HD ** -0.5)).astype(BF16)

    row = lax.broadcasted_iota(jnp.int32, (c, c), 0)
    col = lax.broadcasted_iota(jnp.int32, (c, c), 1)
    causal = col <= row
    ones_blk = jnp.ones((c, ML_HD), BF16)

    def chunk(ci):
        r0 = ci * c
        cb = ct_ref[0, :, pl.ds(r0, c)]
        rep = []
        for qi in range(3):
            hi, mid, lo = _split3(r_ref[pl.ds(r0, c), qi * LANES:(qi + 1) * LANES])
            rep.append(jnp.dot(jnp.concatenate([hi, mid], axis=1), sel_ref[...],
                               preferred_element_type=F32)
                       + jnp.dot(lo, sel_ref[0:LANES, :], preferred_element_type=F32))
        for h in range(ML_HEADS):
            hs = slice(h * ML_HD, (h + 1) * ML_HD)
            qh = q_s[pl.ds(r0, c), hs]
            kh = k_s[pl.ds(r0, c), hs]
            vh = v_ref[pl.ds(r0, c), hs]
            f_t, b_t, cm_t = rep[0][:, hs], rep[1][:, hs], rep[2][:, hs]
            b_row = cb[GATE_MF + h:GATE_MF + h + 1, :]
            m_prev = m_ref[h:h + 1, :]

            g_t = jnp.maximum(m_prev, cm_t)
            dw = jnp.exp2(jnp.where(causal, b_row - jnp.concatenate([g_t, g_t], axis=1),
                                    -jnp.inf))
            inter_w = jnp.exp2(m_prev - g_t)
            sqk = lax.dot_general(qh, kh, NT_DIMS, preferred_element_type=F32) * dw
            state = st_ref[h]
            qc = jnp.dot(qh, state.astype(BF16), preferred_element_type=F32)
            v_aug = jnp.concatenate([vh, ones_blk], axis=1)
            pv = jnp.dot(sqk.astype(BF16), v_aug, preferred_element_type=F32)
            num = inter_w * qc[:, :ML_HD] + pv[:, :ML_HD]
            den = inter_w * qc[:, ML_HD:] + pv[:, ML_HD:]
            hh = num / jnp.maximum(jnp.abs(den), jnp.exp2(-(f_t + g_t)))
            ms = jnp.mean(hh * hh, axis=-1, keepdims=True)
            y = (hh * lax.rsqrt(ms + EPS) * nw_ref[:, hs]
                 * jax.nn.sigmoid(mo_ref[pl.ds(r0, c), hs].astype(F32)))
            o_ref[pl.ds(r0, c), hs] = y.astype(BF16)

            f_last = f_t[c - 1:c, :]
            m_new = jnp.maximum(f_last + m_prev, f_last + cm_t[c - 1:c, :])
            s_old = jnp.exp2(f_last + m_prev - m_new)
            kw = (kh.astype(F32) * jnp.exp2(f_last + b_t - m_new)).astype(BF16)
            upd = lax.dot_general(kw, v_aug, TN_DIMS, preferred_element_type=F32)
            st_ref[h] = jnp.concatenate([s_old, s_old], axis=1) * state + upd
            m_ref[h:h + 1, :] = m_new

    for ci in range(tt // c):
        conv_block(ci)
        chunk(ci)


def _mlstm(proj, rows, ct, conv_w, conv_b, norm_w, *, batch, seq, tt):
    ns = seq // tt
    ub = MQK_OFF // (2 * ML_W)
    vb = MV_OFF // ML_W
    ob = MO_OFF // ML_W
    rmap = lambda b, s: (b * ns + s, 0)
    sel = np.zeros((2, LANES, ML_W), np.float32)
    for h in range(ML_HEADS):
        sel[:, GATE_MF + h, h * ML_HD:(h + 1) * ML_HD] = 1.0
    sel = jnp.asarray(sel.reshape(2 * LANES, ML_W), BF16)
    return pl.pallas_call(
        functools.partial(_mlstm_kernel, tt=tt),
        grid=(batch, ns),
        in_specs=[
            pl.BlockSpec((tt, 2 * ML_W), lambda b, s: (b * ns + s, ub)),
            pl.BlockSpec((tt, ML_W), lambda b, s: (b * ns + s, vb)),
            pl.BlockSpec((tt, ML_W), lambda b, s: (b * ns + s, ob)),
            pl.BlockSpec((tt, 3 * LANES), rmap),
            pl.BlockSpec((1, GATE_USED, tt), lambda b, s: (b, 0, s)),
            pl.BlockSpec((2 * LANES, ML_W), lambda b, s: (0, 0)),
            pl.BlockSpec((CONV_K, 2 * ML_W), lambda b, s: (0, 0)),
            pl.BlockSpec((1, 2 * ML_W), lambda b, s: (0, 0)),
            pl.BlockSpec((1, ML_W), lambda b, s: (0, 0)),
        ],
        out_specs=pl.BlockSpec((tt, ML_W), rmap),
        out_shape=jax.ShapeDtypeStruct((batch * seq, ML_W), BF16),
        scratch_shapes=[pltpu.VMEM((tt + SUBLANES, 2 * ML_W), F32),
                        pltpu.VMEM((tt, ML_W), BF16),
                        pltpu.VMEM((tt, ML_W), BF16),
                        pltpu.VMEM((ML_HEADS, ML_HD, 2 * ML_HD), F32),
                        pltpu.VMEM((SUBLANES, LANES), F32)],
        compiler_params=_params("parallel", "arbitrary"),
        name="mlstm")(proj, proj, proj, rows, ct, sel, conv_w, conv_b, norm_w)


def _mix_xattn_kernel(yf_ref, ym_ref, gt_ref, bg_ref, h_ref, wf_ref, wm_ref, wmix_ref,
                      g_ref, wq_ref, k_ref, v_ref, wo_ref, o_ref):
    g = jax.nn.sigmoid(gt_ref[...].astype(F32) + bg_ref[...])
    uf = jnp.dot(yf_ref[...], wf_ref[...], preferred_element_type=F32)
    um = jnp.dot(ym_ref[...], wm_ref[...], preferred_element_type=F32)
    merged = g[:, :D_MODEL] * uf + g[:, D_MODEL:] * um
    x = h_ref[...] + jnp.dot(merged.astype(BF16), wmix_ref[...], preferred_element_type=F32)

    hn = _rms(x, g_ref[...]).astype(BF16)
    q = jnp.dot(hn, wq_ref[...], preferred_element_type=F32)
    q = (q * (X_HD ** -0.5)).astype(BF16)
    outs = []
    for hd in range(X_HEADS):
        hs = slice(hd * X_HD, (hd + 1) * X_HD)
        s = lax.dot_general(q[:, hs], k_ref[:, hs], NT_DIMS, preferred_element_type=F32)
        e = jnp.exp(s - jnp.max(s, axis=-1, keepdims=True))
        pr = e / jnp.sum(e, axis=-1, keepdims=True)
        outs.append(jnp.dot(pr.astype(BF16), v_ref[:, hs], preferred_element_type=F32))
    o = jnp.concatenate(outs, axis=1).astype(BF16)
    o_ref[...] = x + jnp.dot(o, wo_ref[...], preferred_element_type=F32)


def _mix_xattn(yf, ym, proj, b_gate, h, wf, wm, wmix, gain, wq, kv, wo, *,
               layer, batch, seq, n_mem, tm):
    ns = seq // tm
    const = lambda b, s: (0, 0)
    row = lambda b, s: (b * ns + s, 0)
    return pl.pallas_call(
        _mix_xattn_kernel, grid=(batch, ns),
        in_specs=[
            pl.BlockSpec((tm, FOX_W), row),
            pl.BlockSpec((tm, ML_W), row),
            pl.BlockSpec((tm, 2 * D_MODEL), lambda b, s: (b * ns + s, G_OFF // (2 * D_MODEL))),
            pl.BlockSpec((1, 2 * D_MODEL), const),
            pl.BlockSpec((tm, D_MODEL), row),
            _layer_block((FOX_W, D_MODEL), layer),
            _layer_block((ML_W, D_MODEL), layer),
            _layer_block((D_MODEL, D_MODEL), layer),
            pl.BlockSpec((1, D_MODEL), const),
            _layer_block((D_MODEL, D_MODEL), layer),
            pl.BlockSpec((n_mem, D_MODEL), lambda b, s: (b, 0)),
            pl.BlockSpec((n_mem, D_MODEL), lambda b, s: (b, 1)),
            _layer_block((D_MODEL, D_MODEL), layer),
        ],
        out_specs=pl.BlockSpec((tm, D_MODEL), row),
        out_shape=jax.ShapeDtypeStruct((batch * seq, D_MODEL), F32),
        compiler_params=_params("parallel", "parallel"),
        name="mix_cross_attention")(yf, ym, proj, b_gate, h, wf, wm, wmix,
                                    gain, wq, kv, kv, wo)


def _mlp_kernel(h_ref, g_ref, w1_ref, w2_ref, gf_ref, o_ref, *, tf, final):
    x = h_ref[...]
    hn = _rms(x, g_ref[...]).astype(BF16)
    acc = x
    for j in range(D_FF // tf):
        fs = slice(j * tf, (j + 1) * tf)
        a = jnp.maximum(jnp.dot(hn, w1_ref[:, fs], preferred_element_type=F32), 0.0)
        acc = acc + jnp.dot((a * a).astype(BF16), w2_ref[fs, :], preferred_element_type=F32)
    o_ref[...] = _rms(acc, gf_ref[...]) if final else acc


def _mlp(h, gain, w1, w2, gain_final, *, layer, tm, tf, final):
    m = h.shape[0]
    const = lambda i: (0, 0)
    row = lambda i: (i, 0)
    return pl.pallas_call(
        functools.partial(_mlp_kernel, tf=tf, final=final),
        grid=(m // tm,),
        in_specs=[
            pl.BlockSpec((tm, D_MODEL), row),
            pl.BlockSpec((1, D_MODEL), const),
            _layer_block((D_MODEL, D_FF), layer),
            _layer_block((D_FF, D_MODEL), layer),
            pl.BlockSpec((1, D_MODEL), const),
        ],
        out_specs=pl.BlockSpec((tm, D_MODEL), row),
        out_shape=jax.ShapeDtypeStruct((m, D_MODEL), F32),
        compiler_params=_params("parallel"),
        name="mlp")(h, gain, w1, w2, gain_final)


def _pick(total, prefs):
    for t in prefs:
        if total % t == 0:
            return t
    raise ValueError(f"no tile in {prefs} divides {total}")


def kernel(x, mem, norm_mix, norm_xattn, norm_mem, norm_mlp, w_in, b_fox_f, b_mlstm_i,
           b_mlstm_f, b_gate, conv_w, conv_b, mlstm_norm, w_up_fox, w_up_mlstm, w_mix_out,
           w_xq, w_xkv, w_xo, w_mlp_in, w_mlp_out, norm_final):
    batch, seq, d = x.shape
    n_mem = mem.shape[1]
    depth = w_in.shape[0]
    assert d == D_MODEL and seq % ML_CHUNK == 0
    m = batch * seq

    tm_proj = _pick(m, (512, 256))
    tn_proj = 512
    t_seq = _pick(seq, (1024, 512, 256))
    tk = _pick(seq, (512, 256))
    tq = _pick(seq, (4 * tk, 2 * tk, tk))
    tm_x = _pick(seq, (512, 256))
    tm_mlp = _pick(m, (512, 256))
    tf = 1024

    sp = [0, FOX_W, 2 * FOX_W, 3 * FOX_W, 3 * FOX_W + FOX_HEADS]
    sp += [sp[-1] + 2 * ML_W, sp[-1] + 3 * ML_W, sp[-1] + 4 * ML_W]
    sp += [sp[-1] + ML_HEADS, sp[-1] + 2 * ML_HEADS, sp[-1] + 2 * ML_HEADS + 2 * D_MODEL]
    w_in_b = w_in.astype(BF16)
    seg = [w_in_b[:, :, a:b] for a, b in zip(sp[:-1], sp[1:])]
    fq, fk, fv, ff, mqk, mv, mo, mi, mf, gp = seg
    fq = (w_in[:, :, sp[0]:sp[1]] * (FOX_HD ** -0.5 * LOG2E)).astype(BF16)
    w_main = jnp.concatenate([gp, mqk, fq, fk, fv, mv, mo], axis=2)
    w_gate = jnp.concatenate(
        [ff, mi, mf, jnp.zeros((depth, d, LANES - GATE_USED), BF16)], axis=2)
    gate_bias = jnp.concatenate(
        [b_fox_f, b_mlstm_i, b_mlstm_f, jnp.zeros((depth, LANES - GATE_USED), F32)], axis=1)
    bf = lambda w: w.astype(BF16)
    w_up_fox, w_up_mlstm, w_mix_out = bf(w_up_fox), bf(w_up_mlstm), bf(w_mix_out)
    w_xq, w_xkv, w_xo = bf(w_xq), bf(w_xkv), bf(w_xo)
    w_mlp_in, w_mlp_out = bf(w_mlp_in), bf(w_mlp_out)

    h = x.reshape(m, d)
    mem2 = mem.reshape(batch * n_mem, d)
    for l in range(depth):
        proj, gpre = _in_proj(h, norm_mix[l][None], w_main, w_gate,
                              layer=l, tm=tm_proj, tn=tn_proj)
        rows, ct, cb = _gates(gpre, gate_bias[l][None], batch=batch, seq=seq, tg=t_seq)
        y_fox = _fox(proj, cb, batch=batch, seq=seq, tq=tq, tk=tk, npp=1)
        y_ml = _mlstm(proj, rows, ct, conv_w[l], conv_b[l][None], mlstm_norm[l][None],
                      batch=batch, seq=seq, tt=t_seq)
        kv = _norm_proj(mem2, norm_mem[l][None], w_xkv,
                        layer=l, tm=_pick(batch * n_mem, (1024, 512, 256)), tn=1024)
        h = _mix_xattn(y_fox, y_ml, proj, b_gate[l][None], h,
                       w_up_fox, w_up_mlstm, w_mix_out,
                       norm_xattn[l][None], w_xq, kv, w_xo,
                       layer=l, batch=batch, seq=seq, n_mem=n_mem, tm=tm_x)
        h = _mlp(h, norm_mlp[l][None], w_mlp_in, w_mlp_out, norm_final[None],
                 layer=l, tm=tm_mlp, tf=tf, final=(l == depth - 1))
    return h.reshape(batch, seq, d)
```

```python
import functools

import jax
import jax.numpy as jnp
import numpy as np
from jax import lax
from jax.experimental import pallas as pl
from jax.experimental.pallas import tpu as pltpu

F32 = jnp.float32
BF16 = jnp.bfloat16

D_MODEL = 1024
FOX_HEADS = 8
FOX_HD = 64
FOX_W = FOX_HEADS * FOX_HD
ML_HEADS = 4
ML_HD = 128
ML_W = ML_HEADS * ML_HD
CONV_K = 4
X_HEADS = 4
X_HD = D_MODEL // X_HEADS
D_FF = 4 * D_MODEL
EPS = 1e-6
LOG2E = 1.4426950408889634

LANES = 128
SUBLANES = 8
V7X_VMEM_LIMIT_BYTES = 56 * 1024 * 1024

G_OFF = 0
MQK_OFF = G_OFF + 2 * D_MODEL
FQ_OFF = MQK_OFF + 2 * ML_W
FK_OFF = FQ_OFF + FOX_W
FV_OFF = FK_OFF + FOX_W
MV_OFF = FV_OFF + FOX_W
MO_OFF = MV_OFF + ML_W
PROJ_W = MO_OFF + ML_W
GATE_FF = 0
GATE_MI = GATE_FF + FOX_HEADS
GATE_MF = GATE_MI + ML_HEADS
GATE_USED = GATE_MF + ML_HEADS

ML_CHUNK = 256
NT_DIMS = (((1,), (1,)), ((), ()))
TN_DIMS = (((0,), (0,)), ((), ()))


def _params(*sem):
    return pltpu.CompilerParams(dimension_semantics=sem,
                                vmem_limit_bytes=V7X_VMEM_LIMIT_BYTES)


def _rms(x, g):
    return x * lax.rsqrt(jnp.mean(x * x, axis=-1, keepdims=True) + EPS) * g


def _rms_split(x, g):
    r = lax.rsqrt(jnp.mean(x * x, axis=-1, keepdims=True) + EPS)
    return (x * g).astype(BF16), r


def _layer_block(shape, layer):
    zeros = (0,) * len(shape)
    return pl.BlockSpec((None,) + tuple(shape), lambda *_: (layer,) + zeros)


def _in_proj_kernel(x_ref, g_ref, w_ref, wg_ref, o_ref, og_ref, *, tn):
    xg, r = _rms_split(x_ref[...], g_ref[...])
    og_ref[...] = r * jnp.dot(xg, wg_ref[...], preferred_element_type=F32)
    for j in range(w_ref.shape[1] // tn):
        cs = slice(j * tn, (j + 1) * tn)
        o_ref[:, cs] = (r * jnp.dot(xg, w_ref[:, cs], preferred_element_type=F32)).astype(BF16)


def _in_proj(x, gain, w, wg, *, layer, tm, tn):
    m, d = x.shape
    n = w.shape[-1]
    const = lambda i: (0, 0)
    return pl.pallas_call(
        functools.partial(_in_proj_kernel, tn=tn), grid=(m // tm,),
        in_specs=[pl.BlockSpec((tm, d), lambda i: (i, 0)),
                  pl.BlockSpec((1, d), const),
                  _layer_block((d, n), layer),
                  _layer_block((d, LANES), layer)],
        out_specs=[pl.BlockSpec((tm, n), lambda i: (i, 0)),
                   pl.BlockSpec((tm, LANES), lambda i: (i, 0))],
        out_shape=[jax.ShapeDtypeStruct((m, n), BF16),
                   jax.ShapeDtypeStruct((m, LANES), F32)],
        compiler_params=_params("parallel"),
        name="norm_in_proj")(x, gain, w, wg)


def _norm_proj_kernel(x_ref, g_ref, w_ref, o_ref, xn_ref):
    @pl.when(pl.program_id(1) == 0)
    def _():
        xn_ref[...] = _rms(x_ref[...], g_ref[...]).astype(BF16)

    o_ref[...] = jnp.dot(xn_ref[...], w_ref[...],
                         preferred_element_type=F32).astype(BF16)


def _norm_proj(x, gain, w, *, layer, tm, tn):
    m, d = x.shape
    n = w.shape[-1]
    return pl.pallas_call(
        _norm_proj_kernel, grid=(m // tm, n // tn),
        in_specs=[pl.BlockSpec((tm, d), lambda i, j: (i, 0)),
                  pl.BlockSpec((1, d), lambda i, j: (0, 0)),
                  pl.BlockSpec((None, d, tn), lambda i, j: (layer, 0, j))],
        out_specs=pl.BlockSpec((tm, tn), lambda i, j: (i, j)),
        out_shape=jax.ShapeDtypeStruct((m, n), BF16),
        scratch_shapes=[pltpu.VMEM((tm, d), BF16)],
        compiler_params=_params("parallel", "arbitrary"),
        name="norm_proj")(x, gain, w)


def _split3(v):
    hi = v.astype(BF16)
    r1 = v - hi.astype(F32)
    mid = r1.astype(BF16)
    lo = (r1 - mid.astype(F32)).astype(BF16)
    return hi, mid, lo


def _gates_kernel(x_ref, b_ref, pm_ref, r_ref, ct_ref, cb_ref, carry_ref, *, tg):
    @pl.when(pl.program_id(1) == 0)
    def _():
        carry_ref[...] = jnp.zeros_like(carry_ref)

    c = ML_CHUNK
    row = lax.broadcasted_iota(jnp.int32, (c, c), 0)
    col = lax.broadcasted_iota(jnp.int32, (c, c), 1)
    causal = row >= col
    tri = jnp.where(causal, 1.0, 0.0).astype(BF16)
    lane = lax.broadcasted_iota(jnp.int32, (c, LANES), 1)
    is_fox = lane < GATE_MI
    is_ig = (lane >= GATE_MI) & (lane < GATE_MF)
    is_used = lane < GATE_USED
    bias = b_ref[...]
    for blk in range(tg // c):
        rs = slice(blk * c, (blk + 1) * c)
        x = x_ref[rs, :] + bias
        logsig = jnp.minimum(x, 0.0) - jnp.log1p(jnp.exp(-jnp.abs(x)))
        val = jnp.where(is_ig, x, logsig)
        val = jnp.where(is_used, val, 0.0)
        csp = jnp.dot(tri, jnp.concatenate(_split3(val), axis=1), preferred_element_type=F32)
        cs = csp[:, 0:LANES] + csp[:, LANES:2 * LANES] + csp[:, 2 * LANES:3 * LANES]
        glob = cs + carry_ref[0:1, :]
        carry_ref[0:1, :] = glob[c - 1:c, :]
        f_t = cs * LOG2E
        b_t = (pltpu.roll(val, GATE_MF - GATE_MI, axis=1) - cs) * LOG2E
        bt = b_t.T
        ct_ref[0, :, rs] = bt[:GATE_USED, :]
        cm = jnp.zeros((c, LANES), F32)
        for h in range(ML_HEADS):
            k = GATE_MF + h
            run_max = jnp.max(jnp.where(causal, bt[k:k + 1, :], -jnp.inf),
                              axis=-1, keepdims=True)
            cm = jnp.where(lane == k, run_max, cm)
        r_ref[rs, 0:LANES] = f_t
        r_ref[rs, LANES:2 * LANES] = b_t
        r_ref[rs, 2 * LANES:3 * LANES] = cm
        terms = _split3(jnp.where(is_fox, glob * (-LOG2E), 0.0))
        cb_ref[rs, :] = jnp.dot(jnp.concatenate(terms, axis=1), pm_ref[...],
                                preferred_element_type=F32).astype(BF16)


def _fox_bias_placement():
    pm = np.zeros((3, LANES, FOX_W), np.float32)
    for h in range(FOX_HEADS):
        base = (h // 2) * LANES + (FOX_HD if h % 2 == 0 else 0)
        for t in range(3):
            pm[t, h, base + t] = 1.0
    return jnp.asarray(pm.reshape(3 * LANES, FOX_W), BF16)


def _gates(gpre, bias, *, batch, seq, tg):
    ns = seq // tg
    rmap = lambda b, s: (b * ns + s, 0)
    return pl.pallas_call(
        functools.partial(_gates_kernel, tg=tg),
        grid=(batch, ns),
        in_specs=[pl.BlockSpec((tg, LANES), rmap),
                  pl.BlockSpec((1, LANES), lambda b, s: (0, 0)),
                  pl.BlockSpec((3 * LANES, FOX_W), lambda b, s: (0, 0))],
        out_specs=[pl.BlockSpec((tg, 3 * LANES), rmap),
                   pl.BlockSpec((1, GATE_USED, tg), lambda b, s: (b, 0, s)),
                   pl.BlockSpec((tg, FOX_W), rmap)],
        out_shape=[jax.ShapeDtypeStruct((batch * seq, 3 * LANES), F32),
                   jax.ShapeDtypeStruct((batch, GATE_USED, seq), F32),
                   jax.ShapeDtypeStruct((batch * seq, FOX_W), BF16)],
        scratch_shapes=[pltpu.VMEM((SUBLANES, LANES), F32)],
        compiler_params=_params("parallel", "arbitrary"),
        name="gates")(gpre, bias, _fox_bias_placement())


def _fox_lane_rows():
    r = np.zeros((SUBLANES, LANES), np.float32)
    r[0, :FOX_HD] = 1.0
    r[1, FOX_HD:] = 1.0
    r[2, FOX_HD:FOX_HD + 3] = 1.0
    r[3, 0:3] = 1.0
    r[4, FOX_HD] = 1.0
    r[5, 0] = 1.0
    return jnp.asarray(r, BF16)


def _fox_kernel(q_ref, k_ref, v_ref, cb_ref, mk_ref, o_ref, m_ref, acc_ref, *, tq, tk):
    i = pl.program_id(2)
    n_sub = tq // tk
    npp = q_ref.shape[1] // LANES
    lo, hi = mk_ref[0:1, :], mk_ref[1:2, :]
    one_a, one_b = mk_ref[4:5, :], mk_ref[5:6, :]
    qh = []
    for pp in range(npp):
        q2 = q_ref[:, pp * LANES:(pp + 1) * LANES]
        qh += [q2 * lo + mk_ref[2:3, :], q2 * hi + mk_ref[3:4, :]]

    m_ref[...] = jnp.full_like(m_ref, -jnp.inf)
    acc_ref[...] = jnp.zeros_like(acc_ref)
    tri = (lax.broadcasted_iota(jnp.int32, (tk, tk), 1)
           <= lax.broadcasted_iota(jnp.int32, (tk, tk), 0))

    def step(j, row0, masked, tk=tk):
        rows = tq - row0
        r0 = pl.multiple_of(j * tk, tk)
        for pp in range(npp):
            ps = slice(pp * LANES, (pp + 1) * LANES)
            ks = k_ref[pl.ds(r0, tk), ps]
            vs = v_ref[pl.ds(r0, tk), ps]
            cb = cb_ref[pl.ds(r0, tk), ps]
            kh = (ks * lo + cb * hi, ks * hi + cb * lo)
            vh = (vs * lo + one_a, vs * hi + one_b)
            for h in range(2):
                hh = 2 * pp + h
                s = lax.dot_general(qh[hh][row0:, :], kh[h], NT_DIMS,
                                    preferred_element_type=F32)
                if masked:
                    top = jnp.where(tri, s[:tk], -jnp.inf)
                    s = top if rows == tk else jnp.concatenate([top, s[tk:]], axis=0)
                m_old = m_ref[hh, row0:, :]
                m_new = jnp.maximum(m_old, jnp.max(s, axis=-1, keepdims=True))
                alpha = jnp.exp2(m_old - m_new)
                pe = jnp.exp2(s - jnp.concatenate([m_new] * (tk // LANES), axis=1))
                acc_ref[hh, row0:, :] = alpha * acc_ref[hh, row0:, :] + jnp.dot(
                    pe.astype(BF16), vh[h], preferred_element_type=F32)
                m_ref[hh, row0:, :] = m_new

    tkf = 2 * tk

    def body(jj, carry):
        for u in range(tq // tkf):
            step(jj * (tq // tkf) + u, 0, False, tkf)
        return carry

    lax.fori_loop(0, i, body, 0)
    for d in range(n_sub):
        step(i * n_sub + d, d * tk, True)

    first = lax.broadcasted_iota(jnp.int32, (tq, LANES), 1) < FOX_HD
    for pp in range(npp):
        acc_a, acc_b = acc_ref[2 * pp], acc_ref[2 * pp + 1]
        l_a = jnp.broadcast_to(acc_a[:, FOX_HD:FOX_HD + 1], (tq, LANES))
        l_b = jnp.broadcast_to(acc_b[:, 0:1], (tq, LANES))
        o_ref[:, pp * LANES:(pp + 1) * LANES] = (
            jnp.where(first, acc_a, acc_b) / jnp.where(first, l_a, l_b)).astype(BF16)


def _fox(proj, cb, *, batch, seq, tq, tk, npp):
    nq = seq // tq
    ngrp = FOX_HEADS // 2 // npp
    w = npp * LANES
    qb, kb, vb = FQ_OFF // w, FK_OFF // w, FV_OFF // w
    return pl.pallas_call(
        functools.partial(_fox_kernel, tq=tq, tk=tk),
        grid=(batch, ngrp, nq),
        in_specs=[
            pl.BlockSpec((tq, w), lambda b, p, i: (b * nq + i, qb + p)),
            pl.BlockSpec((seq, w), lambda b, p, i: (b, kb + p)),
            pl.BlockSpec((seq, w), lambda b, p, i: (b, vb + p)),
            pl.BlockSpec((seq, w), lambda b, p, i: (b, p)),
            pl.BlockSpec((SUBLANES, LANES), lambda b, p, i: (0, 0)),
        ],
        out_specs=pl.BlockSpec((tq, w), lambda b, p, i: (b * nq + i, p)),
        out_shape=jax.ShapeDtypeStruct((batch * seq, FOX_W), BF16),
        scratch_shapes=[pltpu.VMEM((2 * npp, tq, LANES), F32),
                        pltpu.VMEM((2 * npp, tq, LANES), F32)],
        compiler_params=_params("parallel", "parallel", "arbitrary"),
        name="fox_attention")(proj, proj, proj, cb, _fox_lane_rows())


def _mlstm_kernel(u_ref, v_ref, mo_ref, r_ref, ct_ref, sel_ref, cw_ref, cb_ref,
                  nw_ref, o_ref, ubuf, q_s, k_s, st_ref, m_ref, *, tt):
    s_idx = pl.program_id(1)
    halo = SUBLANES

    @pl.when(s_idx == 0)
    def _():
        ubuf[0:halo, :] = jnp.zeros((halo, 2 * ML_W), F32)
        st_ref[...] = jnp.zeros_like(st_ref)
        m_ref[...] = jnp.zeros_like(m_ref)

    @pl.when(s_idx > 0)
    def _():
        ubuf[0:halo, :] = ubuf[tt:tt + halo, :]

    c = ML_CHUNK
    for blk in range(tt // c):
        ubuf[halo + blk * c:halo + (blk + 1) * c, :] = (
            u_ref[blk * c:(blk + 1) * c, :].astype(F32))

    def conv_block(blk):
        acc = cb_ref[...]
        for j in (CONV_K - 1,) + tuple(range(CONV_K - 1)):
            off = halo - (CONV_K - 1) + j + blk * c
            acc = acc + cw_ref[j:j + 1, :] * ubuf[off:off + c, :]
        qk = acc * jax.nn.sigmoid(acc)
        q_s[blk * c:(blk + 1) * c, :] = qk[:, :ML_W].astype(BF16)
        k_s[blk * c:(blk + 1) * c, :] = (qk[:, ML_W:] * (ML_HD ** -0.5)).astype(BF16)

    row = lax.broadcasted_iota(jnp.int32, (c, c), 0)
    col = lax.broadcasted_iota(jnp.int32, (c, c), 1)
    causal = col <= row
    ones_blk = jnp.ones((c, ML_HD), BF16)

    def chunk(ci):
        r0 = ci * c
        cb = ct_ref[0, :, pl.ds(r0, c)]
        rep = []
        for qi in range(3):
            hi, mid, lo = _split3(r_ref[pl.ds(r0, c), qi * LANES:(qi + 1) * LANES])
            rep.append(jnp.dot(jnp.concatenate([hi, mid], axis=1), sel_ref[...],
                               preferred_element_type=F32)
                       + jnp.dot(lo, sel_ref[0:LANES, :], preferred_element_type=F32))
        for h in range(ML_HEADS):
            hs = slice(h * ML_HD, (h + 1) * ML_HD)
            qh = q_s[pl.ds(r0, c), hs]
            kh = k_s[pl.ds(r0, c), hs]
            vh = v_ref[pl.ds(r0, c), hs]
            f_t, b_t, cm_t = rep[0][:, hs], rep[1][:, hs], rep[2][:, hs]
            b_row = cb[GATE_MF + h:GATE_MF + h + 1, :]
            m_prev = m_ref[h:h + 1, :]

            g_t = jnp.maximum(m_prev, cm_t)
            dw = jnp.exp2(jnp.where(causal, b_row - jnp.concatenate([g_t, g_t], axis=1),
                                    -jnp.inf))
            inter_w = jnp.exp2(m_prev - g_t)
            sqk = lax.dot_general(qh, kh, NT_DIMS, preferred_element_type=F32) * dw
            state = st_ref[h]
            qc = jnp.dot(qh, state.astype(BF16), preferred_element_type=F32)
            v_aug = jnp.concatenate([vh, ones_blk], axis=1)
            pv = jnp.dot(sqk.astype(BF16), v_aug, preferred_element_type=F32)
            num = inter_w * qc[:, :ML_HD] + pv[:, :ML_HD]
            den = inter_w * qc[:, ML_HD:] + pv[:, ML_HD:]
            hh = num / jnp.maximum(jnp.abs(den), jnp.exp2(-(f_t + g_t)))
            ms = jnp.mean(hh * hh, axis=-1, keepdims=True)
            y = (hh * lax.rsqrt(ms + EPS) * nw_ref[:, hs]
                 * jax.nn.sigmoid(mo_ref[pl.ds(r0, c), hs].astype(F32)))
            o_ref[pl.ds(r0, c), hs] = y.astype(BF16)

            f_last = f_t[c - 1:c, :]
            m_new = jnp.maximum(f_last + m_prev, f_last + cm_t[c - 1:c, :])
            s_old = jnp.exp2(f_last + m_prev - m_new)
            kw = (kh.astype(F32) * jnp.exp2(f_last + b_t - m_new)).astype(BF16)
            upd = lax.dot_general(kw, v_aug, TN_DIMS, preferred_element_type=F32)
            st_ref[h] = jnp.concatenate([s_old, s_old], axis=1) * state + upd
            m_ref[h:h + 1, :] = m_new

    for ci in range(tt // c):
        conv_block(ci)
        chunk(ci)


def _mlstm(proj, rows, ct, conv_w, conv_b, norm_w, *, batch, seq, tt):
    ns = seq // tt
    ub = MQK_OFF // (2 * ML_W)
    vb = MV_OFF // ML_W
    ob = MO_OFF // ML_W
    rmap = lambda b, s: (b * ns + s, 0)
    sel = np.zeros((2, LANES, ML_W), np.float32)
    for h in range(ML_HEADS):
        sel[:, GATE_MF + h, h * ML_HD:(h + 1) * ML_HD] = 1.0
    sel = jnp.asarray(sel.reshape(2 * LANES, ML_W), BF16)
    return pl.pallas_call(
        functools.partial(_mlstm_kernel, tt=tt),
        grid=(batch, ns),
        in_specs=[
            pl.BlockSpec((tt, 2 * ML_W), lambda b, s: (b * ns + s, ub)),
            pl.BlockSpec((tt, ML_W), lambda b, s: (b * ns + s, vb)),
            pl.BlockSpec((tt, ML_W), lambda b, s: (b * ns + s, ob)),
            pl.BlockSpec((tt, 3 * LANES), rmap),
            pl.BlockSpec((1, GATE_USED, tt), lambda b, s: (b, 0, s)),
            pl.BlockSpec((2 * LANES, ML_W), lambda b, s: (0, 0)),
            pl.BlockSpec((CONV_K, 2 * ML_W), lambda b, s: (0, 0)),
            pl.BlockSpec((1, 2 * ML_W), lambda b, s: (0, 0)),
            pl.BlockSpec((1, ML_W), lambda b, s: (0, 0)),
        ],
        out_specs=pl.BlockSpec((tt, ML_W), rmap),
        out_shape=jax.ShapeDtypeStruct((batch * seq, ML_W), BF16),
        scratch_shapes=[pltpu.VMEM((tt + SUBLANES, 2 * ML_W), F32),
                        pltpu.VMEM((tt, ML_W), BF16),
                        pltpu.VMEM((tt, ML_W), BF16),
                        pltpu.VMEM((ML_HEADS, ML_HD, 2 * ML_HD), F32),
                        pltpu.VMEM((SUBLANES, LANES), F32)],
        compiler_params=_params("parallel", "arbitrary"),
        name="mlstm")(proj, proj, proj, rows, ct, sel, conv_w, conv_b, norm_w)


def _mix_xattn_kernel(yf_ref, ym_ref, gt_ref, bg_ref, h_ref, wf_ref, wm_ref, wmix_ref,
                      g_ref, wq_ref, k_ref, v_ref, wo_ref, o_ref):
    g = jax.nn.sigmoid(gt_ref[...].astype(F32) + bg_ref[...])
    uf = jnp.dot(yf_ref[...], wf_ref[...], preferred_element_type=F32)
    um = jnp.dot(ym_ref[...], wm_ref[...], preferred_element_type=F32)
    merged = g[:, :D_MODEL] * uf + g[:, D_MODEL:] * um
    x = h_ref[...] + jnp.dot(merged.astype(BF16), wmix_ref[...], preferred_element_type=F32)

    xg, r = _rms_split(x, g_ref[...])
    q = jnp.dot(xg, wq_ref[...], preferred_element_type=F32)
    q = (q * (r * (X_HD ** -0.5))).astype(BF16)
    outs = []
    for hd in range(X_HEADS):
        hs = slice(hd * X_HD, (hd + 1) * X_HD)
        s = lax.dot_general(q[:, hs], k_ref[:, hs], NT_DIMS, preferred_element_type=F32)
        e = jnp.exp(s - jnp.max(s, axis=-1, keepdims=True))
        pr = e / jnp.sum(e, axis=-1, keepdims=True)
        outs.append(jnp.dot(pr.astype(BF16), v_ref[:, hs], preferred_element_type=F32))
    o = jnp.concatenate(outs, axis=1).astype(BF16)
    o_ref[...] = x + jnp.dot(o, wo_ref[...], preferred_element_type=F32)


def _mix_xattn(yf, ym, proj, b_gate, h, wf, wm, wmix, gain, wq, kv, wo, *,
               layer, batch, seq, n_mem, tm):
    ns = seq // tm
    const = lambda b, s: (0, 0)
    row = lambda b, s: (b * ns + s, 0)
    return pl.pallas_call(
        _mix_xattn_kernel, grid=(batch, ns),
        in_specs=[
            pl.BlockSpec((tm, FOX_W), row),
            pl.BlockSpec((tm, ML_W), row),
            pl.BlockSpec((tm, 2 * D_MODEL), lambda b, s: (b * ns + s, G_OFF // (2 * D_MODEL))),
            pl.BlockSpec((1, 2 * D_MODEL), const),
            pl.BlockSpec((tm, D_MODEL), row),
            _layer_block((FOX_W, D_MODEL), layer),
            _layer_block((ML_W, D_MODEL), layer),
            _layer_block((D_MODEL, D_MODEL), layer),
            pl.BlockSpec((1, D_MODEL), const),
            _layer_block((D_MODEL, D_MODEL), layer),
            pl.BlockSpec((n_mem, D_MODEL), lambda b, s: (b, 0)),
            pl.BlockSpec((n_mem, D_MODEL), lambda b, s: (b, 1)),
            _layer_block((D_MODEL, D_MODEL), layer),
        ],
        out_specs=pl.BlockSpec((tm, D_MODEL), row),
        out_shape=jax.ShapeDtypeStruct((batch * seq, D_MODEL), F32),
        compiler_params=_params("parallel", "parallel"),
        name="mix_cross_attention")(yf, ym, proj, b_gate, h, wf, wm, wmix,
                                    gain, wq, kv, kv, wo)


def _mlp_kernel(h_ref, g_ref, w1_ref, w2_ref, gf_ref, o_ref, *, tf, final):
    x = h_ref[...]
    xg, r = _rms_split(x, g_ref[...])
    acc = x
    for j in range(D_FF // tf):
        fs = slice(j * tf, (j + 1) * tf)
        a = jnp.maximum(r * jnp.dot(xg, w1_ref[:, fs], preferred_element_type=F32), 0.0)
        acc = acc + jnp.dot((a * a).astype(BF16), w2_ref[fs, :], preferred_element_type=F32)
    o_ref[...] = _rms(acc, gf_ref[...]) if final else acc


def _mlp(h, gain, w1, w2, gain_final, *, layer, tm, tf, final):
    m = h.shape[0]
    const = lambda i: (0, 0)
    row = lambda i: (i, 0)
    return pl.pallas_call(
        functools.partial(_mlp_kernel, tf=tf, final=final),
        grid=(m // tm,),
        in_specs=[
            pl.BlockSpec((tm, D_MODEL), row),
            pl.BlockSpec((1, D_MODEL), const),
            _layer_block((D_MODEL, D_FF), layer),
            _layer_block((D_FF, D_MODEL), layer),
            pl.BlockSpec((1, D_MODEL), const),
        ],
        out_specs=pl.BlockSpec((tm, D_MODEL), row),
        out_shape=jax.ShapeDtypeStruct((m, D_MODEL), F32),
        compiler_params=_params("parallel"),
        name="mlp")(h, gain, w1, w2, gain_final)


def _pick(total, prefs):
    for t in prefs:
        if total % t == 0:
            return t
    raise ValueError(f"no tile in {prefs} divides {total}")


def kernel(x, mem, norm_mix, norm_xattn, norm_mem, norm_mlp, w_in, b_fox_f, b_mlstm_i,
           b_mlstm_f, b_gate, conv_w, conv_b, mlstm_norm, w_up_fox, w_up_mlstm, w_mix_out,
           w_xq, w_xkv, w_xo, w_mlp_in, w_mlp_out, norm_final):
    batch, seq, d = x.shape
    n_mem = mem.shape[1]
    depth = w_in.shape[0]
    assert d == D_MODEL and seq % ML_CHUNK == 0
    m = batch * seq

    tm_proj = _pick(m, (512, 256))
    tn_proj = 512
    t_seq = _pick(seq, (1024, 512, 256))
    tk = _pick(seq, (512, 256))
    tq = _pick(seq, (4 * tk, 2 * tk, tk))
    tm_x = _pick(seq, (512, 256))
    tm_mlp = _pick(m, (512, 256))
    tf = 1024

    sp = [0, FOX_W, 2 * FOX_W, 3 * FOX_W, 3 * FOX_W + FOX_HEADS]
    sp += [sp[-1] + 2 * ML_W, sp[-1] + 3 * ML_W, sp[-1] + 4 * ML_W]
    sp += [sp[-1] + ML_HEADS, sp[-1] + 2 * ML_HEADS, sp[-1] + 2 * ML_HEADS + 2 * D_MODEL]
    w_in_b = w_in.astype(BF16)
    seg = [w_in_b[:, :, a:b] for a, b in zip(sp[:-1], sp[1:])]
    fq, fk, fv, ff, mqk, mv, mo, mi, mf, gp = seg
    fq = (w_in[:, :, sp[0]:sp[1]] * (FOX_HD ** -0.5 * LOG2E)).astype(BF16)
    w_main = jnp.concatenate([gp, mqk, fq, fk, fv, mv, mo], axis=2)
    w_gate = jnp.concatenate(
        [ff, mi, mf, jnp.zeros((depth, d, LANES - GATE_USED), BF16)], axis=2)
    gate_bias = jnp.concatenate(
        [b_fox_f, b_mlstm_i, b_mlstm_f, jnp.zeros((depth, LANES - GATE_USED), F32)], axis=1)
    bf = lambda w: w.astype(BF16)
    w_up_fox, w_up_mlstm, w_mix_out = bf(w_up_fox), bf(w_up_mlstm), bf(w_mix_out)
    w_xq, w_xkv, w_xo = bf(w_xq), bf(w_xkv), bf(w_xo)
    w_mlp_in, w_mlp_out = bf(w_mlp_in), bf(w_mlp_out)

    h = x.reshape(m, d)
    mem2 = mem.reshape(batch * n_mem, d)
    for l in range(depth):
        proj, gpre = _in_proj(h, norm_mix[l][None], w_main, w_gate,
                              layer=l, tm=tm_proj, tn=tn_proj)
        rows, ct, cb = _gates(gpre, gate_bias[l][None], batch=batch, seq=seq, tg=t_seq)
        y_fox = _fox(proj, cb, batch=batch, seq=seq, tq=tq, tk=tk, npp=1)
        y_ml = _mlstm(proj, rows, ct, conv_w[l], conv_b[l][None], mlstm_norm[l][None],
                      batch=batch, seq=seq, tt=t_seq)
        kv = _norm_proj(mem2, norm_mem[l][None], w_xkv,
                        layer=l, tm=_pick(batch * n_mem, (1024, 512, 256)), tn=1024)
        h = _mix_xattn(y_fox, y_ml, proj, b_gate[l][None], h,
                       w_up_fox, w_up_mlstm, w_mix_out,
                       norm_xattn[l][None], w_xq, kv, w_xo,
                       layer=l, batch=batch, seq=seq, n_mem=n_mem, tm=tm_x)
        h = _mlp(h, norm_mlp[l][None], w_mlp_in, w_mlp_out, norm_final[None],
                 layer=l, tm=tm_mlp, tf=tf, final=(l == depth - 1))
    return h.reshape(batch, seq, d)
```

```python
import functools

import jax
import jax.numpy as jnp
import numpy as np
from jax import lax
from jax.experimental import pallas as pl
from jax.experimental.pallas import tpu as pltpu

F32 = jnp.float32
BF16 = jnp.bfloat16

D_MODEL = 1024
FOX_HEADS = 8
FOX_HD = 64
FOX_W = FOX_HEADS * FOX_HD
ML_HEADS = 4
ML_HD = 128
ML_W = ML_HEADS * ML_HD
CONV_K = 4
X_HEADS = 4
X_HD = D_MODEL // X_HEADS
D_FF = 4 * D_MODEL
EPS = 1e-6
LOG2E = 1.4426950408889634

LANES = 128
SUBLANES = 8
V7X_VMEM_LIMIT_BYTES = 56 * 1024 * 1024

G_OFF = 0
MQK_OFF = G_OFF + 2 * D_MODEL
FQ_OFF = MQK_OFF + 2 * ML_W
FK_OFF = FQ_OFF + FOX_W
FV_OFF = FK_OFF + FOX_W
MV_OFF = FV_OFF + FOX_W
MO_OFF = MV_OFF + ML_W
PROJ_W = MO_OFF + ML_W
GATE_FF = 0
GATE_MI = GATE_FF + FOX_HEADS
GATE_MF = GATE_MI + ML_HEADS
GATE_USED = GATE_MF + ML_HEADS

ML_CHUNK = 256
NT_DIMS = (((1,), (1,)), ((), ()))
TN_DIMS = (((0,), (0,)), ((), ()))


def _params(*sem):
    return pltpu.CompilerParams(dimension_semantics=sem,
                                vmem_limit_bytes=V7X_VMEM_LIMIT_BYTES)


def _rms(x, g):
    return x * lax.rsqrt(jnp.mean(x * x, axis=-1, keepdims=True) + EPS) * g


def _rms_split(x, g):
    r = lax.rsqrt(jnp.mean(x * x, axis=-1, keepdims=True) + EPS)
    return (x * g).astype(BF16), r


def _layer_block(shape, layer):
    zeros = (0,) * len(shape)
    return pl.BlockSpec((None,) + tuple(shape), lambda *_: (layer,) + zeros)


def _in_proj_kernel(x_ref, g_ref, w_ref, wg_ref, o_ref, og_ref, *, tn):
    xg, r = _rms_split(x_ref[...], g_ref[...])
    og_ref[...] = r * jnp.dot(xg, wg_ref[...], preferred_element_type=F32)
    for j in range(w_ref.shape[1] // tn):
        cs = slice(j * tn, (j + 1) * tn)
        o_ref[:, cs] = (r * jnp.dot(xg, w_ref[:, cs], preferred_element_type=F32)).astype(BF16)


def _in_proj(x, gain, w, wg, *, layer, tm, tn):
    m, d = x.shape
    n = w.shape[-1]
    const = lambda i: (0, 0)
    return pl.pallas_call(
        functools.partial(_in_proj_kernel, tn=tn), grid=(m // tm,),
        in_specs=[pl.BlockSpec((tm, d), lambda i: (i, 0)),
                  pl.BlockSpec((1, d), const),
                  _layer_block((d, n), layer),
                  _layer_block((d, LANES), layer)],
        out_specs=[pl.BlockSpec((tm, n), lambda i: (i, 0)),
                   pl.BlockSpec((tm, LANES), lambda i: (i, 0))],
        out_shape=[jax.ShapeDtypeStruct((m, n), BF16),
                   jax.ShapeDtypeStruct((m, LANES), F32)],
        compiler_params=_params("parallel"),
        name="norm_in_proj")(x, gain, w, wg)


def _norm_proj_kernel(x_ref, g_ref, w_ref, o_ref, xn_ref):
    @pl.when(pl.program_id(1) == 0)
    def _():
        xn_ref[...] = _rms(x_ref[...], g_ref[...]).astype(BF16)

    o_ref[...] = jnp.dot(xn_ref[...], w_ref[...],
                         preferred_element_type=F32).astype(BF16)


def _norm_proj(x, gain, w, *, layer, tm, tn):
    m, d = x.shape
    n = w.shape[-1]
    return pl.pallas_call(
        _norm_proj_kernel, grid=(m // tm, n // tn),
        in_specs=[pl.BlockSpec((tm, d), lambda i, j: (i, 0)),
                  pl.BlockSpec((1, d), lambda i, j: (0, 0)),
                  pl.BlockSpec((None, d, tn), lambda i, j: (layer, 0, j))],
        out_specs=pl.BlockSpec((tm, tn), lambda i, j: (i, j)),
        out_shape=jax.ShapeDtypeStruct((m, n), BF16),
        scratch_shapes=[pltpu.VMEM((tm, d), BF16)],
        compiler_params=_params("parallel", "arbitrary"),
        name="norm_proj")(x, gain, w)


def _split3(v):
    hi = v.astype(BF16)
    r1 = v - hi.astype(F32)
    mid = r1.astype(BF16)
    lo = (r1 - mid.astype(F32)).astype(BF16)
    return hi, mid, lo


def _gates_kernel(x_ref, b_ref, pm_ref, r_ref, ct_ref, cb_ref, carry_ref, *, tg):
    @pl.when(pl.program_id(1) == 0)
    def _():
        carry_ref[...] = jnp.zeros_like(carry_ref)

    c = ML_CHUNK
    row = lax.broadcasted_iota(jnp.int32, (c, c), 0)
    col = lax.broadcasted_iota(jnp.int32, (c, c), 1)
    causal = row >= col
    tri = jnp.where(causal, 1.0, 0.0).astype(BF16)
    lane = lax.broadcasted_iota(jnp.int32, (c, LANES), 1)
    is_fox = lane < GATE_MI
    is_ig = (lane >= GATE_MI) & (lane < GATE_MF)
    is_used = lane < GATE_USED
    bias = b_ref[...]
    for blk in range(tg // c):
        rs = slice(blk * c, (blk + 1) * c)
        x = x_ref[rs, :] + bias
        logsig = jnp.minimum(x, 0.0) - jnp.log1p(jnp.exp(-jnp.abs(x)))
        val = jnp.where(is_ig, x, logsig)
        val = jnp.where(is_used, val, 0.0)
        csp = jnp.dot(tri, jnp.concatenate(_split3(val), axis=1), preferred_element_type=F32)
        cs = csp[:, 0:LANES] + csp[:, LANES:2 * LANES] + csp[:, 2 * LANES:3 * LANES]
        glob = cs + carry_ref[0:1, :]
        carry_ref[0:1, :] = glob[c - 1:c, :]
        f_t = cs * LOG2E
        b_t = (pltpu.roll(val, GATE_MF - GATE_MI, axis=1) - cs) * LOG2E
        bt = b_t.T
        ct_ref[0, :, rs] = bt[:GATE_USED, :]
        cm = jnp.zeros((c, LANES), F32)
        for h in range(ML_HEADS):
            k = GATE_MF + h
            run_max = jnp.max(jnp.where(causal, bt[k:k + 1, :], -jnp.inf),
                              axis=-1, keepdims=True)
            cm = jnp.where(lane == k, run_max, cm)
        r_ref[rs, 0:LANES] = f_t
        r_ref[rs, LANES:2 * LANES] = b_t
        r_ref[rs, 2 * LANES:3 * LANES] = cm
        terms = _split3(jnp.where(is_fox, glob * (-LOG2E), 0.0))
        cb_ref[rs, :] = jnp.dot(jnp.concatenate(terms, axis=1), pm_ref[...],
                                preferred_element_type=F32).astype(BF16)


def _fox_bias_placement():
    pm = np.zeros((3, LANES, FOX_W), np.float32)
    for h in range(FOX_HEADS):
        base = (h // 2) * LANES + (FOX_HD if h % 2 == 0 else 0)
        for t in range(3):
            pm[t, h, base + t] = 1.0
    return jnp.asarray(pm.reshape(3 * LANES, FOX_W), BF16)


def _gates(gpre, bias, *, batch, seq, tg):
    ns = seq // tg
    rmap = lambda b, s: (b * ns + s, 0)
    return pl.pallas_call(
        functools.partial(_gates_kernel, tg=tg),
        grid=(batch, ns),
        in_specs=[pl.BlockSpec((tg, LANES), rmap),
                  pl.BlockSpec((1, LANES), lambda b, s: (0, 0)),
                  pl.BlockSpec((3 * LANES, FOX_W), lambda b, s: (0, 0))],
        out_specs=[pl.BlockSpec((tg, 3 * LANES), rmap),
                   pl.BlockSpec((1, GATE_USED, tg), lambda b, s: (b, 0, s)),
                   pl.BlockSpec((tg, FOX_W), rmap)],
        out_shape=[jax.ShapeDtypeStruct((batch * seq, 3 * LANES), F32),
                   jax.ShapeDtypeStruct((batch, GATE_USED, seq), F32),
                   jax.ShapeDtypeStruct((batch * seq, FOX_W), BF16)],
        scratch_shapes=[pltpu.VMEM((SUBLANES, LANES), F32)],
        compiler_params=_params("parallel", "arbitrary"),
        name="gates")(gpre, bias, _fox_bias_placement())


def _fox_lane_rows():
    r = np.zeros((SUBLANES, LANES), np.float32)
    r[0, :FOX_HD] = 1.0
    r[1, FOX_HD:] = 1.0
    r[2, FOX_HD:FOX_HD + 3] = 1.0
    r[3, 0:3] = 1.0
    r[4, FOX_HD] = 1.0
    r[5, 0] = 1.0
    return jnp.asarray(r, BF16)


def _fox_kernel(q_ref, k_ref, v_ref, cb_ref, mk_ref, o_ref, m_ref, acc_ref, *, tq, tk):
    i = pl.program_id(2)
    n_sub = tq // tk
    npp = q_ref.shape[1] // LANES
    lo, hi = mk_ref[0:1, :], mk_ref[1:2, :]
    one_a, one_b = mk_ref[4:5, :], mk_ref[5:6, :]
    qh = []
    for pp in range(npp):
        q2 = q_ref[:, pp * LANES:(pp + 1) * LANES]
        qh += [q2 * lo + mk_ref[2:3, :], q2 * hi + mk_ref[3:4, :]]

    m_ref[...] = jnp.full_like(m_ref, -jnp.inf)
    acc_ref[...] = jnp.zeros_like(acc_ref)
    tri = (lax.broadcasted_iota(jnp.int32, (tk, tk), 1)
           <= lax.broadcasted_iota(jnp.int32, (tk, tk), 0))

    def step(j, row0, masked, tk=tk):
        rows = tq - row0
        r0 = pl.multiple_of(j * tk, tk)
        for pp in range(npp):
            ps = slice(pp * LANES, (pp + 1) * LANES)
            ks = k_ref[pl.ds(r0, tk), ps]
            vs = v_ref[pl.ds(r0, tk), ps]
            cb = cb_ref[pl.ds(r0, tk), ps]
            kh = (ks * lo + cb * hi, ks * hi + cb * lo)
            vh = (vs * lo + one_a, vs * hi + one_b)
            for h in range(2):
                hh = 2 * pp + h
                s = lax.dot_general(qh[hh][row0:, :], kh[h], NT_DIMS,
                                    preferred_element_type=F32)
                if masked:
                    top = jnp.where(tri, s[:tk], -jnp.inf)
                    s = top if rows == tk else jnp.concatenate([top, s[tk:]], axis=0)
                m_old = m_ref[hh, row0:, :]
                m_new = jnp.maximum(m_old, jnp.max(s, axis=-1, keepdims=True))
                alpha = jnp.exp2(m_old - m_new)
                pe = jnp.exp2(s - jnp.concatenate([m_new] * (tk // LANES), axis=1))
                acc_ref[hh, row0:, :] = alpha * acc_ref[hh, row0:, :] + jnp.dot(
                    pe.astype(BF16), vh[h], preferred_element_type=F32)
                m_ref[hh, row0:, :] = m_new

    tkf = 2 * tk

    def body(jj, carry):
        for u in range(tq // tkf):
            step(jj * (tq // tkf) + u, 0, False, tkf)
        return carry

    lax.fori_loop(0, i, body, 0)
    for d in range(n_sub):
        step(i * n_sub + d, d * tk, True)

    first = lax.broadcasted_iota(jnp.int32, (tq, LANES), 1) < FOX_HD
    for pp in range(npp):
        acc_a, acc_b = acc_ref[2 * pp], acc_ref[2 * pp + 1]
        l_a = jnp.broadcast_to(acc_a[:, FOX_HD:FOX_HD + 1], (tq, LANES))
        l_b = jnp.broadcast_to(acc_b[:, 0:1], (tq, LANES))
        o_ref[:, pp * LANES:(pp + 1) * LANES] = (
            jnp.where(first, acc_a, acc_b) / jnp.where(first, l_a, l_b)).astype(BF16)


def _fox(proj, cb, *, batch, seq, tq, tk, npp):
    nq = seq // tq
    ngrp = FOX_HEADS // 2 // npp
    w = npp * LANES
    qb, kb, vb = FQ_OFF // w, FK_OFF // w, FV_OFF // w
    return pl.pallas_call(
        functools.partial(_fox_kernel, tq=tq, tk=tk),
        grid=(batch, ngrp, nq),
        in_specs=[
            pl.BlockSpec((tq, w), lambda b, p, i: (b * nq + i, qb + p)),
            pl.BlockSpec((seq, w), lambda b, p, i: (b, kb + p)),
            pl.BlockSpec((seq, w), lambda b, p, i: (b, vb + p)),
            pl.BlockSpec((seq, w), lambda b, p, i: (b, p)),
            pl.BlockSpec((SUBLANES, LANES), lambda b, p, i: (0, 0)),
        ],
        out_specs=pl.BlockSpec((tq, w), lambda b, p, i: (b * nq + i, p)),
        out_shape=jax.ShapeDtypeStruct((batch * seq, FOX_W), BF16),
        scratch_shapes=[pltpu.VMEM((2 * npp, tq, LANES), F32),
                        pltpu.VMEM((2 * npp, tq, LANES), F32)],
        compiler_params=_params("parallel", "parallel", "arbitrary"),
        name="fox_attention")(proj, proj, proj, cb, _fox_lane_rows())


def _mlstm_kernel(u_ref, v_ref, mo_ref, r_ref, ct_ref, sel_ref, cw_ref, cb_ref,
                  nw_ref, o_ref, ubuf, q_s, k_s, st_ref, m_ref, *, tt):
    s_idx = pl.program_id(1)
    halo = SUBLANES

    @pl.when(s_idx == 0)
    def _():
        ubuf[0:halo, :] = jnp.zeros((halo, 2 * ML_W), F32)
        st_ref[...] = jnp.zeros_like(st_ref)
        m_ref[...] = jnp.zeros_like(m_ref)

    @pl.when(s_idx > 0)
    def _():
        ubuf[0:halo, :] = ubuf[tt:tt + halo, :]

    c = ML_CHUNK
    for blk in range(tt // c):
        ubuf[halo + blk * c:halo + (blk + 1) * c, :] = (
            u_ref[blk * c:(blk + 1) * c, :].astype(F32))

    def conv_block(blk):
        acc = cb_ref[...]
        for j in (CONV_K - 1,) + tuple(range(CONV_K - 1)):
            off = halo - (CONV_K - 1) + j + blk * c
            acc = acc + cw_ref[j:j + 1, :] * ubuf[off:off + c, :]
        qk = acc * jax.nn.sigmoid(acc)
        q_s[blk * c:(blk + 1) * c, :] = qk[:, :ML_W].astype(BF16)
        k_s[blk * c:(blk + 1) * c, :] = (qk[:, ML_W:] * (ML_HD ** -0.5)).astype(BF16)

    row = lax.broadcasted_iota(jnp.int32, (c, c), 0)
    col = lax.broadcasted_iota(jnp.int32, (c, c), 1)
    causal = col <= row
    ones_blk = jnp.ones((c, ML_HD), BF16)

    def chunk(ci):
        r0 = ci * c
        cb = ct_ref[0, :, pl.ds(r0, c)]
        rep = []
        for qi in range(3):
            hi, mid, lo = _split3(r_ref[pl.ds(r0, c), qi * LANES:(qi + 1) * LANES])
            rep.append(jnp.dot(jnp.concatenate([hi, mid], axis=1), sel_ref[...],
                               preferred_element_type=F32)
                       + jnp.dot(lo, sel_ref[0:LANES, :], preferred_element_type=F32))
        for h in range(ML_HEADS):
            hs = slice(h * ML_HD, (h + 1) * ML_HD)
            qh = q_s[pl.ds(r0, c), hs]
            kh = k_s[pl.ds(r0, c), hs]
            vh = v_ref[pl.ds(r0, c), hs]
            f_t, b_t, cm_t = rep[0][:, hs], rep[1][:, hs], rep[2][:, hs]
            b_row = cb[GATE_MF + h:GATE_MF + h + 1, :]
            m_prev = m_ref[h:h + 1, :]

            g_t = jnp.maximum(m_prev, cm_t)
            dw = jnp.exp2(jnp.where(causal, b_row - jnp.concatenate([g_t, g_t], axis=1),
                                    -jnp.inf))
            inter_w = jnp.exp2(m_prev - g_t)
            sqk = lax.dot_general(qh, kh, NT_DIMS, preferred_element_type=F32) * dw
            state = st_ref[h]
            qc = jnp.dot(qh, state.astype(BF16), preferred_element_type=F32)
            v_aug = jnp.concatenate([vh, ones_blk], axis=1)
            pv = jnp.dot(sqk.astype(BF16), v_aug, preferred_element_type=F32)
            num = inter_w * qc[:, :ML_HD] + pv[:, :ML_HD]
            den = inter_w * qc[:, ML_HD:] + pv[:, ML_HD:]
            hh = num / jnp.maximum(jnp.abs(den), jnp.exp2(-(f_t + g_t)))
            ms = jnp.mean(hh * hh, axis=-1, keepdims=True)
            y = (hh * lax.rsqrt(ms + EPS) * nw_ref[:, hs]
                 * jax.nn.sigmoid(mo_ref[pl.ds(r0, c), hs].astype(F32)))
            o_ref[pl.ds(r0, c), hs] = y.astype(BF16)

            f_last = f_t[c - 1:c, :]
            m_new = jnp.maximum(f_last + m_prev, f_last + cm_t[c - 1:c, :])
            s_old = jnp.exp2(f_last + m_prev - m_new)
            kw = (kh.astype(F32) * jnp.exp2(f_last + b_t - m_new)).astype(BF16)
            upd = lax.dot_general(kw, v_aug, TN_DIMS, preferred_element_type=F32)
            st_ref[h] = jnp.concatenate([s_old, s_old], axis=1) * state + upd
            m_ref[h:h + 1, :] = m_new

    for ci in range(tt // c):
        conv_block(ci)
        chunk(ci)


def _mlstm(proj, rows, ct, conv_w, conv_b, norm_w, *, batch, seq, tt):
    ns = seq // tt
    ub = MQK_OFF // (2 * ML_W)
    vb = MV_OFF // ML_W
    ob = MO_OFF // ML_W
    rmap = lambda b, s: (b * ns + s, 0)
    sel = np.zeros((2, LANES, ML_W), np.float32)
    for h in range(ML_HEADS):
        sel[:, GATE_MF + h, h * ML_HD:(h + 1) * ML_HD] = 1.0
    sel = jnp.asarray(sel.reshape(2 * LANES, ML_W), BF16)
    return pl.pallas_call(
        functools.partial(_mlstm_kernel, tt=tt),
        grid=(batch, ns),
        in_specs=[
            pl.BlockSpec((tt, 2 * ML_W), lambda b, s: (b * ns + s, ub)),
            pl.BlockSpec((tt, ML_W), lambda b, s: (b * ns + s, vb)),
            pl.BlockSpec((tt, ML_W), lambda b, s: (b * ns + s, ob)),
            pl.BlockSpec((tt, 3 * LANES), rmap),
            pl.BlockSpec((1, GATE_USED, tt), lambda b, s: (b, 0, s)),
            pl.BlockSpec((2 * LANES, ML_W), lambda b, s: (0, 0)),
            pl.BlockSpec((CONV_K, 2 * ML_W), lambda b, s: (0, 0)),
            pl.BlockSpec((1, 2 * ML_W), lambda b, s: (0, 0)),
            pl.BlockSpec((1, ML_W), lambda b, s: (0, 0)),
        ],
        out_specs=pl.BlockSpec((tt, ML_W), rmap),
        out_shape=jax.ShapeDtypeStruct((batch * seq, ML_W), BF16),
        scratch_shapes=[pltpu.VMEM((tt + SUBLANES, 2 * ML_W), F32),
                        pltpu.VMEM((tt, ML_W), BF16),
                        pltpu.VMEM((tt, ML_W), BF16),
                        pltpu.VMEM((ML_HEADS, ML_HD, 2 * ML_HD), F32),
                        pltpu.VMEM((SUBLANES, LANES), F32)],
        compiler_params=_params("parallel", "arbitrary"),
        name="mlstm")(proj, proj, proj, rows, ct, sel, conv_w, conv_b, norm_w)


def _mix_xattn_kernel(yf_ref, ym_ref, gt_ref, bg_ref, h_ref, wf_ref, wm_ref, wmix_ref,
                      g_ref, wq_ref, k_ref, v_ref, wo_ref, o_ref):
    g = jax.nn.sigmoid(gt_ref[...].astype(F32) + bg_ref[...])
    uf = jnp.dot(yf_ref[...], wf_ref[...], preferred_element_type=F32)
    um = jnp.dot(ym_ref[...], wm_ref[...], preferred_element_type=F32)
    merged = g[:, :D_MODEL] * uf + g[:, D_MODEL:] * um
    x = h_ref[...] + jnp.dot(merged.astype(BF16), wmix_ref[...], preferred_element_type=F32)

    xg, r = _rms_split(x, g_ref[...])
    q = jnp.dot(xg, wq_ref[...], preferred_element_type=F32)
    q = (q * (r * (X_HD ** -0.5 * LOG2E))).astype(BF16)
    outs = []
    for hd in range(X_HEADS):
        hs = slice(hd * X_HD, (hd + 1) * X_HD)
        s = lax.dot_general(q[:, hs], k_ref[:, hs], NT_DIMS, preferred_element_type=F32)
        e = jnp.exp2(s - jnp.max(s, axis=-1, keepdims=True))
        outs.append(jnp.dot(e.astype(BF16), v_ref[:, hs], preferred_element_type=F32)
                    / jnp.sum(e, axis=-1, keepdims=True))
    o = jnp.concatenate(outs, axis=1).astype(BF16)
    o_ref[...] = x + jnp.dot(o, wo_ref[...], preferred_element_type=F32)


def _mix_xattn(yf, ym, proj, b_gate, h, wf, wm, wmix, gain, wq, kv, wo, *,
               layer, batch, seq, n_mem, tm):
    ns = seq // tm
    const = lambda b, s: (0, 0)
    row = lambda b, s: (b * ns + s, 0)
    return pl.pallas_call(
        _mix_xattn_kernel, grid=(batch, ns),
        in_specs=[
            pl.BlockSpec((tm, FOX_W), row),
            pl.BlockSpec((tm, ML_W), row),
            pl.BlockSpec((tm, 2 * D_MODEL), lambda b, s: (b * ns + s, G_OFF // (2 * D_MODEL))),
            pl.BlockSpec((1, 2 * D_MODEL), const),
            pl.BlockSpec((tm, D_MODEL), row),
            _layer_block((FOX_W, D_MODEL), layer),
            _layer_block((ML_W, D_MODEL), layer),
            _layer_block((D_MODEL, D_MODEL), layer),
            pl.BlockSpec((1, D_MODEL), const),
            _layer_block((D_MODEL, D_MODEL), layer),
            pl.BlockSpec((n_mem, D_MODEL), lambda b, s: (b, 0)),
            pl.BlockSpec((n_mem, D_MODEL), lambda b, s: (b, 1)),
            _layer_block((D_MODEL, D_MODEL), layer),
        ],
        out_specs=pl.BlockSpec((tm, D_MODEL), row),
        out_shape=jax.ShapeDtypeStruct((batch * seq, D_MODEL), F32),
        compiler_params=_params("parallel", "parallel"),
        name="mix_cross_attention")(yf, ym, proj, b_gate, h, wf, wm, wmix,
                                    gain, wq, kv, kv, wo)


def _mlp_kernel(h_ref, g_ref, w1_ref, w2_ref, gf_ref, o_ref, *, tf, final):
    x = h_ref[...]
    xg, r = _rms_split(x, g_ref[...])
    acc = x
    for j in range(D_FF // tf):
        fs = slice(j * tf, (j + 1) * tf)
        a = jnp.maximum(r * jnp.dot(xg, w1_ref[:, fs], preferred_element_type=F32), 0.0)
        acc = acc + jnp.dot((a * a).astype(BF16), w2_ref[fs, :], preferred_element_type=F32)
    o_ref[...] = _rms(acc, gf_ref[...]) if final else acc


def _mlp(h, gain, w1, w2, gain_final, *, layer, tm, tf, final):
    m = h.shape[0]
    const = lambda i: (0, 0)
    row = lambda i: (i, 0)
    return pl.pallas_call(
        functools.partial(_mlp_kernel, tf=tf, final=final),
        grid=(m // tm,),
        in_specs=[
            pl.BlockSpec((tm, D_MODEL), row),
            pl.BlockSpec((1, D_MODEL), const),
            _layer_block((D_MODEL, D_FF), layer),
            _layer_block((D_FF, D_MODEL), layer),
            pl.BlockSpec((1, D_MODEL), const),
        ],
        out_specs=pl.BlockSpec((tm, D_MODEL), row),
        out_shape=jax.ShapeDtypeStruct((m, D_MODEL), F32),
        compiler_params=_params("parallel"),
        name="mlp")(h, gain, w1, w2, gain_final)


def _pick(total, prefs):
    for t in prefs:
        if total % t == 0:
            return t
    raise ValueError(f"no tile in {prefs} divides {total}")


def kernel(x, mem, norm_mix, norm_xattn, norm_mem, norm_mlp, w_in, b_fox_f, b_mlstm_i,
           b_mlstm_f, b_gate, conv_w, conv_b, mlstm_norm, w_up_fox, w_up_mlstm, w_mix_out,
           w_xq, w_xkv, w_xo, w_mlp_in, w_mlp_out, norm_final):
    batch, seq, d = x.shape
    n_mem = mem.shape[1]
    depth = w_in.shape[0]
    assert d == D_MODEL and seq % ML_CHUNK == 0
    m = batch * seq

    tm_proj = _pick(m, (512, 256))
    tn_proj = 512
    t_seq = _pick(seq, (1024, 512, 256))
    tk = _pick(seq, (512, 256))
    tq = _pick(seq, (4 * tk, 2 * tk, tk))
    tm_x = _pick(seq, (512, 256))
    tm_mlp = _pick(m, (512, 256))
    tf = 1024

    sp = [0, FOX_W, 2 * FOX_W, 3 * FOX_W, 3 * FOX_W + FOX_HEADS]
    sp += [sp[-1] + 2 * ML_W, sp[-1] + 3 * ML_W, sp[-1] + 4 * ML_W]
    sp += [sp[-1] + ML_HEADS, sp[-1] + 2 * ML_HEADS, sp[-1] + 2 * ML_HEADS + 2 * D_MODEL]
    w_in_b = w_in.astype(BF16)
    seg = [w_in_b[:, :, a:b] for a, b in zip(sp[:-1], sp[1:])]
    fq, fk, fv, ff, mqk, mv, mo, mi, mf, gp = seg
    fq = (w_in[:, :, sp[0]:sp[1]] * (FOX_HD ** -0.5 * LOG2E)).astype(BF16)
    w_main = jnp.concatenate([gp, mqk, fq, fk, fv, mv, mo], axis=2)
    w_gate = jnp.concatenate(
        [ff, mi, mf, jnp.zeros((depth, d, LANES - GATE_USED), BF16)], axis=2)
    gate_bias = jnp.concatenate(
        [b_fox_f, b_mlstm_i, b_mlstm_f, jnp.zeros((depth, LANES - GATE_USED), F32)], axis=1)
    bf = lambda w: w.astype(BF16)
    w_up_fox, w_up_mlstm, w_mix_out = bf(w_up_fox), bf(w_up_mlstm), bf(w_mix_out)
    w_xq, w_xkv, w_xo = bf(w_xq), bf(w_xkv), bf(w_xo)
    w_mlp_in, w_mlp_out = bf(w_mlp_in), bf(w_mlp_out)

    h = x.reshape(m, d)
    mem2 = mem.reshape(batch * n_mem, d)
    for l in range(depth):
        proj, gpre = _in_proj(h, norm_mix[l][None], w_main, w_gate,
                              layer=l, tm=tm_proj, tn=tn_proj)
        rows, ct, cb = _gates(gpre, gate_bias[l][None], batch=batch, seq=seq, tg=t_seq)
        y_fox = _fox(proj, cb, batch=batch, seq=seq, tq=tq, tk=tk, npp=1)
        y_ml = _mlstm(proj, rows, ct, conv_w[l], conv_b[l][None], mlstm_norm[l][None],
                      batch=batch, seq=seq, tt=t_seq)
        kv = _norm_proj(mem2, norm_mem[l][None], w_xkv,
                        layer=l, tm=_pick(batch * n_mem, (1024, 512, 256)), tn=1024)
        h = _mix_xattn(y_fox, y_ml, proj, b_gate[l][None], h,
                       w_up_fox, w_up_mlstm, w_mix_out,
                       norm_xattn[l][None], w_xq, kv, w_xo,
                       layer=l, batch=batch, seq=seq, n_mem=n_mem, tm=tm_x)
        h = _mlp(h, norm_mlp[l][None], w_mlp_in, w_mlp_out, norm_final[None],
                 layer=l, tm=tm_mlp, tf=tf, final=(l == depth - 1))
    return h.reshape(batch, seq, d)
```

```python
import functools

import jax
import jax.numpy as jnp
import numpy as np
from jax import lax
from jax.experimental import pallas as pl
from jax.experimental.pallas import tpu as pltpu

F32 = jnp.float32
BF16 = jnp.bfloat16

D_MODEL = 1024
FOX_HEADS = 8
FOX_HD = 64
FOX_W = FOX_HEADS * FOX_HD
ML_HEADS = 4
ML_HD = 128
ML_W = ML_HEADS * ML_HD
CONV_K = 4
X_HEADS = 4
X_HD = D_MODEL // X_HEADS
D_FF = 4 * D_MODEL
EPS = 1e-6
LOG2E = 1.4426950408889634

LANES = 128
SUBLANES = 8
V7X_VMEM_LIMIT_BYTES = 56 * 1024 * 1024

G_OFF = 0
MQK_OFF = G_OFF + 2 * D_MODEL
FQ_OFF = MQK_OFF + 2 * ML_W
FK_OFF = FQ_OFF + FOX_W
FV_OFF = FK_OFF + FOX_W
MV_OFF = FV_OFF + FOX_W
MO_OFF = MV_OFF + ML_W
PROJ_W = MO_OFF + ML_W
GATE_FF = 0
GATE_MI = GATE_FF + FOX_HEADS
GATE_MF = GATE_MI + ML_HEADS
GATE_USED = GATE_MF + ML_HEADS

ML_CHUNK = 256
NT_DIMS = (((1,), (1,)), ((), ()))
TN_DIMS = (((0,), (0,)), ((), ()))


def _params(*sem):
    return pltpu.CompilerParams(dimension_semantics=sem,
                                vmem_limit_bytes=V7X_VMEM_LIMIT_BYTES)


def _rms(x, g):
    return x * lax.rsqrt(jnp.mean(x * x, axis=-1, keepdims=True) + EPS) * g


def _rms_split(x, g):
    r = lax.rsqrt(jnp.mean(x * x, axis=-1, keepdims=True) + EPS)
    return (x * g).astype(BF16), r


def _layer_block(shape, layer):
    zeros = (0,) * len(shape)
    return pl.BlockSpec((None,) + tuple(shape), lambda *_: (layer,) + zeros)


def _in_proj_kernel(x_ref, g_ref, w_ref, wg_ref, o_ref, og_ref, *, tn):
    xg, r = _rms_split(x_ref[...], g_ref[...])
    og_ref[...] = r * jnp.dot(xg, wg_ref[...], preferred_element_type=F32)
    for j in range(w_ref.shape[1] // tn):
        cs = slice(j * tn, (j + 1) * tn)
        o_ref[:, cs] = (r * jnp.dot(xg, w_ref[:, cs], preferred_element_type=F32)).astype(BF16)


def _in_proj(x, gain, w, wg, *, layer, tm, tn):
    m, d = x.shape
    n = w.shape[-1]
    const = lambda i: (0, 0)
    return pl.pallas_call(
        functools.partial(_in_proj_kernel, tn=tn), grid=(m // tm,),
        in_specs=[pl.BlockSpec((tm, d), lambda i: (i, 0)),
                  pl.BlockSpec((1, d), const),
                  _layer_block((d, n), layer),
                  _layer_block((d, LANES), layer)],
        out_specs=[pl.BlockSpec((tm, n), lambda i: (i, 0)),
                   pl.BlockSpec((tm, LANES), lambda i: (i, 0))],
        out_shape=[jax.ShapeDtypeStruct((m, n), BF16),
                   jax.ShapeDtypeStruct((m, LANES), F32)],
        compiler_params=_params("parallel"),
        name="norm_in_proj")(x, gain, w, wg)


def _norm_proj_kernel(x_ref, g_ref, w_ref, o_ref, xn_ref):
    @pl.when(pl.program_id(1) == 0)
    def _():
        xn_ref[...] = _rms(x_ref[...], g_ref[...]).astype(BF16)

    o_ref[...] = jnp.dot(xn_ref[...], w_ref[...],
                         preferred_element_type=F32).astype(BF16)


def _norm_proj(x, gain, w, *, layer, tm, tn):
    m, d = x.shape
    n = w.shape[-1]
    return pl.pallas_call(
        _norm_proj_kernel, grid=(m // tm, n // tn),
        in_specs=[pl.BlockSpec((tm, d), lambda i, j: (i, 0)),
                  pl.BlockSpec((1, d), lambda i, j: (0, 0)),
                  pl.BlockSpec((None, d, tn), lambda i, j: (layer, 0, j))],
        out_specs=pl.BlockSpec((tm, tn), lambda i, j: (i, j)),
        out_shape=jax.ShapeDtypeStruct((m, n), BF16),
        scratch_shapes=[pltpu.VMEM((tm, d), BF16)],
        compiler_params=_params("parallel", "arbitrary"),
        name="norm_proj")(x, gain, w)


def _split3(v):
    hi = v.astype(BF16)
    r1 = v - hi.astype(F32)
    mid = r1.astype(BF16)
    lo = (r1 - mid.astype(F32)).astype(BF16)
    return hi, mid, lo


def _gates_kernel(x_ref, b_ref, pm_ref, r_ref, ct_ref, cb_ref, carry_ref, *, tg):
    @pl.when(pl.program_id(1) == 0)
    def _():
        carry_ref[...] = jnp.zeros_like(carry_ref)

    c = ML_CHUNK
    row = lax.broadcasted_iota(jnp.int32, (c, c), 0)
    col = lax.broadcasted_iota(jnp.int32, (c, c), 1)
    causal = row >= col
    tri = jnp.where(causal, 1.0, 0.0).astype(BF16)
    lane = lax.broadcasted_iota(jnp.int32, (c, LANES), 1)
    is_fox = lane < GATE_MI
    is_ig = (lane >= GATE_MI) & (lane < GATE_MF)
    is_used = lane < GATE_USED
    bias = b_ref[...]
    for blk in range(tg // c):
        rs = slice(blk * c, (blk + 1) * c)
        x = x_ref[rs, :] + bias
        logsig = jnp.minimum(x, 0.0) - jnp.log1p(jnp.exp(-jnp.abs(x)))
        val = jnp.where(is_ig, x, logsig)
        val = jnp.where(is_used, val, 0.0)
        csp = jnp.dot(tri, jnp.concatenate(_split3(val), axis=1), preferred_element_type=F32)
        cs = csp[:, 0:LANES] + csp[:, LANES:2 * LANES] + csp[:, 2 * LANES:3 * LANES]
        glob = cs + carry_ref[0:1, :]
        carry_ref[0:1, :] = glob[c - 1:c, :]
        f_t = cs * LOG2E
        b_t = (pltpu.roll(val, GATE_MF - GATE_MI, axis=1) - cs) * LOG2E
        bt = b_t.T
        ct_ref[0, :, rs] = bt[:GATE_USED, :]
        cm = jnp.zeros((c, LANES), F32)
        for h in range(ML_HEADS):
            k = GATE_MF + h
            run_max = jnp.max(jnp.where(causal, bt[k:k + 1, :], -jnp.inf),
                              axis=-1, keepdims=True)
            cm = jnp.where(lane == k, run_max, cm)
        r_ref[rs, 0:LANES] = f_t
        r_ref[rs, LANES:2 * LANES] = b_t
        r_ref[rs, 2 * LANES:3 * LANES] = cm
        terms = _split3(jnp.where(is_fox, glob * (-LOG2E), 0.0))
        cb_ref[rs, :] = jnp.dot(jnp.concatenate(terms, axis=1), pm_ref[...],
                                preferred_element_type=F32).astype(BF16)


def _fox_bias_placement():
    pm = np.zeros((3, LANES, FOX_W), np.float32)
    for h in range(FOX_HEADS):
        base = (h // 2) * LANES + (FOX_HD if h % 2 == 0 else 0)
        for t in range(3):
            pm[t, h, base + t] = 1.0
    return jnp.asarray(pm.reshape(3 * LANES, FOX_W), BF16)


def _gates(gpre, bias, *, batch, seq, tg):
    ns = seq // tg
    rmap = lambda b, s: (b * ns + s, 0)
    return pl.pallas_call(
        functools.partial(_gates_kernel, tg=tg),
        grid=(batch, ns),
        in_specs=[pl.BlockSpec((tg, LANES), rmap),
                  pl.BlockSpec((1, LANES), lambda b, s: (0, 0)),
                  pl.BlockSpec((3 * LANES, FOX_W), lambda b, s: (0, 0))],
        out_specs=[pl.BlockSpec((tg, 3 * LANES), rmap),
                   pl.BlockSpec((1, GATE_USED, tg), lambda b, s: (b, 0, s)),
                   pl.BlockSpec((tg, FOX_W), rmap)],
        out_shape=[jax.ShapeDtypeStruct((batch * seq, 3 * LANES), F32),
                   jax.ShapeDtypeStruct((batch, GATE_USED, seq), F32),
                   jax.ShapeDtypeStruct((batch * seq, FOX_W), BF16)],
        scratch_shapes=[pltpu.VMEM((SUBLANES, LANES), F32)],
        compiler_params=_params("parallel", "arbitrary"),
        name="gates")(gpre, bias, _fox_bias_placement())


def _fox_lane_rows():
    r = np.zeros((SUBLANES, LANES), np.float32)
    r[0, :FOX_HD] = 1.0
    r[1, FOX_HD:] = 1.0
    r[2, FOX_HD:FOX_HD + 3] = 1.0
    r[3, 0:3] = 1.0
    r[4, FOX_HD] = 1.0
    r[5, 0] = 1.0
    return jnp.asarray(r, BF16)


def _fox_kernel(q_ref, k_ref, v_ref, cb_ref, mk_ref, o_ref, m_ref, acc_ref, *, tq, tk):
    i = pl.program_id(2)
    n_sub = tq // tk
    npp = q_ref.shape[1] // LANES
    lo, hi = mk_ref[0:1, :], mk_ref[1:2, :]
    one_a, one_b = mk_ref[4:5, :], mk_ref[5:6, :]
    qh = []
    for pp in range(npp):
        q2 = q_ref[:, pp * LANES:(pp + 1) * LANES]
        qh += [q2 * lo + mk_ref[2:3, :], q2 * hi + mk_ref[3:4, :]]

    tri = (lax.broadcasted_iota(jnp.int32, (tk, tk), 1)
           <= lax.broadcasted_iota(jnp.int32, (tk, tk), 0))

    def step(j, row0, masked, tk=tk, first=False):
        rows = tq - row0
        r0 = pl.multiple_of(j * tk, tk)
        for pp in range(npp):
            ps = slice(pp * LANES, (pp + 1) * LANES)
            ks = k_ref[pl.ds(r0, tk), ps]
            vs = v_ref[pl.ds(r0, tk), ps]
            cb = cb_ref[pl.ds(r0, tk), ps]
            kh = (ks * lo + cb * hi, ks * hi + cb * lo)
            vh = (vs * lo + one_a, vs * hi + one_b)
            for h in range(2):
                hh = 2 * pp + h
                s = lax.dot_general(qh[hh][row0:, :], kh[h], NT_DIMS,
                                    preferred_element_type=F32)
                if masked:
                    top = jnp.where(tri, s[:tk], -jnp.inf)
                    s = top if rows == tk else jnp.concatenate([top, s[tk:]], axis=0)
                s_max = jnp.max(s, axis=-1, keepdims=True)
                if first:
                    m_new = jnp.broadcast_to(s_max, (rows, LANES))
                else:
                    m_old = m_ref[hh, row0:, :]
                    m_new = jnp.maximum(m_old, s_max)
                pe = jnp.exp2(s - jnp.concatenate([m_new] * (tk // LANES), axis=1))
                pv = jnp.dot(pe.astype(BF16), vh[h], preferred_element_type=F32)
                if first:
                    acc_ref[hh, row0:, :] = pv
                else:
                    acc_ref[hh, row0:, :] = jnp.exp2(m_old - m_new) * acc_ref[hh, row0:, :] + pv
                m_ref[hh, row0:, :] = m_new

    tkf = 2 * tk

    def body(jj, carry):
        for u in range(tq // tkf):
            step(jj * (tq // tkf) + u, 0, False, tkf)
        return carry

    for d in range(n_sub):
        step(i * n_sub + d, d * tk, True, first=(d == 0))
    lax.fori_loop(0, i, body, 0)

    first = lax.broadcasted_iota(jnp.int32, (tq, LANES), 1) < FOX_HD
    for pp in range(npp):
        acc_a, acc_b = acc_ref[2 * pp], acc_ref[2 * pp + 1]
        l_a = jnp.broadcast_to(acc_a[:, FOX_HD:FOX_HD + 1], (tq, LANES))
        l_b = jnp.broadcast_to(acc_b[:, 0:1], (tq, LANES))
        o_ref[:, pp * LANES:(pp + 1) * LANES] = (
            jnp.where(first, acc_a, acc_b) / jnp.where(first, l_a, l_b)).astype(BF16)


def _fox(proj, cb, *, batch, seq, tq, tk, npp):
    nq = seq // tq
    ngrp = FOX_HEADS // 2 // npp
    w = npp * LANES
    qb, kb, vb = FQ_OFF // w, FK_OFF // w, FV_OFF // w
    return pl.pallas_call(
        functools.partial(_fox_kernel, tq=tq, tk=tk),
        grid=(batch, ngrp, nq),
        in_specs=[
            pl.BlockSpec((tq, w), lambda b, p, i: (b * nq + i, qb + p)),
            pl.BlockSpec((seq, w), lambda b, p, i: (b, kb + p)),
            pl.BlockSpec((seq, w), lambda b, p, i: (b, vb + p)),
            pl.BlockSpec((seq, w), lambda b, p, i: (b, p)),
            pl.BlockSpec((SUBLANES, LANES), lambda b, p, i: (0, 0)),
        ],
        out_specs=pl.BlockSpec((tq, w), lambda b, p, i: (b * nq + i, p)),
        out_shape=jax.ShapeDtypeStruct((batch * seq, FOX_W), BF16),
        scratch_shapes=[pltpu.VMEM((2 * npp, tq, LANES), F32),
                        pltpu.VMEM((2 * npp, tq, LANES), F32)],
        compiler_params=_params("parallel", "parallel", "arbitrary"),
        name="fox_attention")(proj, proj, proj, cb, _fox_lane_rows())


def _mlstm_kernel(u_ref, v_ref, mo_ref, r_ref, ct_ref, sel_ref, cw_ref, cb_ref,
                  nw_ref, o_ref, ubuf, q_s, k_s, st_ref, m_ref, *, tt):
    s_idx = pl.program_id(1)
    halo = SUBLANES

    @pl.when(s_idx == 0)
    def _():
        ubuf[0:halo, :] = jnp.zeros((halo, 2 * ML_W), F32)
        st_ref[...] = jnp.zeros_like(st_ref)
        m_ref[...] = jnp.zeros_like(m_ref)

    @pl.when(s_idx > 0)
    def _():
        ubuf[0:halo, :] = ubuf[tt:tt + halo, :]

    c = ML_CHUNK
    for blk in range(tt // c):
        ubuf[halo + blk * c:halo + (blk + 1) * c, :] = (
            u_ref[blk * c:(blk + 1) * c, :].astype(F32))

    def conv_block(blk):
        acc = cb_ref[...]
        for j in (CONV_K - 1,) + tuple(range(CONV_K - 1)):
            off = halo - (CONV_K - 1) + j + blk * c
            acc = acc + cw_ref[j:j + 1, :] * ubuf[off:off + c, :]
        qk = acc * jax.nn.sigmoid(acc)
        q_s[blk * c:(blk + 1) * c, :] = qk[:, :ML_W].astype(BF16)
        k_s[blk * c:(blk + 1) * c, :] = (qk[:, ML_W:] * (ML_HD ** -0.5)).astype(BF16)

    row = lax.broadcasted_iota(jnp.int32, (c, c), 0)
    col = lax.broadcasted_iota(jnp.int32, (c, c), 1)
    causal = col <= row
    ones_blk = jnp.ones((c, ML_HD), BF16)

    def chunk(ci):
        r0 = ci * c
        cb = ct_ref[0, :, pl.ds(r0, c)]
        rep = []
        for qi in range(3):
            hi, mid, lo = _split3(r_ref[pl.ds(r0, c), qi * LANES:(qi + 1) * LANES])
            rep.append(jnp.dot(jnp.concatenate([hi, mid], axis=1), sel_ref[...],
                               preferred_element_type=F32)
                       + jnp.dot(lo, sel_ref[0:LANES, :], preferred_element_type=F32))
        for h in range(ML_HEADS):
            hs = slice(h * ML_HD, (h + 1) * ML_HD)
            qh = q_s[pl.ds(r0, c), hs]
            kh = k_s[pl.ds(r0, c), hs]
            vh = v_ref[pl.ds(r0, c), hs]
            f_t, b_t, cm_t = rep[0][:, hs], rep[1][:, hs], rep[2][:, hs]
            b_row = cb[GATE_MF + h:GATE_MF + h + 1, :]
            m_prev = m_ref[h:h + 1, :]

            g_t = jnp.maximum(m_prev, cm_t)
            dw = jnp.exp2(jnp.where(causal, b_row - jnp.concatenate([g_t, g_t], axis=1),
                                    -jnp.inf))
            inter_w = jnp.exp2(m_prev - g_t)
            sqk = lax.dot_general(qh, kh, NT_DIMS, preferred_element_type=F32) * dw
            state = st_ref[h]
            qc = jnp.dot(qh, state.astype(BF16), preferred_element_type=F32)
            v_aug = jnp.concatenate([vh, ones_blk], axis=1)
            pv = jnp.dot(sqk.astype(BF16), v_aug, preferred_element_type=F32)
            num = inter_w * qc[:, :ML_HD] + pv[:, :ML_HD]
            den = inter_w * qc[:, ML_HD:] + pv[:, ML_HD:]
            hh = num / jnp.maximum(jnp.abs(den), jnp.exp2(-(f_t + g_t)))
            ms = jnp.mean(hh * hh, axis=-1, keepdims=True)
            y = (hh * lax.rsqrt(ms + EPS) * nw_ref[:, hs]
                 * jax.nn.sigmoid(mo_ref[pl.ds(r0, c), hs].astype(F32)))
            o_ref[pl.ds(r0, c), hs] = y.astype(BF16)

            f_last = f_t[c - 1:c, :]
            m_new = jnp.maximum(f_last + m_prev, f_last + cm_t[c - 1:c, :])
            s_old = jnp.exp2(f_last + m_prev - m_new)
            kw = (kh.astype(F32) * jnp.exp2(f_last + b_t - m_new)).astype(BF16)
            upd = lax.dot_general(kw, v_aug, TN_DIMS, preferred_element_type=F32)
            st_ref[h] = jnp.concatenate([s_old, s_old], axis=1) * state + upd
            m_ref[h:h + 1, :] = m_new

    for ci in range(tt // c):
        conv_block(ci)
        chunk(ci)


def _mlstm(proj, rows, ct, conv_w, conv_b, norm_w, *, batch, seq, tt):
    ns = seq // tt
    ub = MQK_OFF // (2 * ML_W)
    vb = MV_OFF // ML_W
    ob = MO_OFF // ML_W
    rmap = lambda b, s: (b * ns + s, 0)
    sel = np.zeros((2, LANES, ML_W), np.float32)
    for h in range(ML_HEADS):
        sel[:, GATE_MF + h, h * ML_HD:(h + 1) * ML_HD] = 1.0
    sel = jnp.asarray(sel.reshape(2 * LANES, ML_W), BF16)
    return pl.pallas_call(
        functools.partial(_mlstm_kernel, tt=tt),
        grid=(batch, ns),
        in_specs=[
            pl.BlockSpec((tt, 2 * ML_W), lambda b, s: (b * ns + s, ub)),
            pl.BlockSpec((tt, ML_W), lambda b, s: (b * ns + s, vb)),
            pl.BlockSpec((tt, ML_W), lambda b, s: (b * ns + s, ob)),
            pl.BlockSpec((tt, 3 * LANES), rmap),
            pl.BlockSpec((1, GATE_USED, tt), lambda b, s: (b, 0, s)),
            pl.BlockSpec((2 * LANES, ML_W), lambda b, s: (0, 0)),
            pl.BlockSpec((CONV_K, 2 * ML_W), lambda b, s: (0, 0)),
            pl.BlockSpec((1, 2 * ML_W), lambda b, s: (0, 0)),
            pl.BlockSpec((1, ML_W), lambda b, s: (0, 0)),
        ],
        out_specs=pl.BlockSpec((tt, ML_W), rmap),
        out_shape=jax.ShapeDtypeStruct((batch * seq, ML_W), BF16),
        scratch_shapes=[pltpu.VMEM((tt + SUBLANES, 2 * ML_W), F32),
                        pltpu.VMEM((tt, ML_W), BF16),
                        pltpu.VMEM((tt, ML_W), BF16),
                        pltpu.VMEM((ML_HEADS, ML_HD, 2 * ML_HD), F32),
                        pltpu.VMEM((SUBLANES, LANES), F32)],
        compiler_params=_params("parallel", "arbitrary"),
        name="mlstm")(proj, proj, proj, rows, ct, sel, conv_w, conv_b, norm_w)


def _mix_xattn_kernel(yf_ref, ym_ref, gt_ref, bg_ref, h_ref, wf_ref, wm_ref, wmix_ref,
                      g_ref, wq_ref, k_ref, v_ref, wo_ref, o_ref):
    g = jax.nn.sigmoid(gt_ref[...].astype(F32) + bg_ref[...])
    uf = jnp.dot(yf_ref[...], wf_ref[...], preferred_element_type=F32)
    um = jnp.dot(ym_ref[...], wm_ref[...], preferred_element_type=F32)
    merged = g[:, :D_MODEL] * uf + g[:, D_MODEL:] * um
    x = h_ref[...] + jnp.dot(merged.astype(BF16), wmix_ref[...], preferred_element_type=F32)

    xg, r = _rms_split(x, g_ref[...])
    q = jnp.dot(xg, wq_ref[...], preferred_element_type=F32)
    q = (q * (r * (X_HD ** -0.5 * LOG2E))).astype(BF16)
    outs = []
    for hd in range(X_HEADS):
        hs = slice(hd * X_HD, (hd + 1) * X_HD)
        s = lax.dot_general(q[:, hs], k_ref[:, hs], NT_DIMS, preferred_element_type=F32)
        e = jnp.exp2(s - jnp.max(s, axis=-1, keepdims=True))
        outs.append(jnp.dot(e.astype(BF16), v_ref[:, hs], preferred_element_type=F32)
                    / jnp.sum(e, axis=-1, keepdims=True))
    o = jnp.concatenate(outs, axis=1).astype(BF16)
    o_ref[...] = x + jnp.dot(o, wo_ref[...], preferred_element_type=F32)


def _mix_xattn(yf, ym, proj, b_gate, h, wf, wm, wmix, gain, wq, kv, wo, *,
               layer, batch, seq, n_mem, tm):
    ns = seq // tm
    const = lambda b, s: (0, 0)
    row = lambda b, s: (b * ns + s, 0)
    return pl.pallas_call(
        _mix_xattn_kernel, grid=(batch, ns),
        in_specs=[
            pl.BlockSpec((tm, FOX_W), row),
            pl.BlockSpec((tm, ML_W), row),
            pl.BlockSpec((tm, 2 * D_MODEL), lambda b, s: (b * ns + s, G_OFF // (2 * D_MODEL))),
            pl.BlockSpec((1, 2 * D_MODEL), const),
            pl.BlockSpec((tm, D_MODEL), row),
            _layer_block((FOX_W, D_MODEL), layer),
            _layer_block((ML_W, D_MODEL), layer),
            _layer_block((D_MODEL, D_MODEL), layer),
            pl.BlockSpec((1, D_MODEL), const),
            _layer_block((D_MODEL, D_MODEL), layer),
            pl.BlockSpec((n_mem, D_MODEL), lambda b, s: (b, 0)),
            pl.BlockSpec((n_mem, D_MODEL), lambda b, s: (b, 1)),
            _layer_block((D_MODEL, D_MODEL), layer),
        ],
        out_specs=pl.BlockSpec((tm, D_MODEL), row),
        out_shape=jax.ShapeDtypeStruct((batch * seq, D_MODEL), F32),
        compiler_params=_params("parallel", "parallel"),
        name="mix_cross_attention")(yf, ym, proj, b_gate, h, wf, wm, wmix,
                                    gain, wq, kv, kv, wo)


def _mlp_kernel(h_ref, g_ref, w1_ref, w2_ref, gf_ref, o_ref, *, tf, final):
    x = h_ref[...]
    xg, r = _rms_split(x, g_ref[...])
    acc = x
    for j in range(D_FF // tf):
        fs = slice(j * tf, (j + 1) * tf)
        a = jnp.maximum(r * jnp.dot(xg, w1_ref[:, fs], preferred_element_type=F32), 0.0)
        acc = acc + jnp.dot((a * a).astype(BF16), w2_ref[fs, :], preferred_element_type=F32)
    o_ref[...] = _rms(acc, gf_ref[...]) if final else acc


def _mlp(h, gain, w1, w2, gain_final, *, layer, tm, tf, final):
    m = h.shape[0]
    const = lambda i: (0, 0)
    row = lambda i: (i, 0)
    return pl.pallas_call(
        functools.partial(_mlp_kernel, tf=tf, final=final),
        grid=(m // tm,),
        in_specs=[
            pl.BlockSpec((tm, D_MODEL), row),
            pl.BlockSpec((1, D_MODEL), const),
            _layer_block((D_MODEL, D_FF), layer),
            _layer_block((D_FF, D_MODEL), layer),
            pl.BlockSpec((1, D_MODEL), const),
        ],
        out_specs=pl.BlockSpec((tm, D_MODEL), row),
        out_shape=jax.ShapeDtypeStruct((m, D_MODEL), F32),
        compiler_params=_params("parallel"),
        name="mlp")(h, gain, w1, w2, gain_final)


def _pick(total, prefs):
    for t in prefs:
        if total % t == 0:
            return t
    raise ValueError(f"no tile in {prefs} divides {total}")


def kernel(x, mem, norm_mix, norm_xattn, norm_mem, norm_mlp, w_in, b_fox_f, b_mlstm_i,
           b_mlstm_f, b_gate, conv_w, conv_b, mlstm_norm, w_up_fox, w_up_mlstm, w_mix_out,
           w_xq, w_xkv, w_xo, w_mlp_in, w_mlp_out, norm_final):
    batch, seq, d = x.shape
    n_mem = mem.shape[1]
    depth = w_in.shape[0]
    assert d == D_MODEL and seq % ML_CHUNK == 0
    m = batch * seq

    tm_proj = _pick(m, (512, 256))
    tn_proj = 512
    t_seq = _pick(seq, (1024, 512, 256))
    tk = _pick(seq, (512, 256))
    tq = _pick(seq, (4 * tk, 2 * tk, tk))
    tm_x = _pick(seq, (512, 256))
    tm_mlp = _pick(m, (512, 256))
    tf = 1024

    sp = [0, FOX_W, 2 * FOX_W, 3 * FOX_W, 3 * FOX_W + FOX_HEADS]
    sp += [sp[-1] + 2 * ML_W, sp[-1] + 3 * ML_W, sp[-1] + 4 * ML_W]
    sp += [sp[-1] + ML_HEADS, sp[-1] + 2 * ML_HEADS, sp[-1] + 2 * ML_HEADS + 2 * D_MODEL]
    w_in_b = w_in.astype(BF16)
    seg = [w_in_b[:, :, a:b] for a, b in zip(sp[:-1], sp[1:])]
    fq, fk, fv, ff, mqk, mv, mo, mi, mf, gp = seg
    fq = (w_in[:, :, sp[0]:sp[1]] * (FOX_HD ** -0.5 * LOG2E)).astype(BF16)
    w_main = jnp.concatenate([gp, mqk, fq, fk, fv, mv, mo], axis=2)
    w_gate = jnp.concatenate(
        [ff, mi, mf, jnp.zeros((depth, d, LANES - GATE_USED), BF16)], axis=2)
    gate_bias = jnp.concatenate(
        [b_fox_f, b_mlstm_i, b_mlstm_f, jnp.zeros((depth, LANES - GATE_USED), F32)], axis=1)
    bf = lambda w: w.astype(BF16)
    w_up_fox, w_up_mlstm, w_mix_out = bf(w_up_fox), bf(w_up_mlstm), bf(w_mix_out)
    w_xq, w_xkv, w_xo = bf(w_xq), bf(w_xkv), bf(w_xo)
    w_mlp_in, w_mlp_out = bf(w_mlp_in), bf(w_mlp_out)

    h = x.reshape(m, d)
    mem2 = mem.reshape(batch * n_mem, d)
    for l in range(depth):
        proj, gpre = _in_proj(h, norm_mix[l][None], w_main, w_gate,
                              layer=l, tm=tm_proj, tn=tn_proj)
        rows, ct, cb = _gates(gpre, gate_bias[l][None], batch=batch, seq=seq, tg=t_seq)
        y_fox = _fox(proj, cb, batch=batch, seq=seq, tq=tq, tk=tk, npp=1)
        y_ml = _mlstm(proj, rows, ct, conv_w[l], conv_b[l][None], mlstm_norm[l][None],
                      batch=batch, seq=seq, tt=t_seq)
        kv = _norm_proj(mem2, norm_mem[l][None], w_xkv,
                        layer=l, tm=_pick(batch * n_mem, (1024, 512, 256)), tn=1024)
        h = _mix_xattn(y_fox, y_ml, proj, b_gate[l][None], h,
                       w_up_fox, w_up_mlstm, w_mix_out,
                       norm_xattn[l][None], w_xq, kv, w_xo,
                       layer=l, batch=batch, seq=seq, n_mem=n_mem, tm=tm_x)
        h = _mlp(h, norm_mlp[l][None], w_mlp_in, w_mlp_out, norm_final[None],
                 layer=l, tm=tm_mlp, tf=tf, final=(l == depth - 1))
    return h.reshape(batch, seq, d)
```

```python
import functools

import jax
import jax.numpy as jnp
import numpy as np
from jax import lax
from jax.experimental import pallas as pl
from jax.experimental.pallas import tpu as pltpu

F32 = jnp.float32
BF16 = jnp.bfloat16

D_MODEL = 1024
FOX_HEADS = 8
FOX_HD = 64
FOX_W = FOX_HEADS * FOX_HD
ML_HEADS = 4
ML_HD = 128
ML_W = ML_HEADS * ML_HD
CONV_K = 4
X_HEADS = 4
X_HD = D_MODEL // X_HEADS
D_FF = 4 * D_MODEL
EPS = 1e-6
LOG2E = 1.4426950408889634

LANES = 128
SUBLANES = 8
V7X_VMEM_LIMIT_BYTES = 56 * 1024 * 1024

G_OFF = 0
MQK_OFF = G_OFF + 2 * D_MODEL
FQ_OFF = MQK_OFF + 2 * ML_W
FK_OFF = FQ_OFF + FOX_W
FV_OFF = FK_OFF + FOX_W
MV_OFF = FV_OFF + FOX_W
MO_OFF = MV_OFF + ML_W
PROJ_W = MO_OFF + ML_W
GATE_FF = 0
GATE_MI = GATE_FF + FOX_HEADS
GATE_MF = GATE_MI + ML_HEADS
GATE_USED = GATE_MF + ML_HEADS

ML_CHUNK = 256
NT_DIMS = (((1,), (1,)), ((), ()))
TN_DIMS = (((0,), (0,)), ((), ()))


def _params(*sem):
    return pltpu.CompilerParams(dimension_semantics=sem,
                                vmem_limit_bytes=V7X_VMEM_LIMIT_BYTES)


def _rms(x, g):
    return x * lax.rsqrt(jnp.mean(x * x, axis=-1, keepdims=True) + EPS) * g


def _rms_split(x, g):
    r = lax.rsqrt(jnp.mean(x * x, axis=-1, keepdims=True) + EPS)
    return (x * g).astype(BF16), r


def _layer_block(shape, layer):
    zeros = (0,) * len(shape)
    return pl.BlockSpec((None,) + tuple(shape), lambda *_: (layer,) + zeros)


def _in_proj_kernel(x_ref, g_ref, w_ref, wg_ref, o_ref, og_ref, *, tn):
    xg, r = _rms_split(x_ref[...], g_ref[...])
    og_ref[...] = r * lax.dot_general(xg, wg_ref[...], NT_DIMS, preferred_element_type=F32)
    for j in range(w_ref.shape[0] // tn):
        cs = slice(j * tn, (j + 1) * tn)
        o_ref[:, cs] = (r * lax.dot_general(xg, w_ref[cs, :], NT_DIMS,
                                            preferred_element_type=F32)).astype(BF16)


def _in_proj(x, gain, w, wg, *, layer, tm, tn):
    m, d = x.shape
    n = w.shape[-2]
    const = lambda i: (0, 0)
    return pl.pallas_call(
        functools.partial(_in_proj_kernel, tn=tn), grid=(m // tm,),
        in_specs=[pl.BlockSpec((tm, d), lambda i: (i, 0)),
                  pl.BlockSpec((1, d), const),
                  _layer_block((n, d), layer),
                  _layer_block((LANES, d), layer)],
        out_specs=[pl.BlockSpec((tm, n), lambda i: (i, 0)),
                   pl.BlockSpec((tm, LANES), lambda i: (i, 0))],
        out_shape=[jax.ShapeDtypeStruct((m, n), BF16),
                   jax.ShapeDtypeStruct((m, LANES), F32)],
        compiler_params=_params("parallel"),
        name="norm_in_proj")(x, gain, w, wg)


def _norm_proj_kernel(x_ref, g_ref, w_ref, o_ref, xn_ref):
    @pl.when(pl.program_id(1) == 0)
    def _():
        xn_ref[...] = _rms(x_ref[...], g_ref[...]).astype(BF16)

    o_ref[...] = jnp.dot(xn_ref[...], w_ref[...],
                         preferred_element_type=F32).astype(BF16)


def _norm_proj(x, gain, w, *, layer, tm, tn):
    m, d = x.shape
    n = w.shape[-1]
    return pl.pallas_call(
        _norm_proj_kernel, grid=(m // tm, n // tn),
        in_specs=[pl.BlockSpec((tm, d), lambda i, j: (i, 0)),
                  pl.BlockSpec((1, d), lambda i, j: (0, 0)),
                  pl.BlockSpec((None, d, tn), lambda i, j: (layer, 0, j))],
        out_specs=pl.BlockSpec((tm, tn), lambda i, j: (i, j)),
        out_shape=jax.ShapeDtypeStruct((m, n), BF16),
        scratch_shapes=[pltpu.VMEM((tm, d), BF16)],
        compiler_params=_params("parallel", "arbitrary"),
        name="norm_proj")(x, gain, w)


def _split3(v):
    hi = v.astype(BF16)
    r1 = v - hi.astype(F32)
    mid = r1.astype(BF16)
    lo = (r1 - mid.astype(F32)).astype(BF16)
    return hi, mid, lo


def _gates_kernel(x_ref, b_ref, pm_ref, r_ref, ct_ref, cb_ref, carry_ref, *, tg):
    @pl.when(pl.program_id(1) == 0)
    def _():
        carry_ref[...] = jnp.zeros_like(carry_ref)

    c = ML_CHUNK
    row = lax.broadcasted_iota(jnp.int32, (c, c), 0)
    col = lax.broadcasted_iota(jnp.int32, (c, c), 1)
    causal = row >= col
    tri = jnp.where(causal, 1.0, 0.0).astype(BF16)
    lane = lax.broadcasted_iota(jnp.int32, (c, LANES), 1)
    is_fox = lane < GATE_MI
    is_ig = (lane >= GATE_MI) & (lane < GATE_MF)
    is_used = lane < GATE_USED
    bias = b_ref[...]
    for blk in range(tg // c):
        rs = slice(blk * c, (blk + 1) * c)
        x = x_ref[rs, :] + bias
        logsig = jnp.minimum(x, 0.0) - jnp.log1p(jnp.exp(-jnp.abs(x)))
        val = jnp.where(is_ig, x, logsig)
        val = jnp.where(is_used, val, 0.0)
        csp = jnp.dot(tri, jnp.concatenate(_split3(val), axis=1), preferred_element_type=F32)
        cs = csp[:, 0:LANES] + csp[:, LANES:2 * LANES] + csp[:, 2 * LANES:3 * LANES]
        glob = cs + carry_ref[0:1, :]
        carry_ref[0:1, :] = glob[c - 1:c, :]
        f_t = cs * LOG2E
        b_t = (pltpu.roll(val, GATE_MF - GATE_MI, axis=1) - cs) * LOG2E
        bt = b_t.T
        ct_ref[0, :, rs] = bt[:GATE_USED, :]
        cm = jnp.zeros((c, LANES), F32)
        for h in range(ML_HEADS):
            k = GATE_MF + h
            run_max = jnp.max(jnp.where(causal, bt[k:k + 1, :], -jnp.inf),
                              axis=-1, keepdims=True)
            cm = jnp.where(lane == k, run_max, cm)
        r_ref[rs, 0:LANES] = f_t
        r_ref[rs, LANES:2 * LANES] = b_t
        r_ref[rs, 2 * LANES:3 * LANES] = cm
        terms = _split3(jnp.where(is_fox, glob * (-LOG2E), 0.0))
        cb_ref[rs, :] = jnp.dot(jnp.concatenate(terms, axis=1), pm_ref[...],
                                preferred_element_type=F32).astype(BF16)


def _fox_bias_placement():
    pm = np.zeros((3, LANES, FOX_W), np.float32)
    for h in range(FOX_HEADS):
        base = (h // 2) * LANES + (FOX_HD if h % 2 == 0 else 0)
        for t in range(3):
            pm[t, h, base + t] = 1.0
    return jnp.asarray(pm.reshape(3 * LANES, FOX_W), BF16)


def _gates(gpre, bias, *, batch, seq, tg):
    ns = seq // tg
    rmap = lambda b, s: (b * ns + s, 0)
    return pl.pallas_call(
        functools.partial(_gates_kernel, tg=tg),
        grid=(batch, ns),
        in_specs=[pl.BlockSpec((tg, LANES), rmap),
                  pl.BlockSpec((1, LANES), lambda b, s: (0, 0)),
                  pl.BlockSpec((3 * LANES, FOX_W), lambda b, s: (0, 0))],
        out_specs=[pl.BlockSpec((tg, 3 * LANES), rmap),
                   pl.BlockSpec((1, GATE_USED, tg), lambda b, s: (b, 0, s)),
                   pl.BlockSpec((tg, FOX_W), rmap)],
        out_shape=[jax.ShapeDtypeStruct((batch * seq, 3 * LANES), F32),
                   jax.ShapeDtypeStruct((batch, GATE_USED, seq), F32),
                   jax.ShapeDtypeStruct((batch * seq, FOX_W), BF16)],
        scratch_shapes=[pltpu.VMEM((SUBLANES, LANES), F32)],
        compiler_params=_params("parallel", "arbitrary"),
        name="gates")(gpre, bias, _fox_bias_placement())


def _fox_lane_rows():
    r = np.zeros((SUBLANES, LANES), np.float32)
    r[0, :FOX_HD] = 1.0
    r[1, FOX_HD:] = 1.0
    r[2, FOX_HD:FOX_HD + 3] = 1.0
    r[3, 0:3] = 1.0
    r[4, FOX_HD] = 1.0
    r[5, 0] = 1.0
    return jnp.asarray(r, BF16)


def _fox_kernel(q_ref, k_ref, v_ref, cb_ref, mk_ref, o_ref, m_ref, acc_ref, *, tq, tk):
    i = pl.program_id(2)
    n_sub = tq // tk
    npp = q_ref.shape[1] // LANES
    lo, hi = mk_ref[0:1, :], mk_ref[1:2, :]
    one_a, one_b = mk_ref[4:5, :], mk_ref[5:6, :]
    qh = []
    for pp in range(npp):
        q2 = q_ref[:, pp * LANES:(pp + 1) * LANES]
        qh += [q2 * lo + mk_ref[2:3, :], q2 * hi + mk_ref[3:4, :]]

    tri = (lax.broadcasted_iota(jnp.int32, (tk, tk), 1)
           <= lax.broadcasted_iota(jnp.int32, (tk, tk), 0))

    def step(j, row0, masked, tk=tk, first=False):
        rows = tq - row0
        r0 = pl.multiple_of(j * tk, tk)
        for pp in range(npp):
            ps = slice(pp * LANES, (pp + 1) * LANES)
            ks = k_ref[pl.ds(r0, tk), ps]
            vs = v_ref[pl.ds(r0, tk), ps]
            cb = cb_ref[pl.ds(r0, tk), ps]
            kh = (ks * lo + cb * hi, ks * hi + cb * lo)
            vh = (vs * lo + one_a, vs * hi + one_b)
            for h in range(2):
                hh = 2 * pp + h
                s = lax.dot_general(qh[hh][row0:, :], kh[h], NT_DIMS,
                                    preferred_element_type=F32)
                if masked:
                    top = jnp.where(tri, s[:tk], -jnp.inf)
                    s = top if rows == tk else jnp.concatenate([top, s[tk:]], axis=0)
                s_max = jnp.max(s, axis=-1, keepdims=True)
                if first:
                    m_new = jnp.broadcast_to(s_max, (rows, LANES))
                else:
                    m_old = m_ref[hh, row0:, :]
                    m_new = jnp.maximum(m_old, s_max)
                pe = jnp.exp2(s - jnp.concatenate([m_new] * (tk // LANES), axis=1))
                pv = jnp.dot(pe.astype(BF16), vh[h], preferred_element_type=F32)
                if first:
                    acc_ref[hh, row0:, :] = pv
                else:
                    acc_ref[hh, row0:, :] = jnp.exp2(m_old - m_new) * acc_ref[hh, row0:, :] + pv
                m_ref[hh, row0:, :] = m_new

    tkf = 2 * tk

    def body(jj, carry):
        for u in range(tq // tkf):
            step(jj * (tq // tkf) + u, 0, False, tkf)
        return carry

    for d in range(n_sub):
        step(i * n_sub + d, d * tk, True, first=(d == 0))
    lax.fori_loop(0, i, body, 0)

    first = lax.broadcasted_iota(jnp.int32, (tq, LANES), 1) < FOX_HD
    for pp in range(npp):
        acc_a, acc_b = acc_ref[2 * pp], acc_ref[2 * pp + 1]
        l_a = jnp.broadcast_to(acc_a[:, FOX_HD:FOX_HD + 1], (tq, LANES))
        l_b = jnp.broadcast_to(acc_b[:, 0:1], (tq, LANES))
        o_ref[:, pp * LANES:(pp + 1) * LANES] = (
            jnp.where(first, acc_a, acc_b) / jnp.where(first, l_a, l_b)).astype(BF16)


def _fox(proj, cb, *, batch, seq, tq, tk, npp):
    nq = seq // tq
    ngrp = FOX_HEADS // 2 // npp
    w = npp * LANES
    qb, kb, vb = FQ_OFF // w, FK_OFF // w, FV_OFF // w
    return pl.pallas_call(
        functools.partial(_fox_kernel, tq=tq, tk=tk),
        grid=(batch, ngrp, nq),
        in_specs=[
            pl.BlockSpec((tq, w), lambda b, p, i: (b * nq + i, qb + p)),
            pl.BlockSpec((seq, w), lambda b, p, i: (b, kb + p)),
            pl.BlockSpec((seq, w), lambda b, p, i: (b, vb + p)),
            pl.BlockSpec((seq, w), lambda b, p, i: (b, p)),
            pl.BlockSpec((SUBLANES, LANES), lambda b, p, i: (0, 0)),
        ],
        out_specs=pl.BlockSpec((tq, w), lambda b, p, i: (b * nq + i, p)),
        out_shape=jax.ShapeDtypeStruct((batch * seq, FOX_W), BF16),
        scratch_shapes=[pltpu.VMEM((2 * npp, tq, LANES), F32),
                        pltpu.VMEM((2 * npp, tq, LANES), F32)],
        compiler_params=_params("parallel", "parallel", "arbitrary"),
        name="fox_attention")(proj, proj, proj, cb, _fox_lane_rows())


def _mlstm_kernel(u_ref, v_ref, mo_ref, r_ref, ct_ref, sel_ref, cw_ref, cb_ref,
                  nw_ref, o_ref, ubuf, q_s, k_s, st_ref, m_ref, *, tt):
    s_idx = pl.program_id(1)
    halo = SUBLANES

    @pl.when(s_idx == 0)
    def _():
        ubuf[0:halo, :] = jnp.zeros((halo, 2 * ML_W), F32)
        st_ref[...] = jnp.zeros_like(st_ref)
        m_ref[...] = jnp.zeros_like(m_ref)

    @pl.when(s_idx > 0)
    def _():
        ubuf[0:halo, :] = ubuf[tt:tt + halo, :]

    c = ML_CHUNK
    for blk in range(tt // c):
        ubuf[halo + blk * c:halo + (blk + 1) * c, :] = (
            u_ref[blk * c:(blk + 1) * c, :].astype(F32))

    def conv_block(blk):
        acc = cb_ref[...]
        for j in (CONV_K - 1,) + tuple(range(CONV_K - 1)):
            off = halo - (CONV_K - 1) + j + blk * c
            acc = acc + cw_ref[j:j + 1, :] * ubuf[off:off + c, :]
        qk = acc * jax.nn.sigmoid(acc)
        q_s[blk * c:(blk + 1) * c, :] = qk[:, :ML_W].astype(BF16)
        k_s[blk * c:(blk + 1) * c, :] = (qk[:, ML_W:] * (ML_HD ** -0.5)).astype(BF16)

    row = lax.broadcasted_iota(jnp.int32, (c, c), 0)
    col = lax.broadcasted_iota(jnp.int32, (c, c), 1)
    causal = col <= row
    ones_blk = jnp.ones((c, ML_HD), BF16)

    def chunk(ci):
        r0 = ci * c
        cb = ct_ref[0, :, pl.ds(r0, c)]
        rep = []
        for qi in range(3):
            hi, mid, lo = _split3(r_ref[pl.ds(r0, c), qi * LANES:(qi + 1) * LANES])
            rep.append(jnp.dot(jnp.concatenate([hi, mid], axis=1), sel_ref[...],
                               preferred_element_type=F32)
                       + jnp.dot(lo, sel_ref[0:LANES, :], preferred_element_type=F32))
        for h in range(ML_HEADS):
            hs = slice(h * ML_HD, (h + 1) * ML_HD)
            qh = q_s[pl.ds(r0, c), hs]
            kh = k_s[pl.ds(r0, c), hs]
            vh = v_ref[pl.ds(r0, c), hs]
            f_t, b_t, cm_t = rep[0][:, hs], rep[1][:, hs], rep[2][:, hs]
            b_row = cb[GATE_MF + h:GATE_MF + h + 1, :]
            m_prev = m_ref[h:h + 1, :]

            g_t = jnp.maximum(m_prev, cm_t)
            dw = jnp.exp2(jnp.where(causal, b_row - jnp.concatenate([g_t, g_t], axis=1),
                                    -jnp.inf))
            inter_w = jnp.exp2(m_prev - g_t)
            sqk = lax.dot_general(qh, kh, NT_DIMS, preferred_element_type=F32) * dw
            state = st_ref[h]
            qc = jnp.dot(qh, state.astype(BF16), preferred_element_type=F32)
            v_aug = jnp.concatenate([vh, ones_blk], axis=1)
            pv = jnp.dot(sqk.astype(BF16), v_aug, preferred_element_type=F32)
            num = inter_w * qc[:, :ML_HD] + pv[:, :ML_HD]
            den = inter_w * qc[:, ML_HD:] + pv[:, ML_HD:]
            hh = num / jnp.maximum(jnp.abs(den), jnp.exp2(-(f_t + g_t)))
            ms = jnp.mean(hh * hh, axis=-1, keepdims=True)
            y = (hh * lax.rsqrt(ms + EPS) * nw_ref[:, hs]
                 * jax.nn.sigmoid(mo_ref[pl.ds(r0, c), hs].astype(F32)))
            o_ref[pl.ds(r0, c), hs] = y.astype(BF16)

            f_last = f_t[c - 1:c, :]
            m_new = jnp.maximum(f_last + m_prev, f_last + cm_t[c - 1:c, :])
            s_old = jnp.exp2(f_last + m_prev - m_new)
            kw = (kh.astype(F32) * jnp.exp2(f_last + b_t - m_new)).astype(BF16)
            upd = lax.dot_general(kw, v_aug, TN_DIMS, preferred_element_type=F32)
            st_ref[h] = jnp.concatenate([s_old, s_old], axis=1) * state + upd
            m_ref[h:h + 1, :] = m_new

    for ci in range(tt // c):
        conv_block(ci)
        chunk(ci)


def _mlstm(proj, rows, ct, conv_w, conv_b, norm_w, *, batch, seq, tt):
    ns = seq // tt
    ub = MQK_OFF // (2 * ML_W)
    vb = MV_OFF // ML_W
    ob = MO_OFF // ML_W
    rmap = lambda b, s: (b * ns + s, 0)
    sel = np.zeros((2, LANES, ML_W), np.float32)
    for h in range(ML_HEADS):
        sel[:, GATE_MF + h, h * ML_HD:(h + 1) * ML_HD] = 1.0
    sel = jnp.asarray(sel.reshape(2 * LANES, ML_W), BF16)
    return pl.pallas_call(
        functools.partial(_mlstm_kernel, tt=tt),
        grid=(batch, ns),
        in_specs=[
            pl.BlockSpec((tt, 2 * ML_W), lambda b, s: (b * ns + s, ub)),
            pl.BlockSpec((tt, ML_W), lambda b, s: (b * ns + s, vb)),
            pl.BlockSpec((tt, ML_W), lambda b, s: (b * ns + s, ob)),
            pl.BlockSpec((tt, 3 * LANES), rmap),
            pl.BlockSpec((1, GATE_USED, tt), lambda b, s: (b, 0, s)),
            pl.BlockSpec((2 * LANES, ML_W), lambda b, s: (0, 0)),
            pl.BlockSpec((CONV_K, 2 * ML_W), lambda b, s: (0, 0)),
            pl.BlockSpec((1, 2 * ML_W), lambda b, s: (0, 0)),
            pl.BlockSpec((1, ML_W), lambda b, s: (0, 0)),
        ],
        out_specs=pl.BlockSpec((tt, ML_W), rmap),
        out_shape=jax.ShapeDtypeStruct((batch * seq, ML_W), BF16),
        scratch_shapes=[pltpu.VMEM((tt + SUBLANES, 2 * ML_W), F32),
                        pltpu.VMEM((tt, ML_W), BF16),
                        pltpu.VMEM((tt, ML_W), BF16),
                        pltpu.VMEM((ML_HEADS, ML_HD, 2 * ML_HD), F32),
                        pltpu.VMEM((SUBLANES, LANES), F32)],
        compiler_params=_params("parallel", "arbitrary"),
        name="mlstm")(proj, proj, proj, rows, ct, sel, conv_w, conv_b, norm_w)


def _mix_xattn_kernel(yf_ref, ym_ref, gt_ref, bg_ref, h_ref, wf_ref, wm_ref, wmix_ref,
                      g_ref, wq_ref, k_ref, v_ref, wo_ref, o_ref):
    g = jax.nn.sigmoid(gt_ref[...].astype(F32) + bg_ref[...])
    uf = jnp.dot(yf_ref[...], wf_ref[...], preferred_element_type=F32)
    um = jnp.dot(ym_ref[...], wm_ref[...], preferred_element_type=F32)
    merged = g[:, :D_MODEL] * uf + g[:, D_MODEL:] * um
    x = h_ref[...] + jnp.dot(merged.astype(BF16), wmix_ref[...], preferred_element_type=F32)

    xg, r = _rms_split(x, g_ref[...])
    q = jnp.dot(xg, wq_ref[...], preferred_element_type=F32)
    q = (q * (r * (X_HD ** -0.5 * LOG2E))).astype(BF16)
    outs = []
    for hd in range(X_HEADS):
        hs = slice(hd * X_HD, (hd + 1) * X_HD)
        s = lax.dot_general(q[:, hs], k_ref[:, hs], NT_DIMS, preferred_element_type=F32)
        e = jnp.exp2(s - jnp.max(s, axis=-1, keepdims=True))
        outs.append(jnp.dot(e.astype(BF16), v_ref[:, hs], preferred_element_type=F32)
                    / jnp.sum(e, axis=-1, keepdims=True))
    o = jnp.concatenate(outs, axis=1).astype(BF16)
    o_ref[...] = x + jnp.dot(o, wo_ref[...], preferred_element_type=F32)


def _mix_xattn(yf, ym, proj, b_gate, h, wf, wm, wmix, gain, wq, kv, wo, *,
               layer, batch, seq, n_mem, tm):
    ns = seq // tm
    const = lambda b, s: (0, 0)
    row = lambda b, s: (b * ns + s, 0)
    return pl.pallas_call(
        _mix_xattn_kernel, grid=(batch, ns),
        in_specs=[
            pl.BlockSpec((tm, FOX_W), row),
            pl.BlockSpec((tm, ML_W), row),
            pl.BlockSpec((tm, 2 * D_MODEL), lambda b, s: (b * ns + s, G_OFF // (2 * D_MODEL))),
            pl.BlockSpec((1, 2 * D_MODEL), const),
            pl.BlockSpec((tm, D_MODEL), row),
            _layer_block((FOX_W, D_MODEL), layer),
            _layer_block((ML_W, D_MODEL), layer),
            _layer_block((D_MODEL, D_MODEL), layer),
            pl.BlockSpec((1, D_MODEL), const),
            _layer_block((D_MODEL, D_MODEL), layer),
            pl.BlockSpec((n_mem, D_MODEL), lambda b, s: (b, 0)),
            pl.BlockSpec((n_mem, D_MODEL), lambda b, s: (b, 1)),
            _layer_block((D_MODEL, D_MODEL), layer),
        ],
        out_specs=pl.BlockSpec((tm, D_MODEL), row),
        out_shape=jax.ShapeDtypeStruct((batch * seq, D_MODEL), F32),
        compiler_params=_params("parallel", "parallel"),
        name="mix_cross_attention")(yf, ym, proj, b_gate, h, wf, wm, wmix,
                                    gain, wq, kv, kv, wo)


def _mlp_kernel(h_ref, g_ref, w1_ref, w2_ref, gf_ref, o_ref, *, tf, final):
    x = h_ref[...]
    xg, r = _rms_split(x, g_ref[...])
    acc = x
    for j in range(D_FF // tf):
        fs = slice(j * tf, (j + 1) * tf)
        a = jnp.maximum(r * jnp.dot(xg, w1_ref[:, fs], preferred_element_type=F32), 0.0)
        acc = acc + jnp.dot((a * a).astype(BF16), w2_ref[fs, :], preferred_element_type=F32)
    o_ref[...] = _rms(acc, gf_ref[...]) if final else acc


def _mlp(h, gain, w1, w2, gain_final, *, layer, tm, tf, final):
    m = h.shape[0]
    const = lambda i: (0, 0)
    row = lambda i: (i, 0)
    return pl.pallas_call(
        functools.partial(_mlp_kernel, tf=tf, final=final),
        grid=(m // tm,),
        in_specs=[
            pl.BlockSpec((tm, D_MODEL), row),
            pl.BlockSpec((1, D_MODEL), const),
            _layer_block((D_MODEL, D_FF), layer),
            _layer_block((D_FF, D_MODEL), layer),
            pl.BlockSpec((1, D_MODEL), const),
        ],
        out_specs=pl.BlockSpec((tm, D_MODEL), row),
        out_shape=jax.ShapeDtypeStruct((m, D_MODEL), F32),
        compiler_params=_params("parallel"),
        name="mlp")(h, gain, w1, w2, gain_final)


def _pick(total, prefs):
    for t in prefs:
        if total % t == 0:
            return t
    raise ValueError(f"no tile in {prefs} divides {total}")


def kernel(x, mem, norm_mix, norm_xattn, norm_mem, norm_mlp, w_in, b_fox_f, b_mlstm_i,
           b_mlstm_f, b_gate, conv_w, conv_b, mlstm_norm, w_up_fox, w_up_mlstm, w_mix_out,
           w_xq, w_xkv, w_xo, w_mlp_in, w_mlp_out, norm_final):
    batch, seq, d = x.shape
    n_mem = mem.shape[1]
    depth = w_in.shape[0]
    assert d == D_MODEL and seq % ML_CHUNK == 0
    m = batch * seq

    tm_proj = _pick(m, (512, 256))
    tn_proj = 512
    t_seq = _pick(seq, (1024, 512, 256))
    t_gates = _pick(seq, (2048, 1024, 512, 256))
    tk = _pick(seq, (512, 256))
    tq = _pick(seq, (4 * tk, 2 * tk, tk))
    tm_x = _pick(seq, (512, 256))
    tm_mlp = _pick(m, (512, 256))
    tf = 1024

    sp = [0, FOX_W, 2 * FOX_W, 3 * FOX_W, 3 * FOX_W + FOX_HEADS]
    sp += [sp[-1] + 2 * ML_W, sp[-1] + 3 * ML_W, sp[-1] + 4 * ML_W]
    sp += [sp[-1] + ML_HEADS, sp[-1] + 2 * ML_HEADS, sp[-1] + 2 * ML_HEADS + 2 * D_MODEL]
    w_in_t = jnp.swapaxes(w_in, 1, 2)
    w_in_b = w_in_t.astype(BF16)
    seg = [w_in_b[:, a:b, :] for a, b in zip(sp[:-1], sp[1:])]
    fq, fk, fv, ff, mqk, mv, mo, mi, mf, gp = seg
    fq = (w_in_t[:, sp[0]:sp[1], :] * (FOX_HD ** -0.5 * LOG2E)).astype(BF16)
    w_main = jnp.concatenate([gp, mqk, fq, fk, fv, mv, mo], axis=1)
    w_gate = jnp.concatenate(
        [ff, mi, mf, jnp.zeros((depth, LANES - GATE_USED, d), BF16)], axis=1)
    gate_bias = jnp.concatenate(
        [b_fox_f, b_mlstm_i, b_mlstm_f, jnp.zeros((depth, LANES - GATE_USED), F32)], axis=1)
    bf = lambda w: w.astype(BF16)
    w_up_fox, w_up_mlstm, w_mix_out = bf(w_up_fox), bf(w_up_mlstm), bf(w_mix_out)
    w_xq, w_xkv, w_xo = bf(w_xq), bf(w_xkv), bf(w_xo)
    w_mlp_in, w_mlp_out = bf(w_mlp_in), bf(w_mlp_out)

    h = x.reshape(m, d)
    mem2 = mem.reshape(batch * n_mem, d)
    for l in range(depth):
        proj, gpre = _in_proj(h, norm_mix[l][None], w_main, w_gate,
                              layer=l, tm=tm_proj, tn=tn_proj)
        rows, ct, cb = _gates(gpre, gate_bias[l][None], batch=batch, seq=seq, tg=t_gates)
        y_fox = _fox(proj, cb, batch=batch, seq=seq, tq=tq, tk=tk, npp=1)
        y_ml = _mlstm(proj, rows, ct, conv_w[l], conv_b[l][None], mlstm_norm[l][None],
                      batch=batch, seq=seq, tt=t_seq)
        kv = _norm_proj(mem2, norm_mem[l][None], w_xkv,
                        layer=l, tm=_pick(batch * n_mem, (1024, 512, 256)), tn=1024)
        h = _mix_xattn(y_fox, y_ml, proj, b_gate[l][None], h,
                       w_up_fox, w_up_mlstm, w_mix_out,
                       norm_xattn[l][None], w_xq, kv, w_xo,
                       layer=l, batch=batch, seq=seq, n_mem=n_mem, tm=tm_x)
        h = _mlp(h, norm_mlp[l][None], w_mlp_in, w_mlp_out, norm_final[None],
                 layer=l, tm=tm_mlp, tf=tf, final=(l == depth - 1))
    return h.reshape(batch, seq, d)
```

```python
import functools

import jax
import jax.numpy as jnp
import numpy as np
from jax import lax
from jax.experimental import pallas as pl
from jax.experimental.pallas import tpu as pltpu

F32 = jnp.float32
BF16 = jnp.bfloat16

D_MODEL = 1024
FOX_HEADS = 8
FOX_HD = 64
FOX_W = FOX_HEADS * FOX_HD
ML_HEADS = 4
ML_HD = 128
ML_W = ML_HEADS * ML_HD
CONV_K = 4
X_HEADS = 4
X_HD = D_MODEL // X_HEADS
D_FF = 4 * D_MODEL
EPS = 1e-6
LOG2E = 1.4426950408889634

LANES = 128
SUBLANES = 8
V7X_VMEM_LIMIT_BYTES = 56 * 1024 * 1024

G_OFF = 0
MQK_OFF = G_OFF + 2 * D_MODEL
FQ_OFF = MQK_OFF + 2 * ML_W
FK_OFF = FQ_OFF + FOX_W
FV_OFF = FK_OFF + FOX_W
MV_OFF = FV_OFF + FOX_W
MO_OFF = MV_OFF + ML_W
PROJ_W = MO_OFF + ML_W
GATE_FF = 0
GATE_MI = GATE_FF + FOX_HEADS
GATE_MF = GATE_MI + ML_HEADS
GATE_USED = GATE_MF + ML_HEADS

ML_CHUNK = 256
NT_DIMS = (((1,), (1,)), ((), ()))
TN_DIMS = (((0,), (0,)), ((), ()))


def _params(*sem):
    return pltpu.CompilerParams(dimension_semantics=sem,
                                vmem_limit_bytes=V7X_VMEM_LIMIT_BYTES)


def _rms(x, g):
    return x * lax.rsqrt(jnp.mean(x * x, axis=-1, keepdims=True) + EPS) * g


def _rms_split(x, g):
    r = lax.rsqrt(jnp.mean(x * x, axis=-1, keepdims=True) + EPS)
    return (x * g).astype(BF16), r


def _layer_block(shape, layer):
    zeros = (0,) * len(shape)
    return pl.BlockSpec((None,) + tuple(shape), lambda *_: (layer,) + zeros)


def _in_proj_kernel(x_ref, g_ref, w_ref, wg_ref, o_ref, og_ref, *, tn):
    xg, r = _rms_split(x_ref[...], g_ref[...])
    og_ref[...] = r * lax.dot_general(xg, wg_ref[...], NT_DIMS, preferred_element_type=F32)
    for j in range(w_ref.shape[0] // tn):
        cs = slice(j * tn, (j + 1) * tn)
        o_ref[:, cs] = (r * lax.dot_general(xg, w_ref[cs, :], NT_DIMS,
                                            preferred_element_type=F32)).astype(BF16)


def _in_proj(x, gain, w, wg, *, layer, tm, tn):
    m, d = x.shape
    n = w.shape[-2]
    const = lambda i: (0, 0)
    return pl.pallas_call(
        functools.partial(_in_proj_kernel, tn=tn), grid=(m // tm,),
        in_specs=[pl.BlockSpec((tm, d), lambda i: (i, 0)),
                  pl.BlockSpec((1, d), const),
                  _layer_block((n, d), layer),
                  _layer_block((LANES, d), layer)],
        out_specs=[pl.BlockSpec((tm, n), lambda i: (i, 0)),
                   pl.BlockSpec((tm, LANES), lambda i: (i, 0))],
        out_shape=[jax.ShapeDtypeStruct((m, n), BF16),
                   jax.ShapeDtypeStruct((m, LANES), F32)],
        compiler_params=_params("parallel"),
        name="norm_in_proj")(x, gain, w, wg)


def _norm_proj_kernel(x_ref, g_ref, w_ref, o_ref, xn_ref):
    @pl.when(pl.program_id(1) == 0)
    def _():
        xn_ref[...] = _rms(x_ref[...], g_ref[...]).astype(BF16)

    o_ref[...] = jnp.dot(xn_ref[...], w_ref[...],
                         preferred_element_type=F32).astype(BF16)


def _norm_proj(x, gain, w, *, layer, tm, tn):
    m, d = x.shape
    n = w.shape[-1]
    return pl.pallas_call(
        _norm_proj_kernel, grid=(m // tm, n // tn),
        in_specs=[pl.BlockSpec((tm, d), lambda i, j: (i, 0)),
                  pl.BlockSpec((1, d), lambda i, j: (0, 0)),
                  pl.BlockSpec((None, d, tn), lambda i, j: (layer, 0, j))],
        out_specs=pl.BlockSpec((tm, tn), lambda i, j: (i, j)),
        out_shape=jax.ShapeDtypeStruct((m, n), BF16),
        scratch_shapes=[pltpu.VMEM((tm, d), BF16)],
        compiler_params=_params("parallel", "arbitrary"),
        name="norm_proj")(x, gain, w)


def _split3(v):
    hi = v.astype(BF16)
    r1 = v - hi.astype(F32)
    mid = r1.astype(BF16)
    lo = (r1 - mid.astype(F32)).astype(BF16)
    return hi, mid, lo


def _gates_kernel(x_ref, b_ref, pm_ref, r_ref, ct_ref, cb_ref, carry_ref, *, tg):
    @pl.when(pl.program_id(1) == 0)
    def _():
        carry_ref[...] = jnp.zeros_like(carry_ref)

    c = ML_CHUNK
    row = lax.broadcasted_iota(jnp.int32, (c, c), 0)
    col = lax.broadcasted_iota(jnp.int32, (c, c), 1)
    causal = row >= col
    tri = jnp.where(causal, 1.0, 0.0).astype(BF16)
    lane = lax.broadcasted_iota(jnp.int32, (c, LANES), 1)
    is_fox = lane < GATE_MI
    is_ig = (lane >= GATE_MI) & (lane < GATE_MF)
    is_used = lane < GATE_USED
    bias = b_ref[...]
    for blk in range(tg // c):
        rs = slice(blk * c, (blk + 1) * c)
        x = x_ref[rs, :] + bias
        logsig = jnp.minimum(x, 0.0) - jnp.log1p(jnp.exp(-jnp.abs(x)))
        val = jnp.where(is_ig, x, logsig)
        val = jnp.where(is_used, val, 0.0)
        csp = jnp.dot(tri, jnp.concatenate(_split3(val), axis=1), preferred_element_type=F32)
        cs = csp[:, 0:LANES] + csp[:, LANES:2 * LANES] + csp[:, 2 * LANES:3 * LANES]
        glob = cs + carry_ref[0:1, :]
        carry_ref[0:1, :] = glob[c - 1:c, :]
        f_t = cs * LOG2E
        b_t = (pltpu.roll(val, GATE_MF - GATE_MI, axis=1) - cs) * LOG2E
        bt = b_t.T
        ct_ref[0, :, rs] = bt[:GATE_USED, :]
        cm = jnp.zeros((c, LANES), F32)
        for h in range(ML_HEADS):
            k = GATE_MF + h
            run_max = jnp.max(jnp.where(causal, bt[k:k + 1, :], -jnp.inf),
                              axis=-1, keepdims=True)
            cm = jnp.where(lane == k, run_max, cm)
        r_ref[rs, 0:LANES] = f_t
        r_ref[rs, LANES:2 * LANES] = b_t
        r_ref[rs, 2 * LANES:3 * LANES] = cm
        terms = _split3(jnp.where(is_fox, glob * (-LOG2E), 0.0))
        cb_ref[rs, :] = jnp.dot(jnp.concatenate(terms, axis=1), pm_ref[...],
                                preferred_element_type=F32).astype(BF16)


def _fox_bias_placement():
    pm = np.zeros((3, LANES, FOX_W), np.float32)
    for h in range(FOX_HEADS):
        base = (h // 2) * LANES + (FOX_HD if h % 2 == 0 else 0)
        for t in range(3):
            pm[t, h, base + t] = 1.0
    return jnp.asarray(pm.reshape(3 * LANES, FOX_W), BF16)


def _gates(gpre, bias, *, batch, seq, tg):
    ns = seq // tg
    rmap = lambda b, s: (b * ns + s, 0)
    return pl.pallas_call(
        functools.partial(_gates_kernel, tg=tg),
        grid=(batch, ns),
        in_specs=[pl.BlockSpec((tg, LANES), rmap),
                  pl.BlockSpec((1, LANES), lambda b, s: (0, 0)),
                  pl.BlockSpec((3 * LANES, FOX_W), lambda b, s: (0, 0))],
        out_specs=[pl.BlockSpec((tg, 3 * LANES), rmap),
                   pl.BlockSpec((1, GATE_USED, tg), lambda b, s: (b, 0, s)),
                   pl.BlockSpec((tg, FOX_W), rmap)],
        out_shape=[jax.ShapeDtypeStruct((batch * seq, 3 * LANES), F32),
                   jax.ShapeDtypeStruct((batch, GATE_USED, seq), F32),
                   jax.ShapeDtypeStruct((batch * seq, FOX_W), BF16)],
        scratch_shapes=[pltpu.VMEM((SUBLANES, LANES), F32)],
        compiler_params=_params("parallel", "arbitrary"),
        name="gates")(gpre, bias, _fox_bias_placement())


def _fox_lane_rows():
    r = np.zeros((SUBLANES, LANES), np.float32)
    r[0, :FOX_HD] = 1.0
    r[1, FOX_HD:] = 1.0
    r[2, FOX_HD:FOX_HD + 3] = 1.0
    r[3, 0:3] = 1.0
    r[4, FOX_HD] = 1.0
    r[5, 0] = 1.0
    return jnp.asarray(r, BF16)


def _fox_kernel(q_ref, k_ref, v_ref, cb_ref, mk_ref, o_ref, m_ref, acc_ref, *, tq, tk):
    i = pl.program_id(2)
    n_sub = tq // tk
    npp = q_ref.shape[1] // LANES
    lo, hi = mk_ref[0:1, :], mk_ref[1:2, :]
    one_a, one_b = mk_ref[4:5, :], mk_ref[5:6, :]
    qh = []
    for pp in range(npp):
        q2 = q_ref[:, pp * LANES:(pp + 1) * LANES]
        qh += [q2 * lo + mk_ref[2:3, :], q2 * hi + mk_ref[3:4, :]]

    tri = (lax.broadcasted_iota(jnp.int32, (tk, tk), 1)
           <= lax.broadcasted_iota(jnp.int32, (tk, tk), 0))

    def step(j, row0, masked, tk=tk, first=False):
        rows = tq - row0
        r0 = pl.multiple_of(j * tk, tk)
        for pp in range(npp):
            ps = slice(pp * LANES, (pp + 1) * LANES)
            ks = k_ref[pl.ds(r0, tk), ps]
            vs = v_ref[pl.ds(r0, tk), ps]
            cb = cb_ref[pl.ds(r0, tk), ps]
            kh = (ks * lo + cb * hi, ks * hi + cb * lo)
            vh = (vs * lo + one_a, vs * hi + one_b)
            for h in range(2):
                hh = 2 * pp + h
                s = lax.dot_general(qh[hh][row0:, :], kh[h], NT_DIMS,
                                    preferred_element_type=F32)
                if masked:
                    top = jnp.where(tri, s[:tk], -jnp.inf)
                    s = top if rows == tk else jnp.concatenate([top, s[tk:]], axis=0)
                s_max = jnp.max(s, axis=-1, keepdims=True)
                if first:
                    m_new = jnp.broadcast_to(s_max, (rows, LANES))
                else:
                    m_old = m_ref[hh, row0:, :]
                    m_new = jnp.maximum(m_old, s_max)
                pe = jnp.exp2(s - jnp.concatenate([m_new] * (tk // LANES), axis=1))
                pv = jnp.dot(pe.astype(BF16), vh[h], preferred_element_type=F32)
                if first:
                    acc_ref[hh, row0:, :] = pv
                else:
                    acc_ref[hh, row0:, :] = jnp.exp2(m_old - m_new) * acc_ref[hh, row0:, :] + pv
                m_ref[hh, row0:, :] = m_new

    tkf = 2 * tk if tq % (2 * tk) == 0 else tk

    def body(jj, carry):
        for u in range(tq // tkf):
            step(jj * (tq // tkf) + u, 0, False, tkf)
        return carry

    for d in range(n_sub):
        step(i * n_sub + d, d * tk, True, first=(d == 0))
    lax.fori_loop(0, i, body, 0)

    first = lax.broadcasted_iota(jnp.int32, (tq, LANES), 1) < FOX_HD
    for pp in range(npp):
        acc_a, acc_b = acc_ref[2 * pp], acc_ref[2 * pp + 1]
        l_a = jnp.broadcast_to(acc_a[:, FOX_HD:FOX_HD + 1], (tq, LANES))
        l_b = jnp.broadcast_to(acc_b[:, 0:1], (tq, LANES))
        o_ref[:, pp * LANES:(pp + 1) * LANES] = (
            jnp.where(first, acc_a, acc_b) / jnp.where(first, l_a, l_b)).astype(BF16)


def _fox(proj, cb, *, batch, seq, tq, tk, npp):
    nq = seq // tq
    ngrp = FOX_HEADS // 2 // npp
    w = npp * LANES
    qb, kb, vb = FQ_OFF // w, FK_OFF // w, FV_OFF // w
    return pl.pallas_call(
        functools.partial(_fox_kernel, tq=tq, tk=tk),
        grid=(batch, ngrp, nq),
        in_specs=[
            pl.BlockSpec((tq, w), lambda b, p, i: (b * nq + i, qb + p)),
            pl.BlockSpec((seq, w), lambda b, p, i: (b, kb + p)),
            pl.BlockSpec((seq, w), lambda b, p, i: (b, vb + p)),
            pl.BlockSpec((seq, w), lambda b, p, i: (b, p)),
            pl.BlockSpec((SUBLANES, LANES), lambda b, p, i: (0, 0)),
        ],
        out_specs=pl.BlockSpec((tq, w), lambda b, p, i: (b * nq + i, p)),
        out_shape=jax.ShapeDtypeStruct((batch * seq, FOX_W), BF16),
        scratch_shapes=[pltpu.VMEM((2 * npp, tq, LANES), F32),
                        pltpu.VMEM((2 * npp, tq, LANES), F32)],
        compiler_params=_params("parallel", "parallel", "arbitrary"),
        name="fox_attention")(proj, proj, proj, cb, _fox_lane_rows())


def _mlstm_kernel(u_ref, v_ref, mo_ref, r_ref, ct_ref, sel_ref, cw_ref, cb_ref,
                  nw_ref, o_ref, ubuf, q_s, k_s, st_ref, m_ref, *, tt):
    s_idx = pl.program_id(1)
    halo = SUBLANES

    @pl.when(s_idx == 0)
    def _():
        ubuf[0:halo, :] = jnp.zeros((halo, 2 * ML_W), F32)
        st_ref[...] = jnp.zeros_like(st_ref)
        m_ref[...] = jnp.zeros_like(m_ref)

    @pl.when(s_idx > 0)
    def _():
        ubuf[0:halo, :] = ubuf[tt:tt + halo, :]

    c = ML_CHUNK
    for blk in range(tt // c):
        ubuf[halo + blk * c:halo + (blk + 1) * c, :] = (
            u_ref[blk * c:(blk + 1) * c, :].astype(F32))

    def conv_block(blk):
        acc = cb_ref[...]
        for j in (CONV_K - 1,) + tuple(range(CONV_K - 1)):
            off = halo - (CONV_K - 1) + j + blk * c
            acc = acc + cw_ref[j:j + 1, :] * ubuf[off:off + c, :]
        qk = acc * jax.nn.sigmoid(acc)
        q_s[blk * c:(blk + 1) * c, :] = qk[:, :ML_W].astype(BF16)
        k_s[blk * c:(blk + 1) * c, :] = (qk[:, ML_W:] * (ML_HD ** -0.5)).astype(BF16)

    row = lax.broadcasted_iota(jnp.int32, (c, c), 0)
    col = lax.broadcasted_iota(jnp.int32, (c, c), 1)
    causal = col <= row
    ones_blk = jnp.ones((c, ML_HD), BF16)

    def chunk(ci):
        r0 = ci * c
        cb = ct_ref[0, :, pl.ds(r0, c)]
        rep = []
        for qi in range(3):
            hi, mid, lo = _split3(r_ref[pl.ds(r0, c), qi * LANES:(qi + 1) * LANES])
            rep.append(jnp.dot(jnp.concatenate([hi, mid], axis=1), sel_ref[...],
                               preferred_element_type=F32)
                       + jnp.dot(lo, sel_ref[0:LANES, :], preferred_element_type=F32))
        for h in range(ML_HEADS):
            hs = slice(h * ML_HD, (h + 1) * ML_HD)
            qh = q_s[pl.ds(r0, c), hs]
            kh = k_s[pl.ds(r0, c), hs]
            vh = v_ref[pl.ds(r0, c), hs]
            f_t, b_t, cm_t = rep[0][:, hs], rep[1][:, hs], rep[2][:, hs]
            b_row = cb[GATE_MF + h:GATE_MF + h + 1, :]
            m_prev = m_ref[h:h + 1, :]

            g_t = jnp.maximum(m_prev, cm_t)
            dw = jnp.exp2(jnp.where(causal, b_row - jnp.concatenate([g_t, g_t], axis=1),
                                    -jnp.inf))
            inter_w = jnp.exp2(m_prev - g_t)
            sqk = lax.dot_general(qh, kh, NT_DIMS, preferred_element_type=F32) * dw
            state = st_ref[h]
            qc = jnp.dot(qh, state.astype(BF16), preferred_element_type=F32)
            v_aug = jnp.concatenate([vh, ones_blk], axis=1)
            pv = jnp.dot(sqk.astype(BF16), v_aug, preferred_element_type=F32)
            num = inter_w * qc[:, :ML_HD] + pv[:, :ML_HD]
            den = inter_w * qc[:, ML_HD:] + pv[:, ML_HD:]
            hh = num / jnp.maximum(jnp.abs(den), jnp.exp2(-(f_t + g_t)))
            ms = jnp.mean(hh * hh, axis=-1, keepdims=True)
            y = (hh * lax.rsqrt(ms + EPS) * nw_ref[:, hs]
                 * jax.nn.sigmoid(mo_ref[pl.ds(r0, c), hs].astype(F32)))
            o_ref[pl.ds(r0, c), hs] = y.astype(BF16)

            f_last = f_t[c - 1:c, :]
            m_new = jnp.maximum(f_last + m_prev, f_last + cm_t[c - 1:c, :])
            s_old = jnp.exp2(f_last + m_prev - m_new)
            kw = (kh.astype(F32) * jnp.exp2(f_last + b_t - m_new)).astype(BF16)
            upd = lax.dot_general(kw, v_aug, TN_DIMS, preferred_element_type=F32)
            st_ref[h] = jnp.concatenate([s_old, s_old], axis=1) * state + upd
            m_ref[h:h + 1, :] = m_new

    for ci in range(tt // c):
        conv_block(ci)
        chunk(ci)


def _mlstm(proj, rows, ct, conv_w, conv_b, norm_w, *, batch, seq, tt):
    ns = seq // tt
    ub = MQK_OFF // (2 * ML_W)
    vb = MV_OFF // ML_W
    ob = MO_OFF // ML_W
    rmap = lambda b, s: (b * ns + s, 0)
    sel = np.zeros((2, LANES, ML_W), np.float32)
    for h in range(ML_HEADS):
        sel[:, GATE_MF + h, h * ML_HD:(h + 1) * ML_HD] = 1.0
    sel = jnp.asarray(sel.reshape(2 * LANES, ML_W), BF16)
    return pl.pallas_call(
        functools.partial(_mlstm_kernel, tt=tt),
        grid=(batch, ns),
        in_specs=[
            pl.BlockSpec((tt, 2 * ML_W), lambda b, s: (b * ns + s, ub)),
            pl.BlockSpec((tt, ML_W), lambda b, s: (b * ns + s, vb)),
            pl.BlockSpec((tt, ML_W), lambda b, s: (b * ns + s, ob)),
            pl.BlockSpec((tt, 3 * LANES), rmap),
            pl.BlockSpec((1, GATE_USED, tt), lambda b, s: (b, 0, s)),
            pl.BlockSpec((2 * LANES, ML_W), lambda b, s: (0, 0)),
            pl.BlockSpec((CONV_K, 2 * ML_W), lambda b, s: (0, 0)),
            pl.BlockSpec((1, 2 * ML_W), lambda b, s: (0, 0)),
            pl.BlockSpec((1, ML_W), lambda b, s: (0, 0)),
        ],
        out_specs=pl.BlockSpec((tt, ML_W), rmap),
        out_shape=jax.ShapeDtypeStruct((batch * seq, ML_W), BF16),
        scratch_shapes=[pltpu.VMEM((tt + SUBLANES, 2 * ML_W), F32),
                        pltpu.VMEM((tt, ML_W), BF16),
                        pltpu.VMEM((tt, ML_W), BF16),
                        pltpu.VMEM((ML_HEADS, ML_HD, 2 * ML_HD), F32),
                        pltpu.VMEM((SUBLANES, LANES), F32)],
        compiler_params=_params("parallel", "arbitrary"),
        name="mlstm")(proj, proj, proj, rows, ct, sel, conv_w, conv_b, norm_w)


def _mix_xattn_kernel(yf_ref, ym_ref, gt_ref, bg_ref, h_ref, wf_ref, wm_ref, wmix_ref,
                      g_ref, wq_ref, k_ref, v_ref, wo_ref, o_ref):
    g = jax.nn.sigmoid(gt_ref[...].astype(F32) + bg_ref[...])
    uf = jnp.dot(yf_ref[...], wf_ref[...], preferred_element_type=F32)
    um = jnp.dot(ym_ref[...], wm_ref[...], preferred_element_type=F32)
    merged = g[:, :D_MODEL] * uf + g[:, D_MODEL:] * um
    x = h_ref[...] + jnp.dot(merged.astype(BF16), wmix_ref[...], preferred_element_type=F32)

    xg, r = _rms_split(x, g_ref[...])
    q = jnp.dot(xg, wq_ref[...], preferred_element_type=F32)
    q = (q * (r * (X_HD ** -0.5 * LOG2E))).astype(BF16)
    outs = []
    for hd in range(X_HEADS):
        hs = slice(hd * X_HD, (hd + 1) * X_HD)
        s = lax.dot_general(q[:, hs], k_ref[:, hs], NT_DIMS, preferred_element_type=F32)
        e = jnp.exp2(s - jnp.max(s, axis=-1, keepdims=True))
        outs.append(jnp.dot(e.astype(BF16), v_ref[:, hs], preferred_element_type=F32)
                    / jnp.sum(e, axis=-1, keepdims=True))
    o = jnp.concatenate(outs, axis=1).astype(BF16)
    o_ref[...] = x + jnp.dot(o, wo_ref[...], preferred_element_type=F32)


def _mix_xattn(yf, ym, proj, b_gate, h, wf, wm, wmix, gain, wq, kv, wo, *,
               layer, batch, seq, n_mem, tm):
    ns = seq // tm
    const = lambda b, s: (0, 0)
    row = lambda b, s: (b * ns + s, 0)
    return pl.pallas_call(
        _mix_xattn_kernel, grid=(batch, ns),
        in_specs=[
            pl.BlockSpec((tm, FOX_W), row),
            pl.BlockSpec((tm, ML_W), row),
            pl.BlockSpec((tm, 2 * D_MODEL), lambda b, s: (b * ns + s, G_OFF // (2 * D_MODEL))),
            pl.BlockSpec((1, 2 * D_MODEL), const),
            pl.BlockSpec((tm, D_MODEL), row),
            _layer_block((FOX_W, D_MODEL), layer),
            _layer_block((ML_W, D_MODEL), layer),
            _layer_block((D_MODEL, D_MODEL), layer),
            pl.BlockSpec((1, D_MODEL), const),
            _layer_block((D_MODEL, D_MODEL), layer),
            pl.BlockSpec((n_mem, D_MODEL), lambda b, s: (b, 0)),
            pl.BlockSpec((n_mem, D_MODEL), lambda b, s: (b, 1)),
            _layer_block((D_MODEL, D_MODEL), layer),
        ],
        out_specs=pl.BlockSpec((tm, D_MODEL), row),
        out_shape=jax.ShapeDtypeStruct((batch * seq, D_MODEL), F32),
        compiler_params=_params("parallel", "parallel"),
        name="mix_cross_attention")(yf, ym, proj, b_gate, h, wf, wm, wmix,
                                    gain, wq, kv, kv, wo)


def _mlp_kernel(h_ref, g_ref, w1_ref, w2_ref, gf_ref, o_ref, *, tf, final):
    x = h_ref[...]
    xg, r = _rms_split(x, g_ref[...])
    acc = x
    for j in range(D_FF // tf):
        fs = slice(j * tf, (j + 1) * tf)
        a = jnp.maximum(r * jnp.dot(xg, w1_ref[:, fs], preferred_element_type=F32), 0.0)
        acc = acc + jnp.dot((a * a).astype(BF16), w2_ref[fs, :], preferred_element_type=F32)
    o_ref[...] = _rms(acc, gf_ref[...]) if final else acc


def _mlp(h, gain, w1, w2, gain_final, *, layer, tm, tf, final):
    m = h.shape[0]
    const = lambda i: (0, 0)
    row = lambda i: (i, 0)
    return pl.pallas_call(
        functools.partial(_mlp_kernel, tf=tf, final=final),
        grid=(m // tm,),
        in_specs=[
            pl.BlockSpec((tm, D_MODEL), row),
            pl.BlockSpec((1, D_MODEL), const),
            _layer_block((D_MODEL, D_FF), layer),
            _layer_block((D_FF, D_MODEL), layer),
            pl.BlockSpec((1, D_MODEL), const),
        ],
        out_specs=pl.BlockSpec((tm, D_MODEL), row),
        out_shape=jax.ShapeDtypeStruct((m, D_MODEL), F32),
        compiler_params=_params("parallel"),
        name="mlp")(h, gain, w1, w2, gain_final)


def _pick(total, prefs):
    for t in prefs:
        if total % t == 0:
            return t
    raise ValueError(f"no tile in {prefs} divides {total}")


def kernel(x, mem, norm_mix, norm_xattn, norm_mem, norm_mlp, w_in, b_fox_f, b_mlstm_i,
           b_mlstm_f, b_gate, conv_w, conv_b, mlstm_norm, w_up_fox, w_up_mlstm, w_mix_out,
           w_xq, w_xkv, w_xo, w_mlp_in, w_mlp_out, norm_final):
    batch, seq, d = x.shape
    n_mem = mem.shape[1]
    depth = w_in.shape[0]
    assert d == D_MODEL and seq % ML_CHUNK == 0
    m = batch * seq

    tm_proj = _pick(m, (512, 256))
    tn_proj = 512
    t_seq = _pick(seq, (1024, 512, 256))
    t_gates = _pick(seq, (2048, 1024, 512, 256))
    tk = _pick(seq, (512, 256))
    tq = _pick(seq, (4 * tk, 2 * tk, tk))
    tm_x = _pick(seq, (512, 256))
    tm_mlp = _pick(m, (512, 256))
    tf = 1024

    sp = [0, FOX_W, 2 * FOX_W, 3 * FOX_W, 3 * FOX_W + FOX_HEADS]
    sp += [sp[-1] + 2 * ML_W, sp[-1] + 3 * ML_W, sp[-1] + 4 * ML_W]
    sp += [sp[-1] + ML_HEADS, sp[-1] + 2 * ML_HEADS, sp[-1] + 2 * ML_HEADS + 2 * D_MODEL]
    w_in_t = jnp.swapaxes(w_in, 1, 2)
    w_in_b = w_in_t.astype(BF16)
    seg = [w_in_b[:, a:b, :] for a, b in zip(sp[:-1], sp[1:])]
    fq, fk, fv, ff, mqk, mv, mo, mi, mf, gp = seg
    fq = (w_in_t[:, sp[0]:sp[1], :] * (FOX_HD ** -0.5 * LOG2E)).astype(BF16)
    w_main = jnp.concatenate([gp, mqk, fq, fk, fv, mv, mo], axis=1)
    w_gate = jnp.concatenate(
        [ff, mi, mf, jnp.zeros((depth, LANES - GATE_USED, d), BF16)], axis=1)
    gate_bias = jnp.concatenate(
        [b_fox_f, b_mlstm_i, b_mlstm_f, jnp.zeros((depth, LANES - GATE_USED), F32)], axis=1)
    bf = lambda w: w.astype(BF16)
    w_up_fox, w_up_mlstm, w_mix_out = bf(w_up_fox), bf(w_up_mlstm), bf(w_mix_out)
    w_xq, w_xkv, w_xo = bf(w_xq), bf(w_xkv), bf(w_xo)
    w_mlp_in, w_mlp_out = bf(w_mlp_in), bf(w_mlp_out)

    h = x.reshape(m, d)
    mem2 = mem.reshape(batch * n_mem, d)
    for l in range(depth):
        proj, gpre = _in_proj(h, norm_mix[l][None], w_main, w_gate,
                              layer=l, tm=tm_proj, tn=tn_proj)
        rows, ct, cb = _gates(gpre, gate_bias[l][None], batch=batch, seq=seq, tg=t_gates)
        y_fox = _fox(proj, cb, batch=batch, seq=seq, tq=tq, tk=tk, npp=1)
        y_ml = _mlstm(proj, rows, ct, conv_w[l], conv_b[l][None], mlstm_norm[l][None],
                      batch=batch, seq=seq, tt=t_seq)
        kv = _norm_proj(mem2, norm_mem[l][None], w_xkv,
                        layer=l, tm=_pick(batch * n_mem, (1024, 512, 256)), tn=1024)
        h = _mix_xattn(y_fox, y_ml, proj, b_gate[l][None], h,
                       w_up_fox, w_up_mlstm, w_mix_out,
                       norm_xattn[l][None], w_xq, kv, w_xo,
                       layer=l, batch=batch, seq=seq, n_mem=n_mem, tm=tm_x)
        h = _mlp(h, norm_mlp[l][None], w_mlp_in, w_mlp_out, norm_final[None],
                 layer=l, tm=tm_mlp, tf=tf, final=(l == depth - 1))
    return h.reshape(batch, seq, d)
```

```python
import functools

import jax
import jax.numpy as jnp
import numpy as np
from jax import lax
from jax.experimental import pallas as pl
from jax.experimental.pallas import tpu as pltpu

F32 = jnp.float32
BF16 = jnp.bfloat16

D_MODEL = 1024
FOX_HEADS = 8
FOX_HD = 64
FOX_W = FOX_HEADS * FOX_HD
ML_HEADS = 4
ML_HD = 128
ML_W = ML_HEADS * ML_HD
CONV_K = 4
X_HEADS = 4
X_HD = D_MODEL // X_HEADS
D_FF = 4 * D_MODEL
EPS = 1e-6
LOG2E = 1.4426950408889634

LANES = 128
SUBLANES = 8
V7X_VMEM_LIMIT_BYTES = 56 * 1024 * 1024

G_OFF = 0
MQK_OFF = G_OFF + 2 * D_MODEL
FQ_OFF = MQK_OFF + 2 * ML_W
FK_OFF = FQ_OFF + FOX_W
FV_OFF = FK_OFF + FOX_W
MV_OFF = FV_OFF + FOX_W
MO_OFF = MV_OFF + ML_W
PROJ_W = MO_OFF + ML_W
GATE_FF = 0
GATE_MI = GATE_FF + FOX_HEADS
GATE_MF = GATE_MI + ML_HEADS
GATE_USED = GATE_MF + ML_HEADS

ML_CHUNK = 256
NT_DIMS = (((1,), (1,)), ((), ()))
TN_DIMS = (((0,), (0,)), ((), ()))


def _params(*sem):
    return pltpu.CompilerParams(dimension_semantics=sem,
                                vmem_limit_bytes=V7X_VMEM_LIMIT_BYTES)


def _rms(x, g):
    return x * lax.rsqrt(jnp.mean(x * x, axis=-1, keepdims=True) + EPS) * g


def _rms_split(x, g):
    r = lax.rsqrt(jnp.mean(x * x, axis=-1, keepdims=True) + EPS)
    return (x * g).astype(BF16), r


def _layer_block(shape, layer):
    zeros = (0,) * len(shape)
    return pl.BlockSpec((None,) + tuple(shape), lambda *_: (layer,) + zeros)


def _in_proj_kernel(x_ref, g_ref, w_ref, wg_ref, o_ref, og_ref, *, tn):
    xg, r = _rms_split(x_ref[...], g_ref[...])
    og_ref[...] = r * lax.dot_general(xg, wg_ref[...], NT_DIMS, preferred_element_type=F32)
    for j in range(w_ref.shape[0] // tn):
        cs = slice(j * tn, (j + 1) * tn)
        o_ref[:, cs] = (r * lax.dot_general(xg, w_ref[cs, :], NT_DIMS,
                                            preferred_element_type=F32)).astype(BF16)


def _in_proj(x, gain, w, wg, *, layer, tm, tn):
    m, d = x.shape
    n = w.shape[-2]
    const = lambda i: (0, 0)
    return pl.pallas_call(
        functools.partial(_in_proj_kernel, tn=tn), grid=(m // tm,),
        in_specs=[pl.BlockSpec((tm, d), lambda i: (i, 0)),
                  pl.BlockSpec((1, d), const),
                  _layer_block((n, d), layer),
                  _layer_block((LANES, d), layer)],
        out_specs=[pl.BlockSpec((tm, n), lambda i: (i, 0)),
                   pl.BlockSpec((tm, LANES), lambda i: (i, 0))],
        out_shape=[jax.ShapeDtypeStruct((m, n), BF16),
                   jax.ShapeDtypeStruct((m, LANES), F32)],
        compiler_params=_params("parallel"),
        name="norm_in_proj")(x, gain, w, wg)


def _norm_proj_kernel(x_ref, g_ref, w_ref, o_ref, xn_ref):
    @pl.when(pl.program_id(1) == 0)
    def _():
        xn_ref[...] = _rms(x_ref[...], g_ref[...]).astype(BF16)

    o_ref[...] = jnp.dot(xn_ref[...], w_ref[...],
                         preferred_element_type=F32).astype(BF16)


def _norm_proj(x, gain, w, *, layer, tm, tn):
    m, d = x.shape
    n = w.shape[-1]
    return pl.pallas_call(
        _norm_proj_kernel, grid=(m // tm, n // tn),
        in_specs=[pl.BlockSpec((tm, d), lambda i, j: (i, 0)),
                  pl.BlockSpec((1, d), lambda i, j: (0, 0)),
                  pl.BlockSpec((None, d, tn), lambda i, j: (layer, 0, j))],
        out_specs=pl.BlockSpec((tm, tn), lambda i, j: (i, j)),
        out_shape=jax.ShapeDtypeStruct((m, n), BF16),
        scratch_shapes=[pltpu.VMEM((tm, d), BF16)],
        compiler_params=_params("parallel", "arbitrary"),
        name="norm_proj")(x, gain, w)


def _split3(v):
    hi = v.astype(BF16)
    r1 = v - hi.astype(F32)
    mid = r1.astype(BF16)
    lo = (r1 - mid.astype(F32)).astype(BF16)
    return hi, mid, lo


def _gates_kernel(x_ref, b_ref, pm_ref, r_ref, ct_ref, cb_ref, carry_ref, *, tg):
    @pl.when(pl.program_id(1) == 0)
    def _():
        carry_ref[...] = jnp.zeros_like(carry_ref)

    c = ML_CHUNK
    row = lax.broadcasted_iota(jnp.int32, (c, c), 0)
    col = lax.broadcasted_iota(jnp.int32, (c, c), 1)
    causal = row >= col
    tri = jnp.where(causal, 1.0, 0.0).astype(BF16)
    lane = lax.broadcasted_iota(jnp.int32, (c, LANES), 1)
    is_fox = lane < GATE_MI
    is_ig = (lane >= GATE_MI) & (lane < GATE_MF)
    is_used = lane < GATE_USED
    bias = b_ref[...]
    for blk in range(tg // c):
        rs = slice(blk * c, (blk + 1) * c)
        x = x_ref[rs, :] + bias
        logsig = jnp.minimum(x, 0.0) - jnp.log1p(jnp.exp(-jnp.abs(x)))
        val = jnp.where(is_ig, x, logsig)
        val = jnp.where(is_used, val, 0.0)
        csp = jnp.dot(tri, jnp.concatenate(_split3(val), axis=1), preferred_element_type=F32)
        cs = csp[:, 0:LANES] + csp[:, LANES:2 * LANES] + csp[:, 2 * LANES:3 * LANES]
        glob = cs + carry_ref[0:1, :]
        carry_ref[0:1, :] = glob[c - 1:c, :]
        f_t = cs * LOG2E
        b_t = (pltpu.roll(val, GATE_MF - GATE_MI, axis=1) - cs) * LOG2E
        bt = b_t.T
        ct_ref[0, :, rs] = bt[:GATE_USED, :]
        cm = jnp.zeros((c, LANES), F32)
        for h in range(ML_HEADS):
            k = GATE_MF + h
            run_max = jnp.max(jnp.where(causal, bt[k:k + 1, :], -jnp.inf),
                              axis=-1, keepdims=True)
            cm = jnp.where(lane == k, run_max, cm)
        r_ref[rs, 0:LANES] = f_t
        r_ref[rs, LANES:2 * LANES] = b_t
        r_ref[rs, 2 * LANES:3 * LANES] = cm
        terms = _split3(jnp.where(is_fox, glob * (-LOG2E), 0.0))
        cb_ref[rs, :] = jnp.dot(jnp.concatenate(terms, axis=1), pm_ref[...],
                                preferred_element_type=F32).astype(BF16)


def _fox_bias_placement():
    pm = np.zeros((3, LANES, FOX_W), np.float32)
    for h in range(FOX_HEADS):
        base = (h // 2) * LANES + (FOX_HD if h % 2 == 0 else 0)
        for t in range(3):
            pm[t, h, base + t] = 1.0
    return jnp.asarray(pm.reshape(3 * LANES, FOX_W), BF16)


def _gates(gpre, bias, *, batch, seq, tg):
    ns = seq // tg
    rmap = lambda b, s: (b * ns + s, 0)
    return pl.pallas_call(
        functools.partial(_gates_kernel, tg=tg),
        grid=(batch, ns),
        in_specs=[pl.BlockSpec((tg, LANES), rmap),
                  pl.BlockSpec((1, LANES), lambda b, s: (0, 0)),
                  pl.BlockSpec((3 * LANES, FOX_W), lambda b, s: (0, 0))],
        out_specs=[pl.BlockSpec((tg, 3 * LANES), rmap),
                   pl.BlockSpec((1, GATE_USED, tg), lambda b, s: (b, 0, s)),
                   pl.BlockSpec((tg, FOX_W), rmap)],
        out_shape=[jax.ShapeDtypeStruct((batch * seq, 3 * LANES), F32),
                   jax.ShapeDtypeStruct((batch, GATE_USED, seq), F32),
                   jax.ShapeDtypeStruct((batch * seq, FOX_W), BF16)],
        scratch_shapes=[pltpu.VMEM((SUBLANES, LANES), F32)],
        compiler_params=_params("parallel", "arbitrary"),
        name="gates")(gpre, bias, _fox_bias_placement())


def _fox_lane_rows():
    r = np.zeros((SUBLANES, LANES), np.float32)
    r[0, :FOX_HD] = 1.0
    r[1, FOX_HD:] = 1.0
    r[2, FOX_HD:FOX_HD + 3] = 1.0
    r[3, 0:3] = 1.0
    return jnp.asarray(r, BF16)


def _fox_kernel(q_ref, k_ref, v_ref, cb_ref, mk_ref, o_ref, m_ref, acc_ref, *, tq, tk):
    i = pl.program_id(2)
    n_sub = tq // tk
    npp = q_ref.shape[1] // LANES
    lo, hi = mk_ref[0:1, :], mk_ref[1:2, :]
    qh = []
    for pp in range(npp):
        q2 = q_ref[:, pp * LANES:(pp + 1) * LANES]
        qh += [q2 * lo + mk_ref[2:3, :], q2 * hi + mk_ref[3:4, :]]

    tri = (lax.broadcasted_iota(jnp.int32, (tk, tk), 1)
           <= lax.broadcasted_iota(jnp.int32, (tk, tk), 0))

    def step(j, row0, masked, tk=tk, first=False):
        rows = tq - row0
        r0 = pl.multiple_of(j * tk, tk)
        for pp in range(npp):
            ps = slice(pp * LANES, (pp + 1) * LANES)
            ks = k_ref[pl.ds(r0, tk), ps]
            vs = v_ref[pl.ds(r0, tk), ps]
            cb = cb_ref[pl.ds(r0, tk), ps]
            kh = (ks * lo + cb * hi, ks * hi + cb * lo)
            vh = (vs * lo + hi, vs * hi + lo)
            for h in range(2):
                hh = 2 * pp + h
                s = lax.dot_general(qh[hh][row0:, :], kh[h], NT_DIMS,
                                    preferred_element_type=F32)
                if masked:
                    top = jnp.where(tri, s[:tk], -jnp.inf)
                    s = top if rows == tk else jnp.concatenate([top, s[tk:]], axis=0)
                s_max = jnp.max(s, axis=-1, keepdims=True)
                if first:
                    m_new = jnp.broadcast_to(s_max, (rows, LANES))
                else:
                    m_old = m_ref[hh, row0:, :]
                    m_new = jnp.maximum(m_old, s_max)
                pe = jnp.exp2(s - jnp.concatenate([m_new] * (tk // LANES), axis=1))
                pv = jnp.dot(pe.astype(BF16), vh[h], preferred_element_type=F32)
                if first:
                    acc_ref[hh, row0:, :] = pv
                else:
                    acc_ref[hh, row0:, :] = jnp.exp2(m_old - m_new) * acc_ref[hh, row0:, :] + pv
                m_ref[hh, row0:, :] = m_new

    tkf = 2 * tk if tq % (2 * tk) == 0 else tk

    def body(jj, carry):
        for u in range(tq // tkf):
            step(jj * (tq // tkf) + u, 0, False, tkf)
        return carry

    for d in range(n_sub):
        step(i * n_sub + d, d * tk, True, first=(d == 0))
    lax.fori_loop(0, i, body, 0)

    first = lax.broadcasted_iota(jnp.int32, (tq, LANES), 1) < FOX_HD
    for pp in range(npp):
        acc_a, acc_b = acc_ref[2 * pp], acc_ref[2 * pp + 1]
        den = pltpu.roll(jnp.where(first, acc_b, acc_a), FOX_HD, axis=1)
        o_ref[:, pp * LANES:(pp + 1) * LANES] = (
            jnp.where(first, acc_a, acc_b) / den).astype(BF16)


def _fox(proj, cb, *, batch, seq, tq, tk, npp):
    nq = seq // tq
    ngrp = FOX_HEADS // 2 // npp
    w = npp * LANES
    qb, kb, vb = FQ_OFF // w, FK_OFF // w, FV_OFF // w
    return pl.pallas_call(
        functools.partial(_fox_kernel, tq=tq, tk=tk),
        grid=(batch, ngrp, nq),
        in_specs=[
            pl.BlockSpec((tq, w), lambda b, p, i: (b * nq + i, qb + p)),
            pl.BlockSpec((seq, w), lambda b, p, i: (b, kb + p)),
            pl.BlockSpec((seq, w), lambda b, p, i: (b, vb + p)),
            pl.BlockSpec((seq, w), lambda b, p, i: (b, p)),
            pl.BlockSpec((SUBLANES, LANES), lambda b, p, i: (0, 0)),
        ],
        out_specs=pl.BlockSpec((tq, w), lambda b, p, i: (b * nq + i, p)),
        out_shape=jax.ShapeDtypeStruct((batch * seq, FOX_W), BF16),
        scratch_shapes=[pltpu.VMEM((2 * npp, tq, LANES), F32),
                        pltpu.VMEM((2 * npp, tq, LANES), F32)],
        compiler_params=_params("parallel", "parallel", "arbitrary"),
        name="fox_attention")(proj, proj, proj, cb, _fox_lane_rows())


def _mlstm_kernel(u_ref, v_ref, mo_ref, r_ref, ct_ref, sel_ref, cw_ref, cb_ref,
                  nw_ref, o_ref, ubuf, q_s, k_s, st_ref, m_ref, *, tt):
    s_idx = pl.program_id(1)
    halo = SUBLANES

    @pl.when(s_idx == 0)
    def _():
        ubuf[0:halo, :] = jnp.zeros((halo, 2 * ML_W), F32)
        st_ref[...] = jnp.zeros_like(st_ref)
        m_ref[...] = jnp.zeros_like(m_ref)

    @pl.when(s_idx > 0)
    def _():
        ubuf[0:halo, :] = ubuf[tt:tt + halo, :]

    c = ML_CHUNK
    for blk in range(tt // c):
        ubuf[halo + blk * c:halo + (blk + 1) * c, :] = (
            u_ref[blk * c:(blk + 1) * c, :].astype(F32))

    def conv_block(blk):
        acc = cb_ref[...]
        for j in (CONV_K - 1,) + tuple(range(CONV_K - 1)):
            off = halo - (CONV_K - 1) + j + blk * c
            acc = acc + cw_ref[j:j + 1, :] * ubuf[off:off + c, :]
        qk = acc * jax.nn.sigmoid(acc)
        q_s[blk * c:(blk + 1) * c, :] = qk[:, :ML_W].astype(BF16)
        k_s[blk * c:(blk + 1) * c, :] = (qk[:, ML_W:] * (ML_HD ** -0.5)).astype(BF16)

    row = lax.broadcasted_iota(jnp.int32, (c, c), 0)
    col = lax.broadcasted_iota(jnp.int32, (c, c), 1)
    causal = col <= row
    ones_blk = jnp.ones((c, ML_HD), BF16)

    def chunk(ci):
        r0 = ci * c
        cb = ct_ref[0, :, pl.ds(r0, c)]
        rep = []
        for qi in range(3):
            hi, mid, lo = _split3(r_ref[pl.ds(r0, c), qi * LANES:(qi + 1) * LANES])
            rep.append(jnp.dot(jnp.concatenate([hi, mid], axis=1), sel_ref[...],
                               preferred_element_type=F32)
                       + jnp.dot(lo, sel_ref[0:LANES, :], preferred_element_type=F32))
        for h in range(ML_HEADS):
            hs = slice(h * ML_HD, (h + 1) * ML_HD)
            qh = q_s[pl.ds(r0, c), hs]
            kh = k_s[pl.ds(r0, c), hs]
            vh = v_ref[pl.ds(r0, c), hs]
            f_t, b_t, cm_t = rep[0][:, hs], rep[1][:, hs], rep[2][:, hs]
            b_row = cb[GATE_MF + h:GATE_MF + h + 1, :]
            m_prev = m_ref[h:h + 1, :]

            g_t = jnp.maximum(m_prev, cm_t)
            dw = jnp.exp2(jnp.where(causal, b_row - jnp.concatenate([g_t, g_t], axis=1),
                                    -jnp.inf))
            inter_w = jnp.exp2(m_prev - g_t)
            sqk = lax.dot_general(qh, kh, NT_DIMS, preferred_element_type=F32) * dw
            state = st_ref[h]
            qc = jnp.dot(qh, state.astype(BF16), preferred_element_type=F32)
            v_aug = jnp.concatenate([vh, ones_blk], axis=1)
            pv = jnp.dot(sqk.astype(BF16), v_aug, preferred_element_type=F32)
            num = inter_w * qc[:, :ML_HD] + pv[:, :ML_HD]
            den = inter_w * qc[:, ML_HD:] + pv[:, ML_HD:]
            hh = num / jnp.maximum(jnp.abs(den), jnp.exp2(-(f_t + g_t)))
            ms = jnp.mean(hh * hh, axis=-1, keepdims=True)
            y = (hh * lax.rsqrt(ms + EPS) * nw_ref[:, hs]
                 * jax.nn.sigmoid(mo_ref[pl.ds(r0, c), hs].astype(F32)))
            o_ref[pl.ds(r0, c), hs] = y.astype(BF16)

            f_last = f_t[c - 1:c, :]
            m_new = jnp.maximum(f_last + m_prev, f_last + cm_t[c - 1:c, :])
            s_old = jnp.exp2(f_last + m_prev - m_new)
            kw = (kh.astype(F32) * jnp.exp2(f_last + b_t - m_new)).astype(BF16)
            upd = lax.dot_general(kw, v_aug, TN_DIMS, preferred_element_type=F32)
            st_ref[h] = jnp.concatenate([s_old, s_old], axis=1) * state + upd
            m_ref[h:h + 1, :] = m_new

    for ci in range(tt // c):
        conv_block(ci)
        chunk(ci)


def _mlstm(proj, rows, ct, conv_w, conv_b, norm_w, *, batch, seq, tt):
    ns = seq // tt
    ub = MQK_OFF // (2 * ML_W)
    vb = MV_OFF // ML_W
    ob = MO_OFF // ML_W
    rmap = lambda b, s: (b * ns + s, 0)
    sel = np.zeros((2, LANES, ML_W), np.float32)
    for h in range(ML_HEADS):
        sel[:, GATE_MF + h, h * ML_HD:(h + 1) * ML_HD] = 1.0
    sel = jnp.asarray(sel.reshape(2 * LANES, ML_W), BF16)
    return pl.pallas_call(
        functools.partial(_mlstm_kernel, tt=tt),
        grid=(batch, ns),
        in_specs=[
            pl.BlockSpec((tt, 2 * ML_W), lambda b, s: (b * ns + s, ub)),
            pl.BlockSpec((tt, ML_W), lambda b, s: (b * ns + s, vb)),
            pl.BlockSpec((tt, ML_W), lambda b, s: (b * ns + s, ob)),
            pl.BlockSpec((tt, 3 * LANES), rmap),
            pl.BlockSpec((1, GATE_USED, tt), lambda b, s: (b, 0, s)),
            pl.BlockSpec((2 * LANES, ML_W), lambda b, s: (0, 0)),
            pl.BlockSpec((CONV_K, 2 * ML_W), lambda b, s: (0, 0)),
            pl.BlockSpec((1, 2 * ML_W), lambda b, s: (0, 0)),
            pl.BlockSpec((1, ML_W), lambda b, s: (0, 0)),
        ],
        out_specs=pl.BlockSpec((tt, ML_W), rmap),
        out_shape=jax.ShapeDtypeStruct((batch * seq, ML_W), BF16),
        scratch_shapes=[pltpu.VMEM((tt + SUBLANES, 2 * ML_W), F32),
                        pltpu.VMEM((tt, ML_W), BF16),
                        pltpu.VMEM((tt, ML_W), BF16),
                        pltpu.VMEM((ML_HEADS, ML_HD, 2 * ML_HD), F32),
                        pltpu.VMEM((SUBLANES, LANES), F32)],
        compiler_params=_params("parallel", "arbitrary"),
        name="mlstm")(proj, proj, proj, rows, ct, sel, conv_w, conv_b, norm_w)


def _mix_xattn_kernel(yf_ref, ym_ref, gt_ref, bg_ref, h_ref, wf_ref, wm_ref, wmix_ref,
                      g_ref, wq_ref, k_ref, v_ref, wo_ref, o_ref):
    g = jax.nn.sigmoid(gt_ref[...].astype(F32) + bg_ref[...])
    uf = jnp.dot(yf_ref[...], wf_ref[...], preferred_element_type=F32)
    um = jnp.dot(ym_ref[...], wm_ref[...], preferred_element_type=F32)
    merged = g[:, :D_MODEL] * uf + g[:, D_MODEL:] * um
    x = h_ref[...] + jnp.dot(merged.astype(BF16), wmix_ref[...], preferred_element_type=F32)

    xg, r = _rms_split(x, g_ref[...])
    q = jnp.dot(xg, wq_ref[...], preferred_element_type=F32)
    q = (q * (r * (X_HD ** -0.5 * LOG2E))).astype(BF16)
    outs = []
    for hd in range(X_HEADS):
        hs = slice(hd * X_HD, (hd + 1) * X_HD)
        s = lax.dot_general(q[:, hs], k_ref[:, hs], NT_DIMS, preferred_element_type=F32)
        e = jnp.exp2(s - jnp.max(s, axis=-1, keepdims=True))
        outs.append(jnp.dot(e.astype(BF16), v_ref[:, hs], preferred_element_type=F32)
                    / jnp.sum(e, axis=-1, keepdims=True))
    o = jnp.concatenate(outs, axis=1).astype(BF16)
    o_ref[...] = x + jnp.dot(o, wo_ref[...], preferred_element_type=F32)


def _mix_xattn(yf, ym, proj, b_gate, h, wf, wm, wmix, gain, wq, kv, wo, *,
               layer, batch, seq, n_mem, tm):
    ns = seq // tm
    const = lambda b, s: (0, 0)
    row = lambda b, s: (b * ns + s, 0)
    return pl.pallas_call(
        _mix_xattn_kernel, grid=(batch, ns),
        in_specs=[
            pl.BlockSpec((tm, FOX_W), row),
            pl.BlockSpec((tm, ML_W), row),
            pl.BlockSpec((tm, 2 * D_MODEL), lambda b, s: (b * ns + s, G_OFF // (2 * D_MODEL))),
            pl.BlockSpec((1, 2 * D_MODEL), const),
            pl.BlockSpec((tm, D_MODEL), row),
            _layer_block((FOX_W, D_MODEL), layer),
            _layer_block((ML_W, D_MODEL), layer),
            _layer_block((D_MODEL, D_MODEL), layer),
            pl.BlockSpec((1, D_MODEL), const),
            _layer_block((D_MODEL, D_MODEL), layer),
            pl.BlockSpec((n_mem, D_MODEL), lambda b, s: (b, 0)),
            pl.BlockSpec((n_mem, D_MODEL), lambda b, s: (b, 1)),
            _layer_block((D_MODEL, D_MODEL), layer),
        ],
        out_specs=pl.BlockSpec((tm, D_MODEL), row),
        out_shape=jax.ShapeDtypeStruct((batch * seq, D_MODEL), F32),
        compiler_params=_params("parallel", "parallel"),
        name="mix_cross_attention")(yf, ym, proj, b_gate, h, wf, wm, wmix,
                                    gain, wq, kv, kv, wo)


def _mlp_kernel(h_ref, g_ref, w1_ref, w2_ref, gf_ref, o_ref, *, tf, final):
    x = h_ref[...]
    xg, r = _rms_split(x, g_ref[...])
    acc = x
    for j in range(D_FF // tf):
        fs = slice(j * tf, (j + 1) * tf)
        a = jnp.maximum(r * jnp.dot(xg, w1_ref[:, fs], preferred_element_type=F32), 0.0)
        acc = acc + jnp.dot((a * a).astype(BF16), w2_ref[fs, :], preferred_element_type=F32)
    o_ref[...] = _rms(acc, gf_ref[...]) if final else acc


def _mlp(h, gain, w1, w2, gain_final, *, layer, tm, tf, final):
    m = h.shape[0]
    const = lambda i: (0, 0)
    row = lambda i: (i, 0)
    return pl.pallas_call(
        functools.partial(_mlp_kernel, tf=tf, final=final),
        grid=(m // tm,),
        in_specs=[
            pl.BlockSpec((tm, D_MODEL), row),
            pl.BlockSpec((1, D_MODEL), const),
            _layer_block((D_MODEL, D_FF), layer),
            _layer_block((D_FF, D_MODEL), layer),
            pl.BlockSpec((1, D_MODEL), const),
        ],
        out_specs=pl.BlockSpec((tm, D_MODEL), row),
        out_shape=jax.ShapeDtypeStruct((m, D_MODEL), F32),
        compiler_params=_params("parallel"),
        name="mlp")(h, gain, w1, w2, gain_final)


def _pick(total, prefs):
    for t in prefs:
        if total % t == 0:
            return t
    raise ValueError(f"no tile in {prefs} divides {total}")


def kernel(x, mem, norm_mix, norm_xattn, norm_mem, norm_mlp, w_in, b_fox_f, b_mlstm_i,
           b_mlstm_f, b_gate, conv_w, conv_b, mlstm_norm, w_up_fox, w_up_mlstm, w_mix_out,
           w_xq, w_xkv, w_xo, w_mlp_in, w_mlp_out, norm_final):
    batch, seq, d = x.shape
    n_mem = mem.shape[1]
    depth = w_in.shape[0]
    assert d == D_MODEL and seq % ML_CHUNK == 0
    m = batch * seq

    tm_proj = _pick(m, (512, 256))
    tn_proj = 512
    t_seq = _pick(seq, (2048, 1024, 512, 256))
    tk = _pick(seq, (512, 256))
    tq = _pick(seq, (4 * tk, 2 * tk, tk))
    tm_x = _pick(seq, (512, 256))
    tm_mlp = _pick(m, (512, 256))
    tf = 1024

    sp = [0, FOX_W, 2 * FOX_W, 3 * FOX_W, 3 * FOX_W + FOX_HEADS]
    sp += [sp[-1] + 2 * ML_W, sp[-1] + 3 * ML_W, sp[-1] + 4 * ML_W]
    sp += [sp[-1] + ML_HEADS, sp[-1] + 2 * ML_HEADS, sp[-1] + 2 * ML_HEADS + 2 * D_MODEL]
    w_in_t = jnp.swapaxes(w_in, 1, 2)
    w_in_b = w_in_t.astype(BF16)
    seg = [w_in_b[:, a:b, :] for a, b in zip(sp[:-1], sp[1:])]
    fq, fk, fv, ff, mqk, mv, mo, mi, mf, gp = seg
    fq = (w_in_t[:, sp[0]:sp[1], :] * (FOX_HD ** -0.5 * LOG2E)).astype(BF16)
    w_main = jnp.concatenate([gp, mqk, fq, fk, fv, mv, mo], axis=1)
    w_gate = jnp.concatenate(
        [ff, mi, mf, jnp.zeros((depth, LANES - GATE_USED, d), BF16)], axis=1)
    gate_bias = jnp.concatenate(
        [b_fox_f, b_mlstm_i, b_mlstm_f, jnp.zeros((depth, LANES - GATE_USED), F32)], axis=1)
    bf = lambda w: w.astype(BF16)
    w_up_fox, w_up_mlstm, w_mix_out = bf(w_up_fox), bf(w_up_mlstm), bf(w_mix_out)
    w_xq, w_xkv, w_xo = bf(w_xq), bf(w_xkv), bf(w_xo)
    w_mlp_in, w_mlp_out = bf(w_mlp_in), bf(w_mlp_out)

    h = x.reshape(m, d)
    mem2 = mem.reshape(batch * n_mem, d)
    for l in range(depth):
        proj, gpre = _in_proj(h, norm_mix[l][None], w_main, w_gate,
                              layer=l, tm=tm_proj, tn=tn_proj)
        rows, ct, cb = _gates(gpre, gate_bias[l][None], batch=batch, seq=seq, tg=t_seq)
        y_fox = _fox(proj, cb, batch=batch, seq=seq, tq=tq, tk=tk, npp=1)
        y_ml = _mlstm(proj, rows, ct, conv_w[l], conv_b[l][None], mlstm_norm[l][None],
                      batch=batch, seq=seq, tt=t_seq)
        kv = _norm_proj(mem2, norm_mem[l][None], w_xkv,
                        layer=l, tm=_pick(batch * n_mem, (1024, 512, 256)), tn=1024)
        h = _mix_xattn(y_fox, y_ml, proj, b_gate[l][None], h,
                       w_up_fox, w_up_mlstm, w_mix_out,
                       norm_xattn[l][None], w_xq, kv, w_xo,
                       layer=l, batch=batch, seq=seq, n_mem=n_mem, tm=tm_x)
        h = _mlp(h, norm_mlp[l][None], w_mlp_in, w_mlp_out, norm_final[None],
                 layer=l, tm=tm_mlp, tf=tf, final=(l == depth - 1))
    return h.reshape(batch, seq, d)
```

```python
import functools

import jax
import jax.numpy as jnp
import numpy as np
from jax import lax
from jax.experimental import pallas as pl
from jax.experimental.pallas import tpu as pltpu

F32 = jnp.float32
BF16 = jnp.bfloat16

D_MODEL = 1024
FOX_HEADS = 8
FOX_HD = 64
FOX_W = FOX_HEADS * FOX_HD
ML_HEADS = 4
ML_HD = 128
ML_W = ML_HEADS * ML_HD
CONV_K = 4
X_HEADS = 4
X_HD = D_MODEL // X_HEADS
D_FF = 4 * D_MODEL
EPS = 1e-6
LOG2E = 1.4426950408889634

LANES = 128
SUBLANES = 8
V7X_VMEM_LIMIT_BYTES = 56 * 1024 * 1024

G_OFF = 0
MQK_OFF = G_OFF + 2 * D_MODEL
FQ_OFF = MQK_OFF + 2 * ML_W
FK_OFF = FQ_OFF + FOX_W
FV_OFF = FK_OFF + FOX_W
MV_OFF = FV_OFF + FOX_W
MO_OFF = MV_OFF + ML_W
PROJ_W = MO_OFF + ML_W
GATE_FF = 0
GATE_MI = GATE_FF + FOX_HEADS
GATE_MF = GATE_MI + ML_HEADS
GATE_USED = GATE_MF + ML_HEADS

ML_CHUNK = 256
NT_DIMS = (((1,), (1,)), ((), ()))
TN_DIMS = (((0,), (0,)), ((), ()))


def _params(*sem):
    return pltpu.CompilerParams(dimension_semantics=sem,
                                vmem_limit_bytes=V7X_VMEM_LIMIT_BYTES)


def _rms(x, g):
    return x * lax.rsqrt(jnp.mean(x * x, axis=-1, keepdims=True) + EPS) * g


def _rms_split(x, g):
    r = lax.rsqrt(jnp.mean(x * x, axis=-1, keepdims=True) + EPS)
    return (x * g).astype(BF16), r


def _layer_block(shape, layer):
    zeros = (0,) * len(shape)
    return pl.BlockSpec((None,) + tuple(shape), lambda *_: (layer,) + zeros)


def _in_proj_kernel(x_ref, g_ref, w_ref, wg_ref, o_ref, og_ref, *, tn):
    xg, r = _rms_split(x_ref[...], g_ref[...])
    og_ref[...] = r * lax.dot_general(xg, wg_ref[...], NT_DIMS, preferred_element_type=F32)
    for j in range(w_ref.shape[0] // tn):
        cs = slice(j * tn, (j + 1) * tn)
        o_ref[:, cs] = (r * lax.dot_general(xg, w_ref[cs, :], NT_DIMS,
                                            preferred_element_type=F32)).astype(BF16)


def _in_proj(x, gain, w, wg, *, layer, tm, tn):
    m, d = x.shape
    n = w.shape[-2]
    const = lambda i: (0, 0)
    return pl.pallas_call(
        functools.partial(_in_proj_kernel, tn=tn), grid=(m // tm,),
        in_specs=[pl.BlockSpec((tm, d), lambda i: (i, 0)),
                  pl.BlockSpec((1, d), const),
                  _layer_block((n, d), layer),
                  _layer_block((LANES, d), layer)],
        out_specs=[pl.BlockSpec((tm, n), lambda i: (i, 0)),
                   pl.BlockSpec((tm, LANES), lambda i: (i, 0))],
        out_shape=[jax.ShapeDtypeStruct((m, n), BF16),
                   jax.ShapeDtypeStruct((m, LANES), F32)],
        compiler_params=_params("parallel"),
        name="norm_in_proj")(x, gain, w, wg)


def _norm_proj_kernel(x_ref, g_ref, w_ref, o_ref, xn_ref):
    @pl.when(pl.program_id(2) == 0)
    def _():
        xn_ref[...] = _rms(x_ref[...], g_ref[...]).astype(BF16)

    o_ref[...] = jnp.dot(xn_ref[...], w_ref[...],
                         preferred_element_type=F32).astype(BF16)


def _norm_proj_layers(x, gains, w, *, tm, tn):
    m, d = x.shape
    depth, _, n = w.shape
    return pl.pallas_call(
        _norm_proj_kernel, grid=(depth, m // tm, n // tn),
        in_specs=[pl.BlockSpec((tm, d), lambda l, i, j: (i, 0)),
                  pl.BlockSpec((None, 1, d), lambda l, i, j: (l, 0, 0)),
                  pl.BlockSpec((None, d, tn), lambda l, i, j: (l, 0, j))],
        out_specs=pl.BlockSpec((None, tm, tn), lambda l, i, j: (l, i, j)),
        out_shape=jax.ShapeDtypeStruct((depth, m, n), BF16),
        scratch_shapes=[pltpu.VMEM((tm, d), BF16)],
        compiler_params=_params("parallel", "parallel", "arbitrary"),
        name="norm_proj")(x, gains, w)


def _split3(v):
    hi = v.astype(BF16)
    r1 = v - hi.astype(F32)
    mid = r1.astype(BF16)
    lo = (r1 - mid.astype(F32)).astype(BF16)
    return hi, mid, lo


def _gates_kernel(x_ref, b_ref, pm_ref, r_ref, ct_ref, cb_ref, carry_ref, *, tg):
    @pl.when(pl.program_id(1) == 0)
    def _():
        carry_ref[...] = jnp.zeros_like(carry_ref)

    c = ML_CHUNK
    row = lax.broadcasted_iota(jnp.int32, (c, c), 0)
    col = lax.broadcasted_iota(jnp.int32, (c, c), 1)
    causal = row >= col
    tri = jnp.where(causal, 1.0, 0.0).astype(BF16)
    lane = lax.broadcasted_iota(jnp.int32, (c, LANES), 1)
    is_fox = lane < GATE_MI
    is_ig = (lane >= GATE_MI) & (lane < GATE_MF)
    is_used = lane < GATE_USED
    bias = b_ref[...]
    for blk in range(tg // c):
        rs = slice(blk * c, (blk + 1) * c)
        x = x_ref[rs, :] + bias
        logsig = jnp.minimum(x, 0.0) - jnp.log1p(jnp.exp(-jnp.abs(x)))
        val = jnp.where(is_ig, x, logsig)
        val = jnp.where(is_used, val, 0.0)
        csp = jnp.dot(tri, jnp.concatenate(_split3(val), axis=1), preferred_element_type=F32)
        cs = csp[:, 0:LANES] + csp[:, LANES:2 * LANES] + csp[:, 2 * LANES:3 * LANES]
        glob = cs + carry_ref[0:1, :]
        carry_ref[0:1, :] = glob[c - 1:c, :]
        f_t = cs * LOG2E
        b_t = (pltpu.roll(val, GATE_MF - GATE_MI, axis=1) - cs) * LOG2E
        bt = b_t.T
        ct_ref[0, :, rs] = bt[:GATE_USED, :]
        cm = jnp.zeros((c, LANES), F32)
        for h in range(ML_HEADS):
            k = GATE_MF + h
            run_max = jnp.max(jnp.where(causal, bt[k:k + 1, :], -jnp.inf),
                              axis=-1, keepdims=True)
            cm = jnp.where(lane == k, run_max, cm)
        r_ref[rs, 0:LANES] = f_t
        r_ref[rs, LANES:2 * LANES] = b_t
        r_ref[rs, 2 * LANES:3 * LANES] = cm
        terms = _split3(jnp.where(is_fox, glob * (-LOG2E), 0.0))
        cb_ref[rs, :] = jnp.dot(jnp.concatenate(terms, axis=1), pm_ref[...],
                                preferred_element_type=F32).astype(BF16)


def _fox_bias_placement():
    pm = np.zeros((3, LANES, FOX_W), np.float32)
    for h in range(FOX_HEADS):
        base = (h // 2) * LANES + (FOX_HD if h % 2 == 0 else 0)
        for t in range(3):
            pm[t, h, base + t] = 1.0
    return jnp.asarray(pm.reshape(3 * LANES, FOX_W), BF16)


def _gates(gpre, bias, *, batch, seq, tg):
    ns = seq // tg
    rmap = lambda b, s: (b * ns + s, 0)
    return pl.pallas_call(
        functools.partial(_gates_kernel, tg=tg),
        grid=(batch, ns),
        in_specs=[pl.BlockSpec((tg, LANES), rmap),
                  pl.BlockSpec((1, LANES), lambda b, s: (0, 0)),
                  pl.BlockSpec((3 * LANES, FOX_W), lambda b, s: (0, 0))],
        out_specs=[pl.BlockSpec((tg, 3 * LANES), rmap),
                   pl.BlockSpec((1, GATE_USED, tg), lambda b, s: (b, 0, s)),
                   pl.BlockSpec((tg, FOX_W), rmap)],
        out_shape=[jax.ShapeDtypeStruct((batch * seq, 3 * LANES), F32),
                   jax.ShapeDtypeStruct((batch, GATE_USED, seq), F32),
                   jax.ShapeDtypeStruct((batch * seq, FOX_W), BF16)],
        scratch_shapes=[pltpu.VMEM((SUBLANES, LANES), F32)],
        compiler_params=_params("parallel", "arbitrary"),
        name="gates")(gpre, bias, _fox_bias_placement())


def _fox_lane_rows():
    r = np.zeros((SUBLANES, LANES), np.float32)
    r[0, :FOX_HD] = 1.0
    r[1, FOX_HD:] = 1.0
    r[2, FOX_HD:FOX_HD + 3] = 1.0
    r[3, 0:3] = 1.0
    return jnp.asarray(r, BF16)


def _fox_kernel(q_ref, k_ref, v_ref, cb_ref, mk_ref, o_ref, m_ref, acc_ref, *, tq, tk):
    i = pl.program_id(2)
    n_sub = tq // tk
    npp = q_ref.shape[1] // LANES
    lo, hi = mk_ref[0:1, :], mk_ref[1:2, :]
    qh = []
    for pp in range(npp):
        q2 = q_ref[:, pp * LANES:(pp + 1) * LANES]
        qh += [q2 * lo + mk_ref[2:3, :], q2 * hi + mk_ref[3:4, :]]

    tri = (lax.broadcasted_iota(jnp.int32, (tk, tk), 1)
           <= lax.broadcasted_iota(jnp.int32, (tk, tk), 0))

    def step(j, row0, masked, tk=tk, first=False):
        rows = tq - row0
        r0 = pl.multiple_of(j * tk, tk)
        for pp in range(npp):
            ps = slice(pp * LANES, (pp + 1) * LANES)
            ks = k_ref[pl.ds(r0, tk), ps]
            vs = v_ref[pl.ds(r0, tk), ps]
            cb = cb_ref[pl.ds(r0, tk), ps]
            kh = (ks * lo + cb * hi, ks * hi + cb * lo)
            vh = (vs * lo + hi, vs * hi + lo)
            for h in range(2):
                hh = 2 * pp + h
                s = lax.dot_general(qh[hh][row0:, :], kh[h], NT_DIMS,
                                    preferred_element_type=F32)
                if masked:
                    top = jnp.where(tri, s[:tk], -jnp.inf)
                    s = top if rows == tk else jnp.concatenate([top, s[tk:]], axis=0)
                s_max = jnp.max(s, axis=-1, keepdims=True)
                if first:
                    m_new = jnp.broadcast_to(s_max, (rows, LANES))
                else:
                    m_old = m_ref[hh, row0:, :]
                    m_new = jnp.maximum(m_old, s_max)
                pe = jnp.exp2(s - jnp.concatenate([m_new] * (tk // LANES), axis=1))
                pv = jnp.dot(pe.astype(BF16), vh[h], preferred_element_type=F32)
                if first:
                    acc_ref[hh, row0:, :] = pv
                else:
                    acc_ref[hh, row0:, :] = jnp.exp2(m_old - m_new) * acc_ref[hh, row0:, :] + pv
                m_ref[hh, row0:, :] = m_new

    tkf = 2 * tk if tq % (2 * tk) == 0 else tk

    def body(jj, carry):
        for u in range(tq // tkf):
            step(jj * (tq // tkf) + u, 0, False, tkf)
        return carry

    for d in range(n_sub):
        step(i * n_sub + d, d * tk, True, first=(d == 0))
    lax.fori_loop(0, i, body, 0)

    first = lax.broadcasted_iota(jnp.int32, (tq, LANES), 1) < FOX_HD
    for pp in range(npp):
        acc_a, acc_b = acc_ref[2 * pp], acc_ref[2 * pp + 1]
        den = pltpu.roll(jnp.where(first, acc_b, acc_a), FOX_HD, axis=1)
        o_ref[:, pp * LANES:(pp + 1) * LANES] = (
            jnp.where(first, acc_a, acc_b) / den).astype(BF16)


def _fox(proj, cb, *, batch, seq, tq, tk, npp):
    nq = seq // tq
    ngrp = FOX_HEADS // 2 // npp
    w = npp * LANES
    qb, kb, vb = FQ_OFF // w, FK_OFF // w, FV_OFF // w
    return pl.pallas_call(
        functools.partial(_fox_kernel, tq=tq, tk=tk),
        grid=(batch, ngrp, nq),
        in_specs=[
            pl.BlockSpec((tq, w), lambda b, p, i: (b * nq + i, qb + p)),
            pl.BlockSpec((seq, w), lambda b, p, i: (b, kb + p)),
            pl.BlockSpec((seq, w), lambda b, p, i: (b, vb + p)),
            pl.BlockSpec((seq, w), lambda b, p, i: (b, p)),
            pl.BlockSpec((SUBLANES, LANES), lambda b, p, i: (0, 0)),
        ],
        out_specs=pl.BlockSpec((tq, w), lambda b, p, i: (b * nq + i, p)),
        out_shape=jax.ShapeDtypeStruct((batch * seq, FOX_W), BF16),
        scratch_shapes=[pltpu.VMEM((2 * npp, tq, LANES), F32),
                        pltpu.VMEM((2 * npp, tq, LANES), F32)],
        compiler_params=_params("parallel", "parallel", "arbitrary"),
        name="fox_attention")(proj, proj, proj, cb, _fox_lane_rows())


def _mlstm_kernel(u_ref, v_ref, mo_ref, r_ref, ct_ref, sel_ref, cw_ref, cb_ref,
                  nw_ref, o_ref, ubuf, q_s, k_s, st_ref, m_ref, *, tt):
    s_idx = pl.program_id(1)
    halo = SUBLANES

    @pl.when(s_idx == 0)
    def _():
        ubuf[0:halo, :] = jnp.zeros((halo, 2 * ML_W), F32)
        st_ref[...] = jnp.zeros_like(st_ref)
        m_ref[...] = jnp.zeros_like(m_ref)

    @pl.when(s_idx > 0)
    def _():
        ubuf[0:halo, :] = ubuf[tt:tt + halo, :]

    c = ML_CHUNK
    for blk in range(tt // c):
        ubuf[halo + blk * c:halo + (blk + 1) * c, :] = (
            u_ref[blk * c:(blk + 1) * c, :].astype(F32))

    def conv_block(blk):
        acc = cb_ref[...]
        for j in (CONV_K - 1,) + tuple(range(CONV_K - 1)):
            off = halo - (CONV_K - 1) + j + blk * c
            acc = acc + cw_ref[j:j + 1, :] * ubuf[off:off + c, :]
        qk = acc * jax.nn.sigmoid(acc)
        q_s[blk * c:(blk + 1) * c, :] = qk[:, :ML_W].astype(BF16)
        k_s[blk * c:(blk + 1) * c, :] = (qk[:, ML_W:] * (ML_HD ** -0.5)).astype(BF16)

    row = lax.broadcasted_iota(jnp.int32, (c, c), 0)
    col = lax.broadcasted_iota(jnp.int32, (c, c), 1)
    causal = col <= row
    ones_blk = jnp.ones((c, ML_HD), BF16)

    def chunk(ci):
        r0 = ci * c
        cb = ct_ref[0, :, pl.ds(r0, c)]
        rep = []
        for qi in range(3):
            hi, mid, lo = _split3(r_ref[pl.ds(r0, c), qi * LANES:(qi + 1) * LANES])
            rep.append(jnp.dot(jnp.concatenate([hi, mid], axis=1), sel_ref[...],
                               preferred_element_type=F32)
                       + jnp.dot(lo, sel_ref[0:LANES, :], preferred_element_type=F32))
        for h in range(ML_HEADS):
            hs = slice(h * ML_HD, (h + 1) * ML_HD)
            qh = q_s[pl.ds(r0, c), hs]
            kh = k_s[pl.ds(r0, c), hs]
            vh = v_ref[pl.ds(r0, c), hs]
            f_t, b_t, cm_t = rep[0][:, hs], rep[1][:, hs], rep[2][:, hs]
            b_row = cb[GATE_MF + h:GATE_MF + h + 1, :]
            m_prev = m_ref[h:h + 1, :]

            g_t = jnp.maximum(m_prev, cm_t)
            dw = jnp.exp2(jnp.where(causal, b_row - jnp.concatenate([g_t, g_t], axis=1),
                                    -jnp.inf))
            inter_w = jnp.exp2(m_prev - g_t)
            sqk = lax.dot_general(qh, kh, NT_DIMS, preferred_element_type=F32) * dw
            state = st_ref[h]
            qc = jnp.dot(qh, state.astype(BF16), preferred_element_type=F32)
            v_aug = jnp.concatenate([vh, ones_blk], axis=1)
            pv = jnp.dot(sqk.astype(BF16), v_aug, preferred_element_type=F32)
            num = inter_w * qc[:, :ML_HD] + pv[:, :ML_HD]
            den = inter_w * qc[:, ML_HD:] + pv[:, ML_HD:]
            hh = num / jnp.maximum(jnp.abs(den), jnp.exp2(-(f_t + g_t)))
            ms = jnp.mean(hh * hh, axis=-1, keepdims=True)
            y = (hh * lax.rsqrt(ms + EPS) * nw_ref[:, hs]
                 * jax.nn.sigmoid(mo_ref[pl.ds(r0, c), hs].astype(F32)))
            o_ref[pl.ds(r0, c), hs] = y.astype(BF16)

            f_last = f_t[c - 1:c, :]
            m_new = jnp.maximum(f_last + m_prev, f_last + cm_t[c - 1:c, :])
            s_old = jnp.exp2(f_last + m_prev - m_new)
            kw = (kh.astype(F32) * jnp.exp2(f_last + b_t - m_new)).astype(BF16)
            upd = lax.dot_general(kw, v_aug, TN_DIMS, preferred_element_type=F32)
            st_ref[h] = jnp.concatenate([s_old, s_old], axis=1) * state + upd
            m_ref[h:h + 1, :] = m_new

    for ci in range(tt // c):
        conv_block(ci)
        chunk(ci)


def _mlstm(proj, rows, ct, conv_w, conv_b, norm_w, *, batch, seq, tt):
    ns = seq // tt
    ub = MQK_OFF // (2 * ML_W)
    vb = MV_OFF // ML_W
    ob = MO_OFF // ML_W
    rmap = lambda b, s: (b * ns + s, 0)
    sel = np.zeros((2, LANES, ML_W), np.float32)
    for h in range(ML_HEADS):
        sel[:, GATE_MF + h, h * ML_HD:(h + 1) * ML_HD] = 1.0
    sel = jnp.asarray(sel.reshape(2 * LANES, ML_W), BF16)
    return pl.pallas_call(
        functools.partial(_mlstm_kernel, tt=tt),
        grid=(batch, ns),
        in_specs=[
            pl.BlockSpec((tt, 2 * ML_W), lambda b, s: (b * ns + s, ub)),
            pl.BlockSpec((tt, ML_W), lambda b, s: (b * ns + s, vb)),
            pl.BlockSpec((tt, ML_W), lambda b, s: (b * ns + s, ob)),
            pl.BlockSpec((tt, 3 * LANES), rmap),
            pl.BlockSpec((1, GATE_USED, tt), lambda b, s: (b, 0, s)),
            pl.BlockSpec((2 * LANES, ML_W), lambda b, s: (0, 0)),
            pl.BlockSpec((CONV_K, 2 * ML_W), lambda b, s: (0, 0)),
            pl.BlockSpec((1, 2 * ML_W), lambda b, s: (0, 0)),
            pl.BlockSpec((1, ML_W), lambda b, s: (0, 0)),
        ],
        out_specs=pl.BlockSpec((tt, ML_W), rmap),
        out_shape=jax.ShapeDtypeStruct((batch * seq, ML_W), BF16),
        scratch_shapes=[pltpu.VMEM((tt + SUBLANES, 2 * ML_W), F32),
                        pltpu.VMEM((tt, ML_W), BF16),
                        pltpu.VMEM((tt, ML_W), BF16),
                        pltpu.VMEM((ML_HEADS, ML_HD, 2 * ML_HD), F32),
                        pltpu.VMEM((SUBLANES, LANES), F32)],
        compiler_params=_params("parallel", "arbitrary"),
        name="mlstm")(proj, proj, proj, rows, ct, sel, conv_w, conv_b, norm_w)


def _mix_xattn_kernel(yf_ref, ym_ref, gt_ref, bg_ref, h_ref, wf_ref, wm_ref, wmix_ref,
                      g_ref, wq_ref, k_ref, v_ref, wo_ref, o_ref):
    g = jax.nn.sigmoid(gt_ref[...].astype(F32) + bg_ref[...])
    uf = jnp.dot(yf_ref[...], wf_ref[...], preferred_element_type=F32)
    um = jnp.dot(ym_ref[...], wm_ref[...], preferred_element_type=F32)
    merged = g[:, :D_MODEL] * uf + g[:, D_MODEL:] * um
    x = h_ref[...] + jnp.dot(merged.astype(BF16), wmix_ref[...], preferred_element_type=F32)

    xg, r = _rms_split(x, g_ref[...])
    q = jnp.dot(xg, wq_ref[...], preferred_element_type=F32)
    q = (q * (r * (X_HD ** -0.5 * LOG2E))).astype(BF16)
    outs = []
    for hd in range(X_HEADS):
        hs = slice(hd * X_HD, (hd + 1) * X_HD)
        s = lax.dot_general(q[:, hs], k_ref[:, hs], NT_DIMS, preferred_element_type=F32)
        e = jnp.exp2(s - jnp.max(s, axis=-1, keepdims=True))
        outs.append(jnp.dot(e.astype(BF16), v_ref[:, hs], preferred_element_type=F32)
                    / jnp.sum(e, axis=-1, keepdims=True))
    o = jnp.concatenate(outs, axis=1).astype(BF16)
    o_ref[...] = x + jnp.dot(o, wo_ref[...], preferred_element_type=F32)


def _mix_xattn(yf, ym, proj, b_gate, h, wf, wm, wmix, gain, wq, kv, wo, *,
               layer, batch, seq, n_mem, tm):
    ns = seq // tm
    const = lambda b, s: (0, 0)
    row = lambda b, s: (b * ns + s, 0)
    return pl.pallas_call(
        _mix_xattn_kernel, grid=(batch, ns),
        in_specs=[
            pl.BlockSpec((tm, FOX_W), row),
            pl.BlockSpec((tm, ML_W), row),
            pl.BlockSpec((tm, 2 * D_MODEL), lambda b, s: (b * ns + s, G_OFF // (2 * D_MODEL))),
            pl.BlockSpec((1, 2 * D_MODEL), const),
            pl.BlockSpec((tm, D_MODEL), row),
            _layer_block((FOX_W, D_MODEL), layer),
            _layer_block((ML_W, D_MODEL), layer),
            _layer_block((D_MODEL, D_MODEL), layer),
            pl.BlockSpec((1, D_MODEL), const),
            _layer_block((D_MODEL, D_MODEL), layer),
            pl.BlockSpec((None, n_mem, D_MODEL), lambda b, s: (layer, b, 0)),
            pl.BlockSpec((None, n_mem, D_MODEL), lambda b, s: (layer, b, 1)),
            _layer_block((D_MODEL, D_MODEL), layer),
        ],
        out_specs=pl.BlockSpec((tm, D_MODEL), row),
        out_shape=jax.ShapeDtypeStruct((batch * seq, D_MODEL), F32),
        compiler_params=_params("parallel", "parallel"),
        name="mix_cross_attention")(yf, ym, proj, b_gate, h, wf, wm, wmix,
                                    gain, wq, kv, kv, wo)


def _mlp_kernel(h_ref, g_ref, w1_ref, w2_ref, gf_ref, o_ref, *, tf, final):
    x = h_ref[...]
    xg, r = _rms_split(x, g_ref[...])
    acc = x
    for j in range(D_FF // tf):
        fs = slice(j * tf, (j + 1) * tf)
        a = jnp.maximum(r * jnp.dot(xg, w1_ref[:, fs], preferred_element_type=F32), 0.0)
        acc = acc + jnp.dot((a * a).astype(BF16), w2_ref[fs, :], preferred_element_type=F32)
    o_ref[...] = _rms(acc, gf_ref[...]) if final else acc


def _mlp(h, gain, w1, w2, gain_final, *, layer, tm, tf, final):
    m = h.shape[0]
    const = lambda i: (0, 0)
    row = lambda i: (i, 0)
    return pl.pallas_call(
        functools.partial(_mlp_kernel, tf=tf, final=final),
        grid=(m // tm,),
        in_specs=[
            pl.BlockSpec((tm, D_MODEL), row),
            pl.BlockSpec((1, D_MODEL), const),
            _layer_block((D_MODEL, D_FF), layer),
            _layer_block((D_FF, D_MODEL), layer),
            pl.BlockSpec((1, D_MODEL), const),
        ],
        out_specs=pl.BlockSpec((tm, D_MODEL), row),
        out_shape=jax.ShapeDtypeStruct((m, D_MODEL), F32),
        compiler_params=_params("parallel"),
        name="mlp")(h, gain, w1, w2, gain_final)


def _pick(total, prefs):
    for t in prefs:
        if total % t == 0:
            return t
    raise ValueError(f"no tile in {prefs} divides {total}")


def kernel(x, mem, norm_mix, norm_xattn, norm_mem, norm_mlp, w_in, b_fox_f, b_mlstm_i,
           b_mlstm_f, b_gate, conv_w, conv_b, mlstm_norm, w_up_fox, w_up_mlstm, w_mix_out,
           w_xq, w_xkv, w_xo, w_mlp_in, w_mlp_out, norm_final):
    batch, seq, d = x.shape
    n_mem = mem.shape[1]
    depth = w_in.shape[0]
    assert d == D_MODEL and seq % ML_CHUNK == 0
    m = batch * seq

    tm_proj = _pick(m, (512, 256))
    tn_proj = 512
    t_seq = _pick(seq, (2048, 1024, 512, 256))
    tk = _pick(seq, (512, 256))
    tq = _pick(seq, (4 * tk, 2 * tk, tk))
    tm_x = _pick(seq, (512, 256))
    tm_mlp = _pick(m, (512, 256))
    tf = 1024

    sp = [0, FOX_W, 2 * FOX_W, 3 * FOX_W, 3 * FOX_W + FOX_HEADS]
    sp += [sp[-1] + 2 * ML_W, sp[-1] + 3 * ML_W, sp[-1] + 4 * ML_W]
    sp += [sp[-1] + ML_HEADS, sp[-1] + 2 * ML_HEADS, sp[-1] + 2 * ML_HEADS + 2 * D_MODEL]
    w_in_t = jnp.swapaxes(w_in, 1, 2)
    w_in_b = w_in_t.astype(BF16)
    seg = [w_in_b[:, a:b, :] for a, b in zip(sp[:-1], sp[1:])]
    fq, fk, fv, ff, mqk, mv, mo, mi, mf, gp = seg
    fq = (w_in_t[:, sp[0]:sp[1], :] * (FOX_HD ** -0.5 * LOG2E)).astype(BF16)
    w_main = jnp.concatenate([gp, mqk, fq, fk, fv, mv, mo], axis=1)
    w_gate = jnp.concatenate(
        [ff, mi, mf, jnp.zeros((depth, LANES - GATE_USED, d), BF16)], axis=1)
    gate_bias = jnp.concatenate(
        [b_fox_f, b_mlstm_i, b_mlstm_f, jnp.zeros((depth, LANES - GATE_USED), F32)], axis=1)
    bf = lambda w: w.astype(BF16)
    w_up_fox, w_up_mlstm, w_mix_out = bf(w_up_fox), bf(w_up_mlstm), bf(w_mix_out)
    w_xq, w_xkv, w_xo = bf(w_xq), bf(w_xkv), bf(w_xo)
    w_mlp_in, w_mlp_out = bf(w_mlp_in), bf(w_mlp_out)

    h = x.reshape(m, d)
    mem2 = mem.reshape(batch * n_mem, d)
    kv = _norm_proj_layers(mem2, norm_mem[:, None, :], w_xkv,
                           tm=_pick(batch * n_mem, (1024, 512, 256)), tn=1024)
    for l in range(depth):
        proj, gpre = _in_proj(h, norm_mix[l][None], w_main, w_gate,
                              layer=l, tm=tm_proj, tn=tn_proj)
        rows, ct, cb = _gates(gpre, gate_bias[l][None], batch=batch, seq=seq, tg=t_seq)
        y_fox = _fox(proj, cb, batch=batch, seq=seq, tq=tq, tk=tk, npp=1)
        y_ml = _mlstm(proj, rows, ct, conv_w[l], conv_b[l][None], mlstm_norm[l][None],
                      batch=batch, seq=seq, tt=t_seq)
        h = _mix_xattn(y_fox, y_ml, proj, b_gate[l][None], h,
                       w_up_fox, w_up_mlstm, w_mix_out,
                       norm_xattn[l][None], w_xq, kv, w_xo,
                       layer=l, batch=batch, seq=seq, n_mem=n_mem, tm=tm_x)
        h = _mlp(h, norm_mlp[l][None], w_mlp_in, w_mlp_out, norm_final[None],
                 layer=l, tm=tm_mlp, tf=tf, final=(l == depth - 1))
    return h.reshape(batch, seq, d)
```
